```python
import jax
import jax.numpy as jnp
from jax import lax
import numpy as np

D_MODEL = 2048
BATCH = 32
SEQ = 256
DEPTH = 4
DEC_BATCH = 2
DEC_SEQ = 4096
PAST_LEN = 256

GRID_W = 64
N_MIXERS = 4
N_MOD = 6
NORM_EPS = 1e-6
MLA_HEADS = 16
Q_LORA = 768
KV_LORA = 512
QK_NOPE = 128
QK_ROPE = 64
V_DIM = 128
ROPE_PAIRS_AXIS = QK_ROPE // 4
ROPE_THETA = 10000.0
Q_BLOCK = 128
CONV_W = 31
CONV_PAD = CONV_W // 2
CHUNK = 128
D_SGU = D_MODEL
SGU_GROUPS = 16
SGU_GROUP_DIM = D_SGU // SGU_GROUPS
RWKV_N = 64
RWKV_H = D_MODEL // RWKV_N
DECAY_LORA = 96
A_LORA = 96
GATE_LORA = 256
GN_EPS = 64e-5
D_FF = 5504
N_EXPERTS = 8
TOP_K = 2
D_FF_EXPERT = 7168
MOE_BLOCK = 128

kernel_name = 'hybrid_diffusion_prefix_trunk_step'


def rmsnorm(x, g):
    xf = x.astype(jnp.float32)
    y = xf * lax.rsqrt(jnp.mean(xf * xf, axis=-1, keepdims=True) + NORM_EPS)
    return y.astype(x.dtype) * g


def layernorm(x, g, b, eps=1e-5):
    xf = x.astype(jnp.float32)
    mu = jnp.mean(xf, axis=-1, keepdims=True)
    var = jnp.mean(jnp.square(xf - mu), axis=-1, keepdims=True)
    return ((xf - mu) * lax.rsqrt(var + eps)).astype(x.dtype) * g + b


def ada_mod(cond, w_mod, b_mod):
    m = jax.nn.silu(cond) @ w_mod + b_mod
    return jnp.split(m[:, None, :], N_MOD, axis=-1)


def modulate(x, shift, scale):
    return x * (1 + scale) + shift


def swiglu(h, w_gate, w_up, w_down):
    return (jax.nn.silu(h @ w_gate) * (h @ w_up)) @ w_down


def moe_swiglu(h, w_router, b_router, w_gate, w_up, w_down):
    bsz, n, d = h.shape
    n_tok = bsz * n
    x2 = h.reshape(n_tok, d)
    logits = (x2 @ w_router).astype(jnp.float32) + b_router.astype(jnp.float32)
    top_logit, top_e = lax.top_k(logits, TOP_K)
    top_w = jax.nn.softmax(top_logit, axis=-1)
    nk = n_tok * TOP_K
    flat_e = top_e.reshape(nk)
    order = jnp.argsort(flat_e)
    e_sorted = flat_e[order]
    tok_sorted = (jnp.arange(nk) // TOP_K)[order]
    w_sorted = top_w.reshape(nk)[order]
    counts = jnp.zeros((N_EXPERTS,), jnp.int32).at[flat_e].add(1)
    padded = (counts + MOE_BLOCK - 1) // MOE_BLOCK * MOE_BLOCK
    start = jnp.cumsum(counts) - counts
    pad_end = jnp.cumsum(padded)
    pad_start = pad_end - padded
    dest = pad_start[e_sorted] + jnp.arange(nk) - start[e_sorted]
    n_blocks = -(-nk // MOE_BLOCK) + N_EXPERTS
    n_slots = n_blocks * MOE_BLOCK
    slot_tok = jnp.zeros((n_slots,), jnp.int32).at[dest].set(tok_sorted)
    slot_w = jnp.zeros((n_slots,), jnp.float32).at[dest].set(w_sorted)
    block_e = jnp.minimum(jnp.searchsorted(pad_end, jnp.arange(n_blocks) * MOE_BLOCK, side='right'), N_EXPERTS - 1)
    xs = x2[slot_tok].reshape(n_blocks, MOE_BLOCK, d)

    def expert_block(args):
        xb, e = args
        return (jax.nn.silu(xb @ w_gate[e]) * (xb @ w_up[e])) @ w_down[e]

    ys = lax.map(expert_block, (xs, block_e)).reshape(n_slots, d)
    ys = ys * slot_w[:, None].astype(ys.dtype)
    out = jax.ops.segment_sum(ys, slot_tok, num_segments=n_tok)
    return out.reshape(bsz, n, d)


def axial_rope_tables(n_tok, dtype):
    rows = n_tok // GRID_W
    row_pos = jnp.broadcast_to(jnp.arange(rows, dtype=jnp.float32)[:, None], (rows, GRID_W)).reshape(-1)
    col_pos = jnp.broadcast_to(jnp.arange(GRID_W, dtype=jnp.float32)[None, :], (rows, GRID_W)).reshape(-1)
    inv_freq = ROPE_THETA ** (-jnp.arange(ROPE_PAIRS_AXIS, dtype=jnp.float32) / ROPE_PAIRS_AXIS)
    ang = jnp.concatenate([row_pos[:, None] * inv_freq, col_pos[:, None] * inv_freq], axis=-1)
    return jnp.cos(ang).astype(dtype), jnp.sin(ang).astype(dtype)


def apply_rope(x, cos, sin):
    x2 = x.reshape(*x.shape[:-1], -1, 2)
    xr, xi = x2[..., 0], x2[..., 1]
    out = jnp.stack([xr * cos - xi * sin, xr * sin + xi * cos], axis=-1)
    return out.reshape(x.shape)


def mla_project(h, mp):
    bsz, n, _ = h.shape
    c_q, c_kv, k_rope = jnp.split(h @ mp['w_down'], [Q_LORA, Q_LORA + KV_LORA], axis=-1)
    q = (rmsnorm(c_q, mp['q_norm']) @ mp['w_uq']).reshape(bsz, n, MLA_HEADS, QK_NOPE + QK_ROPE)
    return q, rmsnorm(c_kv, mp['kv_norm']), k_rope


def mla_expand(c_kv, k_rope, w_ukv):
    bsz, n, _ = c_kv.shape
    kv = (c_kv @ w_ukv).reshape(bsz, n, MLA_HEADS, QK_NOPE + V_DIM)
    k_nope, v = jnp.split(kv, [QK_NOPE], axis=-1)
    k_pe = jnp.broadcast_to(k_rope[:, :, None, :], (bsz, n, MLA_HEADS, QK_ROPE))
    return jnp.concatenate([k_nope, k_pe], axis=-1), v


def blocked_attention(q, k, v):
    bsz, n, nh, dqk = q.shape
    scale = dqk ** -0.5
    qb = q.reshape(bsz, n // Q_BLOCK, Q_BLOCK, nh, dqk).transpose(1, 0, 2, 3, 4)

    def one_block(q_blk):
        s = jnp.einsum('bqhd,bkhd->bhqk', q_blk, k).astype(jnp.float32) * scale
        p = jax.nn.softmax(s, axis=-1).astype(v.dtype)
        return jnp.einsum('bhqk,bkhd->bqhd', p, v)

    o = lax.map(one_block, qb)
    return o.transpose(1, 0, 2, 3, 4).reshape(bsz, n, nh * v.shape[-1])


def mla_context(h, mp):
    q, c_kv, k_rope = mla_project(h, mp)
    k, v = mla_expand(c_kv, k_rope, mp['w_ukv'])
    return blocked_attention(q, k, v) @ mp['w_o'], (c_kv, k_rope)


def mla_latent(h, ckv_ctx, krope_ctx, mp):
    n = h.shape[1]
    q, c_kv, k_rope = mla_project(h, mp)
    cos, sin = axial_rope_tables(n, h.dtype)
    q = jnp.concatenate([q[..., :QK_NOPE], apply_rope(q[..., QK_NOPE:], cos[:, None, :], sin[:, None, :])], axis=-1)
    k_lat, v_lat = mla_expand(c_kv, apply_rope(k_rope, cos, sin), mp['w_ukv'])
    k_ctx, v_ctx = mla_expand(ckv_ctx.astype(h.dtype), krope_ctx.astype(h.dtype), mp['w_ukv'])
    k = jnp.concatenate([k_lat, k_ctx], axis=1)
    v = jnp.concatenate([v_lat, v_ctx], axis=1)
    return blocked_attention(q, k, v) @ mp['w_o']


def conformer_conv(h, mp):
    u = jax.nn.glu(h @ mp['w_in'] + mp['b_in'], axis=-1)
    dw = lax.conv_general_dilated(u, mp['w_dw'][:, None, :], window_strides=(1,), padding=[(CONV_PAD, CONV_PAD)],
                                  dimension_numbers=('NWC', 'WIO', 'NWC'), feature_group_count=D_MODEL)
    dw = dw + mp['b_dw']
    return jax.nn.silu(layernorm(dw, mp['ln_g'], mp['ln_b'])) @ mp['w_out'] + mp['b_out']


def chunk_sgu(h, mp):
    bsz, n, _ = h.shape
    u, v = jnp.split(jax.nn.gelu(h @ mp['w_in'] + mp['b_in']), 2, axis=-1)
    v = layernorm(v, mp['ln_g'], mp['ln_b'])
    vc = v.reshape(bsz, n // CHUNK, CHUNK, SGU_GROUPS, SGU_GROUP_DIM)
    mixed = jnp.einsum('gpq,bnqgc->bnpgc', mp['w_s'], vc) + jnp.transpose(mp['b_s'])[None, None, :, :, None]
    return (u * mixed.reshape(bsz, n, D_SGU)) @ mp['w_out'] + mp['b_out']


def token_shift_delta(x):
    zero = jnp.zeros_like(x[:, :1])
    prev = jnp.concatenate([zero, x[:, :-1]], axis=1)
    nxt = jnp.concatenate([x[:, 1:], zero], axis=1)
    return 0.5 * (prev + nxt) - x


def wkv_scan(s0, r, decay, kk, a, k_t, v, reverse):
    def step(s, inp):
        r_t, w_t, kk_t, a_t, k_tt, v_t = inp
        sa = jnp.einsum('bhvk,bhk->bhv', s, -kk_t)
        s = s * w_t[:, :, None, :] + sa[..., None] * (kk_t * a_t)[:, :, None, :] + v_t[..., None] * k_tt[:, :, None, :]
        return s, jnp.einsum('bhvk,bhk->bhv', s, r_t)

    seq = tuple(jnp.moveaxis(t, 1, 0) for t in (r, decay, kk, a, k_t, v))
    s_final, ys = lax.scan(step, s0, seq, reverse=reverse)
    return jnp.moveaxis(ys, 0, 1), s_final


def rwkv7_mix(h, s_fwd0, s_bwd0, mp):
    bsz, n, _ = h.shape
    f32 = jnp.float32

    def heads(t):
        return t.astype(f32).reshape(bsz, n, RWKV_H, RWKV_N)

    xx = token_shift_delta(h)
    xr, xw, xk, xv, xa, xg = (h + xx * mp['mu'][j] for j in range(6))
    r = heads(xr @ mp['w_r'])
    k = heads(xk @ mp['w_k'])
    v = heads(xv @ mp['w_v'])
    g = jax.nn.sigmoid(xg @ mp['g1']) @ mp['g2']
    kk = k * mp['k_k'].astype(f32).reshape(RWKV_H, RWKV_N)
    kk = kk / jnp.maximum(jnp.sqrt(jnp.sum(kk * kk, axis=-1, keepdims=True)), 1e-12)
    k_a = mp['k_a'].astype(f32).reshape(RWKV_H, RWKV_N)
    r_k = mp['r_k'].astype(f32)
    y = jnp.zeros_like(r)
    bonus = jnp.zeros_like(r)
    finals = []
    for d, (s0, rev) in enumerate(((s_fwd0, False), (s_bwd0, True))):
        w_log = -jax.nn.softplus(-(mp['w0'][d] + jnp.tanh(xw @ mp['w1'][d]) @ mp['w2'][d])) - 0.5
        decay = jnp.exp(-jnp.exp(heads(w_log)))
        a = jax.nn.sigmoid(heads(mp['a0'][d] + (xa @ mp['a1'][d]) @ mp['a2'][d]))
        k_t = k * (1 + (a - 1) * k_a)
        y_d, s_d = wkv_scan(s0.astype(f32), r, decay, kk, a, k_t, v, rev)
        y = y + y_d
        bonus = bonus + jnp.sum(r * k_t * r_k, axis=-1, keepdims=True) * v
        finals.append(s_d)
    mu = jnp.mean(y, axis=-1, keepdims=True)
    var = jnp.mean(jnp.square(y - mu), axis=-1, keepdims=True)
    yn = ((y - mu) * lax.rsqrt(var + GN_EPS)).reshape(bsz, n, D_MODEL)
    o = yn * mp['ln_g'].astype(f32) + mp['ln_b'].astype(f32) + bonus.reshape(bsz, n, D_MODEL)
    o = (o.astype(h.dtype) * g) @ mp['w_o']
    return o, (finals[0], finals[1])


def setup_inputs(seed: int = 0) -> dict:
    key = jax.random.key(seed)
    ks = iter(jax.random.split(key, 128))
    f32 = jnp.float32
    D = D_MODEL

    def nrm(shape, scale):
        return scale * jax.random.normal(next(ks), shape, f32)

    def lin(shape, fan_in):
        return nrm(shape, fan_in ** -0.5)

    def gain(n):
        return 1.0 + nrm((n,), 0.02)

    def bias(shape):
        return nrm(shape, 0.02)

    inp = {}
    inp['x_prompt'] = nrm((BATCH, SEQ, D), 1.0)
    inp['x_sample'] = nrm((DEC_BATCH, DEC_SEQ, D), 1.0)
    inp['cache_ckv_l0'] = nrm((DEC_BATCH, PAST_LEN, KV_LORA), 1.0)
    inp['cache_krope_l0'] = nrm((DEC_BATCH, PAST_LEN, QK_ROPE), 1.0)
    inp['state_wkv_fwd_l3'] = nrm((DEC_BATCH, RWKV_H, RWKV_N, RWKV_N), 0.3)
    inp['state_wkv_bwd_l3'] = nrm((DEC_BATCH, RWKV_H, RWKV_N, RWKV_N), 0.3)
    inp['c'] = nrm((DEC_BATCH, D), 1.0)
    inp['c_ctx'] = nrm((D,), 1.0)

    def common(p):
        inp[p + 'w_mod'] = nrm((D, N_MOD * D), 0.5 * D ** -0.5)
        inp[p + 'b_mod'] = bias((N_MOD * D,))
        inp[p + 'norm_mix'] = gain(D)
        inp[p + 'norm_ffn'] = gain(D)

    def dense_ffn(p):
        inp[p + 'ffn_w_gate'] = lin((D, D_FF), D)
        inp[p + 'ffn_w_up'] = lin((D, D_FF), D)
        inp[p + 'ffn_w_down'] = lin((D_FF, D), D_FF)

    def moe_ffn(p):
        inp[p + 'moe_w_router'] = lin((D, N_EXPERTS), D)
        inp[p + 'moe_b_router'] = nrm((N_EXPERTS,), 0.01)
        inp[p + 'moe_w_gate'] = lin((N_EXPERTS, D, D_FF_EXPERT), D)
        inp[p + 'moe_w_up'] = lin((N_EXPERTS, D, D_FF_EXPERT), D)
        inp[p + 'moe_w_down'] = lin((N_EXPERTS, D_FF_EXPERT, D), D_FF_EXPERT)

    common('l0_')
    inp['l0_mla_w_down'] = lin((D, Q_LORA + KV_LORA + QK_ROPE), D)
    inp['l0_mla_q_norm'] = gain(Q_LORA)
    inp['l0_mla_kv_norm'] = gain(KV_LORA)
    inp['l0_mla_w_uq'] = lin((Q_LORA, MLA_HEADS * (QK_NOPE + QK_ROPE)), Q_LORA)
    inp['l0_mla_w_ukv'] = lin((KV_LORA, MLA_HEADS * (QK_NOPE + V_DIM)), KV_LORA)
    inp['l0_mla_w_o'] = lin((MLA_HEADS * V_DIM, D), MLA_HEADS * V_DIM)
    dense_ffn('l0_')
    common('l1_')
    inp['l1_conv_w_in'] = lin((D, 2 * D), D)
    inp['l1_conv_b_in'] = bias((2 * D,))
    inp['l1_conv_w_dw'] = lin((CONV_W, D), CONV_W)
    inp['l1_conv_b_dw'] = bias((D,))
    inp['l1_conv_ln_g'] = gain(D)
    inp['l1_conv_ln_b'] = bias((D,))
    inp['l1_conv_w_out'] = lin((D, D), D)
    inp['l1_conv_b_out'] = bias((D,))
    moe_ffn('l1_')
    common('l2_')
    inp['l2_sgu_w_in'] = lin((D, 2 * D_SGU), D)
    inp['l2_sgu_b_in'] = bias((2 * D_SGU,))
    inp['l2_sgu_ln_g'] = gain(D_SGU)
    inp['l2_sgu_ln_b'] = bias((D_SGU,))
    inp['l2_sgu_w_s'] = lin((SGU_GROUPS, CHUNK, CHUNK), CHUNK)
    inp['l2_sgu_b_s'] = 1.0 + nrm((SGU_GROUPS, CHUNK), 0.02)
    inp['l2_sgu_w_out'] = lin((D_SGU, D), D_SGU)
    inp['l2_sgu_b_out'] = bias((D,))
    dense_ffn('l2_')
    common('l3_')
    inp['l3_rwkv_mu'] = jax.random.uniform(next(ks), (6, D), f32)
    inp['l3_rwkv_w_r'] = lin((D, D), D)
    inp['l3_rwkv_w_k'] = lin((D, D), D)
    inp['l3_rwkv_w_v'] = lin((D, D), D)
    inp['l3_rwkv_w_o'] = lin((D, D), D)
    inp['l3_rwkv_w0'] = -1.0 + nrm((2, D), 0.5)
    inp['l3_rwkv_w1'] = lin((2, D, DECAY_LORA), D)
    inp['l3_rwkv_w2'] = nrm((2, DECAY_LORA, D), 0.5 * DECAY_LORA ** -0.5)
    inp['l3_rwkv_a0'] = nrm((2, D), 0.3)
    inp['l3_rwkv_a1'] = lin((2, D, A_LORA), D)
    inp['l3_rwkv_a2'] = nrm((2, A_LORA, D), 0.5 * A_LORA ** -0.5)
    inp['l3_rwkv_g1'] = lin((D, GATE_LORA), D)
    inp['l3_rwkv_g2'] = lin((GATE_LORA, D), GATE_LORA)
    inp['l3_rwkv_k_k'] = 0.85 + nrm((D,), 0.05)
    inp['l3_rwkv_k_a'] = 1.0 + nrm((D,), 0.05)
    inp['l3_rwkv_r_k'] = nrm((RWKV_H, RWKV_N), 0.1)
    inp['l3_rwkv_ln_g'] = gain(D)
    inp['l3_rwkv_ln_b'] = bias((D,))
    moe_ffn('l3_')
    inp['norm_out'] = gain(D)
    return inp


def reference(x_prompt, x_sample, cache_ckv_l0, cache_krope_l0, state_wkv_fwd_l3, state_wkv_bwd_l3, c, c_ctx,
              l0_w_mod, l0_b_mod, l0_norm_mix, l0_norm_ffn,
              l0_mla_w_down, l0_mla_q_norm, l0_mla_kv_norm, l0_mla_w_uq, l0_mla_w_ukv, l0_mla_w_o,
              l0_ffn_w_gate, l0_ffn_w_up, l0_ffn_w_down,
              l1_w_mod, l1_b_mod, l1_norm_mix, l1_norm_ffn,
              l1_conv_w_in, l1_conv_b_in, l1_conv_w_dw, l1_conv_b_dw, l1_conv_ln_g, l1_conv_ln_b, l1_conv_w_out, l1_conv_b_out,
              l1_moe_w_router, l1_moe_b_router, l1_moe_w_gate, l1_moe_w_up, l1_moe_w_down,
              l2_w_mod, l2_b_mod, l2_norm_mix, l2_norm_ffn,
              l2_sgu_w_in, l2_sgu_b_in, l2_sgu_ln_g, l2_sgu_ln_b, l2_sgu_w_s, l2_sgu_b_s, l2_sgu_w_out, l2_sgu_b_out,
              l2_ffn_w_gate, l2_ffn_w_up, l2_ffn_w_down,
              l3_w_mod, l3_b_mod, l3_norm_mix, l3_norm_ffn,
              l3_rwkv_mu, l3_rwkv_w_r, l3_rwkv_w_k, l3_rwkv_w_v, l3_rwkv_w_o,
              l3_rwkv_w0, l3_rwkv_w1, l3_rwkv_w2, l3_rwkv_a0, l3_rwkv_a1, l3_rwkv_a2,
              l3_rwkv_g1, l3_rwkv_g2, l3_rwkv_k_k, l3_rwkv_k_a, l3_rwkv_r_k, l3_rwkv_ln_g, l3_rwkv_ln_b,
              l3_moe_w_router, l3_moe_b_router, l3_moe_w_gate, l3_moe_w_up, l3_moe_w_down,
              norm_out):
    layers = [
        dict(w_mod=l0_w_mod, b_mod=l0_b_mod, norm_mix=l0_norm_mix, norm_ffn=l0_norm_ffn,
             mix=dict(w_down=l0_mla_w_down, q_norm=l0_mla_q_norm, kv_norm=l0_mla_kv_norm,
                      w_uq=l0_mla_w_uq, w_ukv=l0_mla_w_ukv, w_o=l0_mla_w_o),
             ffn=(l0_ffn_w_gate, l0_ffn_w_up, l0_ffn_w_down)),
        dict(w_mod=l1_w_mod, b_mod=l1_b_mod, norm_mix=l1_norm_mix, norm_ffn=l1_norm_ffn,
             mix=dict(w_in=l1_conv_w_in, b_in=l1_conv_b_in, w_dw=l1_conv_w_dw, b_dw=l1_conv_b_dw,
                      ln_g=l1_conv_ln_g, ln_b=l1_conv_ln_b, w_out=l1_conv_w_out, b_out=l1_conv_b_out),
             ffn=(l1_moe_w_router, l1_moe_b_router, l1_moe_w_gate, l1_moe_w_up, l1_moe_w_down)),
        dict(w_mod=l2_w_mod, b_mod=l2_b_mod, norm_mix=l2_norm_mix, norm_ffn=l2_norm_ffn,
             mix=dict(w_in=l2_sgu_w_in, b_in=l2_sgu_b_in, ln_g=l2_sgu_ln_g, ln_b=l2_sgu_ln_b,
                      w_s=l2_sgu_w_s, b_s=l2_sgu_b_s, w_out=l2_sgu_w_out, b_out=l2_sgu_b_out),
             ffn=(l2_ffn_w_gate, l2_ffn_w_up, l2_ffn_w_down)),
        dict(w_mod=l3_w_mod, b_mod=l3_b_mod, norm_mix=l3_norm_mix, norm_ffn=l3_norm_ffn,
             mix=dict(mu=l3_rwkv_mu, w_r=l3_rwkv_w_r, w_k=l3_rwkv_w_k, w_v=l3_rwkv_w_v, w_o=l3_rwkv_w_o,
                      w0=l3_rwkv_w0, w1=l3_rwkv_w1, w2=l3_rwkv_w2, a0=l3_rwkv_a0, a1=l3_rwkv_a1, a2=l3_rwkv_a2,
                      g1=l3_rwkv_g1, g2=l3_rwkv_g2, k_k=l3_rwkv_k_k, k_a=l3_rwkv_k_a, r_k=l3_rwkv_r_k,
                      ln_g=l3_rwkv_ln_g, ln_b=l3_rwkv_ln_b),
             ffn=(l3_moe_w_router, l3_moe_b_router, l3_moe_w_gate, l3_moe_w_up, l3_moe_w_down)),
    ]
    layer_cache = [(cache_ckv_l0, cache_krope_l0), (), (), (state_wkv_fwd_l3, state_wkv_bwd_l3)]

    y_p = x_prompt
    y_s = x_sample
    new_state = []
    for i in range(DEPTH):
        lp = layers[i]
        mp = lp['mix']
        mc = ada_mod(c_ctx[None, :], lp['w_mod'], lp['b_mod'])
        ms = ada_mod(c, lp['w_mod'], lp['b_mod'])
        hc = modulate(rmsnorm(y_p, lp['norm_mix']), mc[0], mc[1])
        hs = modulate(rmsnorm(y_s, lp['norm_mix']), ms[0], ms[1])
        kind = i % N_MIXERS
        if kind == 0:
            oc, ctx_state = mla_context(hc, mp)
            os_ = mla_latent(hs, *layer_cache[i], mp)
        elif kind == 1:
            oc, ctx_state = conformer_conv(hc, mp), ()
            os_ = conformer_conv(hs, mp)
        elif kind == 2:
            oc, ctx_state = chunk_sgu(hc, mp), ()
            os_ = chunk_sgu(hs, mp)
        else:
            zero_state = jnp.zeros((y_p.shape[0], RWKV_H, RWKV_N, RWKV_N), jnp.float32)
            oc, ctx_state = rwkv7_mix(hc, zero_state, zero_state, mp)
            os_, _ = rwkv7_mix(hs, *layer_cache[i], mp)
        new_state.extend(ctx_state)
        y_p = y_p + mc[2] * oc
        y_s = y_s + ms[2] * os_
        hc = modulate(rmsnorm(y_p, lp['norm_ffn']), mc[3], mc[4])
        hs = modulate(rmsnorm(y_s, lp['norm_ffn']), ms[3], ms[4])
        ffn = swiglu if i % 2 == 0 else moe_swiglu
        y_p = y_p + mc[5] * ffn(hc, *lp['ffn'])
        y_s = y_s + ms[5] * ffn(hs, *lp['ffn'])
    y_prompt = rmsnorm(y_p, norm_out)
    y_sample = rmsnorm(y_s, norm_out)
    return (y_prompt, y_sample, *new_state)
```

```python
import functools
import math

import jax
import jax.numpy as jnp
import numpy as np
from jax import lax
from jax.experimental import pallas as pl
from jax.experimental.pallas import tpu as pltpu

F32 = jnp.float32
BF16 = jnp.bfloat16

NORM_EPS = 1e-6
LN_EPS = 1e-5
GN_EPS = 64e-5
QK_NOPE = 128
QK_ROPE = 64
V_DIM = 128
ROPE_THETA = 10000.0
GRID_W = 64
RWKV_N = 64
N_MOD = 6
TOP_K = 2

LANES = 128
SUBLANES = 8
VMEM_LIMIT_BYTES = 56 * 1024 * 1024

WKV_CHUNK = 64
WKV_PAIRS_PER_STEP = 4
MOE_ROWS = 512


def _cp(n_axes, arbitrary_last=False):
    sem = ["parallel"] * n_axes
    if arbitrary_last:
        sem[-1] = "arbitrary"
    return pltpu.CompilerParams(dimension_semantics=tuple(sem), vmem_limit_bytes=VMEM_LIMIT_BYTES)


def _tile(pref, *sizes):
    t = pref
    while any(s % t for s in sizes):
        t //= 2
    assert t >= SUBLANES
    return t


class _Lay:
    def __init__(self, n_ctx_seq, ctx_len, n_lat_seq, lat_len):
        self.n_ctx_seq, self.ctx_len, self.n_lat_seq, self.lat_len = n_ctx_seq, ctx_len, n_lat_seq, lat_len
        self.n_ctx = n_ctx_seq * ctx_len
        self.n_tok = self.n_ctx + n_lat_seq * lat_len

    def tile(self, pref):
        return _tile(pref, self.n_ctx, self.lat_len)

    def group(self, start):
        return jnp.where(start < self.n_ctx, 0, 1 + (start - self.n_ctx) // self.lat_len)

    def seq_pos(self, start):
        is_ctx = start < self.n_ctx
        pos = jnp.where(is_ctx, start % self.ctx_len, (start - self.n_ctx) % self.lat_len)
        return pos, jnp.where(is_ctx, self.ctx_len, self.lat_len)


def _silu(x):
    return x * jax.nn.sigmoid(x)


def _gelu_tanh(x):
    return 0.5 * x * (1.0 + jnp.tanh(math.sqrt(2.0 / math.pi) * (x + 0.044715 * (x * x * x))))


def _softplus(x):
    return jnp.maximum(x, 0.0) + jnp.log(1.0 + jnp.exp(-jnp.abs(x)))


def _split_bf16(x, n):
    parts = []
    r = x
    for _ in range(n):
        h = r.astype(BF16)
        parts.append(h)
        r = r - h.astype(F32)
    return parts


def _dot(a, b):
    return jnp.dot(a, b, preferred_element_type=F32)


def _dot_nt(a, b):
    return lax.dot_general(a, b, (((1,), (1,)), ((), ())), preferred_element_type=F32)


def _dot_tn(a, b):
    return lax.dot_general(a, b, (((0,), (0,)), ((), ())), preferred_element_type=F32)


def _pair_ones():
    r = lax.broadcasted_iota(jnp.int32, (LANES, LANES), 0) // RWKV_N
    c = lax.broadcasted_iota(jnp.int32, (LANES, LANES), 1) // RWKV_N
    return jnp.where(r == c, 1.0, 0.0).astype(BF16)


def _head_sum(x):
    ones = _pair_ones()
    cols = []
    for t in range(x.shape[1] // LANES):
        xt = x[:, t * LANES:(t + 1) * LANES]
        cols.append(sum(_dot(p, ones) for p in _split_bf16(xt, 3)))
    return jnp.concatenate(cols, axis=1)


def _mm(a, ws, *, tm, tn, gn, epi, outs, vecs=(), wcols=None, n_outer=False, name="mm"):
    m, k = a.shape
    gm = m // tm
    nw, nv = len(ws), len(vecs)
    wcols = wcols or [lambda j: j] * nw
    if n_outer:
        grid = (gn, gm)
        wrap = lambda fn: (lambda g0, g1: fn(g1, g0))
    else:
        grid = (gm, gn)
        wrap = lambda fn: (lambda g0, g1: fn(g0, g1))
    in_specs = [pl.BlockSpec((tm, k), wrap(lambda i, j: (i, 0)))]
    for wc in wcols:
        in_specs.append(pl.BlockSpec((k, tn), wrap(lambda i, j, wc=wc: (0, wc(j)))))
    for _, bshape, fn in vecs:
        in_specs.append(pl.BlockSpec(bshape, wrap(fn)))
    out_shape = [jax.ShapeDtypeStruct(s, d) for s, d, _, _ in outs]
    out_specs = [pl.BlockSpec(b, wrap(fn)) for _, _, b, fn in outs]

    def body(*refs):
        a_ref = refs[0]
        w_refs = refs[1:1 + nw]
        v_refs = refs[1 + nw:1 + nw + nv]
        o_refs = refs[1 + nw + nv:]
        av = a_ref[...]
        if av.dtype != BF16:
            av = av.astype(BF16)
        accs = [_dot(av, w[...].astype(BF16)) for w in w_refs]
        for o, r in zip(o_refs, epi(accs, v_refs)):
            o[...] = r.reshape(o.shape).astype(o.dtype)

    res = pl.pallas_call(body, grid=grid, in_specs=in_specs, out_specs=out_specs, out_shape=out_shape,
                         compiler_params=_cp(2), name=name)(a, *ws, *[v[0] for v in vecs])
    return res


def _bias_vec(b, tn):
    return (b.reshape(1, -1), (1, tn), lambda i, j: (0, j))


def _mm_act(a, w, bias, act, out_dtype, *, tm, tn, name, n_outer=False):
    m, n = a.shape[0], w.shape[1]
    vecs = [] if bias is None else [_bias_vec(bias, tn)]

    def epi(accs, v):
        x = accs[0]
        if bias is not None:
            x = x + v[0][...]
        return [act(x) if act is not None else x]

    return _mm(a, [w], tm=tm, tn=tn, gn=pl.cdiv(n, tn), epi=epi, vecs=vecs, n_outer=n_outer, name=name,
               outs=[((m, n), out_dtype, (tm, tn), lambda i, j: (i, j))])[0]


def _mm_dual(a, w1, w2, b1, b2, fn, out_dtype, *, n, col2, tm, tn, name, n_outer=True):
    m = a.shape[0]
    gn = pl.cdiv(n, tn)
    vecs = []
    if b1 is not None:
        vecs = [(b1.reshape(1, -1), (1, tn), lambda i, j: (0, j)),
                (b2.reshape(1, -1), (1, tn), lambda i, j: (0, j + col2))]

    def epi(accs, v):
        x, y = accs
        if b1 is not None:
            x, y = x + v[0][...], y + v[1][...]
        return [fn(x, y)]

    return _mm(a, [w1, w2], tm=tm, tn=tn, gn=gn, epi=epi, vecs=vecs, n_outer=n_outer, name=name,
               wcols=[lambda j: j, lambda j: j + col2],
               outs=[((m, n), out_dtype, (tm, tn), lambda i, j: (i, j))])[0]


def _mm_resid(a, w, bias, resid, mod3, gate_row, lay, *, tm, tn, name, n_outer=False):
    m, n = resid.shape
    vecs = [(mod3, (1, N_MOD, tn), lambda i, j: (lay.group(i * tm), 0, j)),
            (resid, (tm, tn), lambda i, j: (i, j))]
    if bias is not None:
        vecs.append(_bias_vec(bias, tn))

    def epi(accs, v):
        x = accs[0]
        if bias is not None:
            x = x + v[2][...]
        return [v[1][...] + v[0][0, gate_row:gate_row + 1, :] * x]

    return _mm(a, [w], tm=tm, tn=tn, gn=n // tn, epi=epi, vecs=vecs, n_outer=n_outer, name=name,
               outs=[((m, n), F32, (tm, tn), lambda i, j: (i, j))])[0]


def _ada_mod(cond8, w_mod, b_mod):
    d, n = w_mod.shape
    tn = _tile(1536, n)

    def body(c_ref, w_ref, b_ref, o_ref):
        c = c_ref[...]
        o_ref[...] = _dot(_silu(c).astype(BF16), w_ref[...].astype(BF16)) + b_ref[...]

    m = pl.pallas_call(
        body, grid=(n // tn,),
        in_specs=[pl.BlockSpec((SUBLANES, d), lambda j: (0, 0)), pl.BlockSpec((d, tn), lambda j: (0, j)),
                  pl.BlockSpec((1, tn), lambda j: (0, j))],
        out_specs=pl.BlockSpec((SUBLANES, tn), lambda j: (0, j)),
        out_shape=jax.ShapeDtypeStruct((SUBLANES, n), F32), compiler_params=_cp(1), name="ada_mod",
    )(cond8, w_mod, b_mod.reshape(1, n))
    return m.reshape(SUBLANES, N_MOD, d)


def _norm_mod(x, gain, mod3, rows, lay, out_dtype, *, router=None, name="norm_mod"):
    m, d = x.shape
    tm = lay.tile(256)
    n_e = None if router is None else router[3]

    def body(*refs):
        x_ref, g_ref = refs[0], refs[1]
        pos = 2
        xf = x_ref[...]
        y = xf * lax.rsqrt(jnp.mean(xf * xf, axis=-1, keepdims=True) + NORM_EPS) * g_ref[...]
        if rows is not None:
            mod_ref = refs[pos]
            pos += 1
            y = y * (1.0 + mod_ref[0, rows[1]:rows[1] + 1, :]) + mod_ref[0, rows[0]:rows[0] + 1, :]
        if router is not None:
            whi_ref, wlo_ref, rb_ref = refs[pos:pos + 3]
            pos += 3
        o_ref = refs[pos]
        o_ref[...] = y.astype(o_ref.dtype)
        if router is not None:
            e_ref, p_ref = refs[pos + 1], refs[pos + 2]
            y_hi, y_lo = _split_bf16(y, 2)
            logits = _dot(y_hi, whi_ref[...]) + _dot(y_lo, whi_ref[...]) + _dot(y_hi, wlo_ref[...]) + rb_ref[...]
            lane = lax.broadcasted_iota(jnp.int32, logits.shape, 1)
            logits = jnp.where(lane < n_e, logits, -jnp.inf)
            m1 = jnp.max(logits, axis=-1, keepdims=True)
            i1 = jnp.min(jnp.where(logits == m1, lane, LANES), axis=-1, keepdims=True)
            rest = jnp.where(lane == i1, -jnp.inf, logits)
            m2 = jnp.max(rest, axis=-1, keepdims=True)
            i2 = jnp.min(jnp.where(rest == m2, lane, LANES), axis=-1, keepdims=True)
            e2 = jnp.exp(m2 - m1)
            p1 = 1.0 / (1.0 + e2)
            e_ref[...] = jnp.where(lane == 0, i1, jnp.where(lane == 1, i2, 0))
            p_ref[...] = jnp.where(lane == 0, p1, jnp.where(lane == 1, e2 * p1, 0.0))

    in_specs = [pl.BlockSpec((tm, d), lambda i: (i, 0)), pl.BlockSpec((1, d), lambda i: (0, 0))]
    args = [x, gain.reshape(1, d)]
    if rows is not None:
        in_specs.append(pl.BlockSpec((1, N_MOD, d), lambda i: (lay.group(i * tm), 0, 0)))
        args.append(mod3)
    out_shape = [jax.ShapeDtypeStruct((m, d), out_dtype)]
    out_specs = [pl.BlockSpec((tm, d), lambda i: (i, 0))]
    if router is not None:
        in_specs += [pl.BlockSpec((d, LANES), lambda i: (0, 0)), pl.BlockSpec((d, LANES), lambda i: (0, 0)),
                     pl.BlockSpec((1, LANES), lambda i: (0, 0))]
        args += list(router[:3])
        out_shape += [jax.ShapeDtypeStruct((m, LANES), jnp.int32), jax.ShapeDtypeStruct((m, LANES), F32)]
        out_specs += [pl.BlockSpec((tm, LANES), lambda i: (i, 0))] * 2
    res = pl.pallas_call(body, grid=(m // tm,), in_specs=in_specs, out_specs=out_specs, out_shape=out_shape,
                         compiler_params=_cp(1), name=name)(*args)
    return res if router is not None else res[0]


def _dense_ffn(x, h, w_gate, w_up, w_down, mod3, lay):
    d_ff = w_gate.shape[1]
    hid = _mm_dual(h, w_gate, w_up, None, None, lambda g, u: _silu(g) * u, BF16, n=d_ff, col2=0,
                   tm=lay.tile(1024), tn=512, name="ffn_up")
    return _mm_resid(hid, w_down, None, x, mod3, 5, lay, tm=lay.tile(512), tn=512, n_outer=True,
                     name="ffn_down")


def _moe_up(xs, w_gate, w_up, block_e, block_src, n_used, *, tm):
    n_slots, d = xs.shape
    n_e, _, d_ff = w_gate.shape
    tn = _tile(1024, d_ff)
    n_blocks = n_slots // tm

    def body(be_ref, bs_ref, nu_ref, x_ref, wg_ref, wu_ref, o_ref):
        @pl.when(pl.program_id(1) < nu_ref[0])
        def _():
            xv = x_ref[...]
            g = _dot(xv, wg_ref[...].astype(BF16))
            u = _dot(xv, wu_ref[...].astype(BF16))
            o_ref[...] = (_silu(g) * u).astype(o_ref.dtype)

    grid_spec = pltpu.PrefetchScalarGridSpec(
        num_scalar_prefetch=3, grid=(d_ff // tn, n_blocks),
        in_specs=[pl.BlockSpec((tm, d), lambda j, i, be, bs, nu: (bs[i], 0)),
                  pl.BlockSpec((None, d, tn), lambda j, i, be, bs, nu: (be[i], 0, j)),
                  pl.BlockSpec((None, d, tn), lambda j, i, be, bs, nu: (be[i], 0, j))],
        out_specs=pl.BlockSpec((tm, tn), lambda j, i, be, bs, nu: (bs[i], j)))
    return pl.pallas_call(body, grid_spec=grid_spec, out_shape=jax.ShapeDtypeStruct((n_slots, d_ff), BF16),
                          compiler_params=_cp(2, arbitrary_last=True), name="moe_up"
                          )(block_e, block_src, n_used, xs, w_gate, w_up)


def _moe_down(hid, w_down, slot_w, block_e, block_src, n_used, *, tm):
    n_slots, d_ff = hid.shape
    d = w_down.shape[2]
    tn = _tile(512, d)
    n_blocks = n_slots // tm

    def body(be_ref, bs_ref, nu_ref, h_ref, w_ref, sw_ref, o_ref):
        @pl.when(pl.program_id(1) < nu_ref[0])
        def _():
            o_ref[...] = _dot(h_ref[...], w_ref[...].astype(BF16)) * sw_ref[...]

    grid_spec = pltpu.PrefetchScalarGridSpec(
        num_scalar_prefetch=3, grid=(d // tn, n_blocks),
        in_specs=[pl.BlockSpec((tm, d_ff), lambda j, i, be, bs, nu: (bs[i], 0)),
                  pl.BlockSpec((None, d_ff, tn), lambda j, i, be, bs, nu: (be[i], 0, j)),
                  pl.BlockSpec((tm, 1), lambda j, i, be, bs, nu: (bs[i], 0))],
        out_specs=pl.BlockSpec((tm, tn), lambda j, i, be, bs, nu: (bs[i], j)))
    return pl.pallas_call(body, grid_spec=grid_spec, out_shape=jax.ShapeDtypeStruct((n_slots, d), F32),
                          compiler_params=_cp(2, arbitrary_last=True), name="moe_down"
                          )(block_e, block_src, n_used, hid, w_down, slot_w.reshape(n_slots, 1))


def _moe_combine(x, y0, y1, mod3, lay):
    m, d = x.shape
    tm = lay.tile(256)

    def body(x_ref, a_ref, b_ref, mod_ref, o_ref):
        o_ref[...] = x_ref[...] + mod_ref[0, 5:6, :] * (a_ref[...] + b_ref[...])

    row = pl.BlockSpec((tm, d), lambda i: (i, 0))
    return pl.pallas_call(
        body, grid=(m // tm,),
        in_specs=[row, row, row, pl.BlockSpec((1, N_MOD, d), lambda i: (lay.group(i * tm), 0, 0))],
        out_specs=row, out_shape=jax.ShapeDtypeStruct((m, d), F32), compiler_params=_cp(1), name="moe_combine",
    )(x, y0, y1, mod3)


def _moe_ffn(x, norm_g, mod3, lay, w_router, b_router, w_gate, w_up, w_down):
    n_tok, d = x.shape
    n_e = w_router.shape[1]
    wr = jnp.zeros((d, LANES), F32).at[:, :n_e].set(w_router)
    wr_hi = wr.astype(BF16)
    wr_lo = (wr - wr_hi.astype(F32)).astype(BF16)
    rb = jnp.zeros((1, LANES), F32).at[0, :n_e].set(b_router.astype(F32))
    h, top_e, top_p = _norm_mod(x, norm_g, mod3, (3, 4), lay, BF16, router=(wr_hi, wr_lo, rb, n_e),
                                name="norm_router")
    tm = MOE_ROWS
    nk = n_tok * TOP_K
    flat_e = top_e[:, :TOP_K].reshape(nk)
    flat_p = top_p[:, :TOP_K].reshape(nk)
    onehot = (flat_e[:, None] == jnp.arange(n_e, dtype=jnp.int32)[None, :]).astype(jnp.int32)
    rank = jnp.take_along_axis(jnp.cumsum(onehot, axis=0) - onehot, flat_e[:, None], axis=1)[:, 0]
    counts = jnp.sum(onehot, axis=0)
    padded = (counts + tm - 1) // tm * tm
    pad_end = jnp.cumsum(padded)
    dest = (pad_end - padded)[flat_e] + rank
    n_blocks = -(-nk // tm) + n_e
    n_slots = n_blocks * tm
    slot_tok = jnp.zeros((n_slots,), jnp.int32).at[dest].set(jnp.arange(nk, dtype=jnp.int32) // TOP_K)
    slot_w = jnp.zeros((n_slots,), F32).at[dest].set(flat_p)
    n_used = (pad_end[-1] // tm).astype(jnp.int32)
    blk = jnp.arange(n_blocks, dtype=jnp.int32)
    block_src = jnp.minimum(blk, n_used - 1)
    block_e = jnp.minimum(jnp.searchsorted(pad_end, block_src * tm, side="right"), n_e - 1).astype(jnp.int32)
    xs = jnp.take(h, slot_tok, axis=0)
    hid = _moe_up(xs, w_gate, w_up, block_e, block_src, n_used.reshape(1), tm=tm)
    ys = _moe_down(hid, w_down, slot_w, block_e, block_src, n_used.reshape(1), tm=tm)
    pos = dest.reshape(n_tok, TOP_K)
    return _moe_combine(x, jnp.take(ys, pos[:, 0], axis=0), jnp.take(ys, pos[:, 1], axis=0), mod3, lay)


def _mla_mid(dq, q_norm, kv_norm, cos64, sin64, q_lora, kv_lora, lay):
    m, n = dq.shape
    tm = lay.tile(512)
    o_kd, o_ko, o_ks = q_lora + kv_lora, q_lora + kv_lora + QK_ROPE, q_lora + kv_lora + 2 * QK_ROPE

    def body(x_ref, qg_ref, kg_ref, cos_ref, sin_ref, cq_ref, ckv_ref, kro_ref, krr_ref):
        cq = x_ref[:, 0:q_lora]
        cq_ref[...] = (cq * lax.rsqrt(jnp.mean(cq * cq, axis=-1, keepdims=True) + NORM_EPS)
                       * qg_ref[...]).astype(cq_ref.dtype)
        ckv = x_ref[:, q_lora:q_lora + kv_lora]
        ckv_ref[...] = ckv * lax.rsqrt(jnp.mean(ckv * ckv, axis=-1, keepdims=True) + NORM_EPS) * kg_ref[...]
        kro_ref[...] = x_ref[:, o_ko:o_ko + QK_ROPE]
        krr_ref[...] = (x_ref[:, o_kd:o_kd + QK_ROPE] * cos_ref[...]
                        + x_ref[:, o_ks:o_ks + QK_ROPE] * sin_ref[...]).astype(krr_ref.dtype)

    rope = pl.BlockSpec((tm, QK_ROPE), lambda i: (i, 0))
    return pl.pallas_call(
        body, grid=(m // tm,),
        in_specs=[pl.BlockSpec((tm, n), lambda i: (i, 0)), pl.BlockSpec((1, q_lora), lambda i: (0, 0)),
                  pl.BlockSpec((1, kv_lora), lambda i: (0, 0)), rope, rope],
        out_specs=[pl.BlockSpec((tm, q_lora), lambda i: (i, 0)), pl.BlockSpec((tm, kv_lora), lambda i: (i, 0)),
                   rope, rope],
        out_shape=[jax.ShapeDtypeStruct((m, q_lora), BF16), jax.ShapeDtypeStruct((m, kv_lora), F32),
                   jax.ShapeDtypeStruct((m, QK_ROPE), F32), jax.ShapeDtypeStruct((m, QK_ROPE), BF16)],
        compiler_params=_cp(1), name="mla_mid",
    )(dq, q_norm.reshape(1, -1), kv_norm.reshape(1, -1), cos64, sin64)


def _mla_q(cq, w_uq_ext, cos64, sin64, n_heads, tm):
    m = cq.shape[0]
    hw = QK_NOPE + 2 * QK_ROPE
    dqk = QK_NOPE + QK_ROPE
    scale = dqk ** -0.5

    def epi(accs, v):
        x = accs[0]
        rot = x[:, QK_NOPE:dqk] * v[0][...] + x[:, dqk:hw] * v[1][...]
        return [jnp.concatenate([x[:, :QK_NOPE], rot], axis=1) * scale]

    rope = lambda arr: (arr, (tm, QK_ROPE), lambda i, j: (i, 0))
    return _mm(cq, [w_uq_ext], tm=tm, tn=hw, gn=n_heads, epi=epi, vecs=[rope(cos64), rope(sin64)], name="mla_q",
               outs=[((n_heads, m, dqk), BF16, (1, tm, dqk), lambda i, j: (j, i, 0))])[0]


def _mla_kv(ckv, w_ukv, krope, n_heads, tm):
    m = ckv.shape[0]
    dqk = QK_NOPE + QK_ROPE

    def epi(accs, v):
        x = accs[0]
        return [jnp.concatenate([x[:, :QK_NOPE], v[0][...].astype(F32)], axis=1), x[:, QK_NOPE:]]

    return _mm(ckv, [w_ukv], tm=tm, tn=QK_NOPE + V_DIM, gn=n_heads, epi=epi, name="mla_kv",
               vecs=[(krope, (tm, QK_ROPE), lambda i, j: (i, 0))],
               outs=[((n_heads, m, dqk), BF16, (1, tm, dqk), lambda i, j: (j, i, 0)),
                     ((n_heads, m, V_DIM), BF16, (1, tm, V_DIM), lambda i, j: (j, i, 0))])


def _attn_context(q, k, v, lay):
    n_heads = q.shape[0]
    t = lay.ctx_len
    dqk = q.shape[2]

    def body(q_ref, k_ref, v_ref, o_ref):
        for h in range(n_heads):
            s = _dot_nt(q_ref[h], k_ref[h])
            p = jnp.exp(s - jnp.max(s, axis=-1, keepdims=True))
            l = jnp.sum(p, axis=-1, keepdims=True)
            o = _dot(p.astype(BF16), v_ref[h]) / l
            o_ref[:, h * V_DIM:(h + 1) * V_DIM] = o.astype(o_ref.dtype)

    return pl.pallas_call(
        body, grid=(lay.n_ctx_seq,),
        in_specs=[pl.BlockSpec((n_heads, t, dqk), lambda b: (0, b, 0)),
                  pl.BlockSpec((n_heads, t, dqk), lambda b: (0, b, 0)),
                  pl.BlockSpec((n_heads, t, V_DIM), lambda b: (0, b, 0))],
        out_specs=pl.BlockSpec((t, n_heads * V_DIM), lambda b: (b, 0)),
        out_shape=jax.ShapeDtypeStruct((lay.n_ctx, n_heads * V_DIM), BF16),
        compiler_params=_cp(1), name="attn_context",
    )(q, k, v)


def _attn_latent(q, k, v, kc, vc, lay, past_len):
    n_heads = q.shape[0]
    t = lay.lat_len
    dqk = q.shape[2]
    tq = _tile(512, t)
    lat0 = lay.n_ctx // t
    q0 = lay.n_ctx // tq

    def body(q_ref, k_ref, v_ref, kc_ref, vc_ref, o_ref):
        qv = q_ref[0]
        s1 = _dot_nt(qv, k_ref[0])
        s2 = _dot_nt(qv, kc_ref[0])
        mx = jnp.maximum(jnp.max(s1, axis=-1, keepdims=True), jnp.max(s2, axis=-1, keepdims=True))
        p1 = jnp.exp(s1 - mx)
        p2 = jnp.exp(s2 - mx)
        l = jnp.sum(p1, axis=-1, keepdims=True) + jnp.sum(p2, axis=-1, keepdims=True)
        o = (_dot(p1.astype(BF16), v_ref[0]) + _dot(p2.astype(BF16), vc_ref[0])) / l
        o_ref[...] = o.astype(o_ref.dtype)

    return pl.pallas_call(
        body, grid=(lay.n_lat_seq, n_heads, t // tq),
        in_specs=[pl.BlockSpec((1, tq, dqk), lambda b, h, i: (h, q0 + b * (t // tq) + i, 0)),
                  pl.BlockSpec((1, t, dqk), lambda b, h, i: (h, lat0 + b, 0)),
                  pl.BlockSpec((1, t, V_DIM), lambda b, h, i: (h, lat0 + b, 0)),
                  pl.BlockSpec((1, past_len, dqk), lambda b, h, i: (h, b, 0)),
                  pl.BlockSpec((1, past_len, V_DIM), lambda b, h, i: (h, b, 0))],
        out_specs=pl.BlockSpec((tq, V_DIM), lambda b, h, i: (b * (t // tq) + i, h)),
        out_shape=jax.ShapeDtypeStruct((lay.n_lat_seq * t, n_heads * V_DIM), BF16),
        compiler_params=_cp(3), name="attn_latent",
    )(q, k, v, kc, vc)


def _rope_tables(lay):
    n = lay.lat_len
    pairs = QK_ROPE // 4
    row_pos = (jnp.arange(n) // GRID_W).astype(F32)
    col_pos = (jnp.arange(n) % GRID_W).astype(F32)
    inv_freq = ROPE_THETA ** (-jnp.arange(pairs, dtype=F32) / pairs)
    ang = jnp.concatenate([row_pos[:, None] * inv_freq, col_pos[:, None] * inv_freq], axis=-1)
    cos, sin = jnp.cos(ang), jnp.sin(ang)
    cos64 = jnp.concatenate([cos, cos], axis=-1)
    sin64 = jnp.concatenate([-sin, sin], axis=-1)
    one = jnp.ones((lay.n_ctx, QK_ROPE), F32)
    cos_all = jnp.concatenate([one] + [cos64] * lay.n_lat_seq, axis=0)
    sin_all = jnp.concatenate([0.0 * one] + [sin64] * lay.n_lat_seq, axis=0)
    return cos_all, sin_all


def _mla_layer(x, h, mod3, lay, cache_ckv, cache_krope, w_down, q_norm, kv_norm, w_uq, w_ukv, w_o):
    d = x.shape[1]
    q_lora, kv_lora = q_norm.shape[0], kv_norm.shape[0]
    n_heads = w_o.shape[0] // V_DIM
    dqk = QK_NOPE + QK_ROPE
    perm_d = np.concatenate([np.arange(0, QK_ROPE, 2), np.arange(1, QK_ROPE, 2)])
    perm_s = np.concatenate([np.arange(1, QK_ROPE, 2), np.arange(0, QK_ROPE, 2)])
    kr0 = q_lora + kv_lora
    w_down_ext = jnp.concatenate(
        [w_down[:, :kr0], w_down[:, kr0 + perm_d], w_down[:, kr0:], w_down[:, kr0 + perm_s],
         jnp.zeros((d, QK_ROPE), F32)], axis=1)
    w_uq_h = w_uq.reshape(q_lora, n_heads, dqk)
    w_uq_ext = jnp.concatenate([w_uq_h[:, :, :QK_NOPE], w_uq_h[:, :, QK_NOPE + perm_d],
                                w_uq_h[:, :, QK_NOPE + perm_s]], axis=2).reshape(q_lora, -1)
    cos64, sin64 = _rope_tables(lay)
    tm = lay.tile(1024)
    dq = _mm_act(h, w_down_ext, None, None, F32, tm=tm, tn=512, name="mla_down")
    cq, ckv, krope_raw, krope_rot = _mla_mid(dq, q_norm, kv_norm, cos64, sin64, q_lora, kv_lora, lay)
    q = _mla_q(cq, w_uq_ext, cos64, sin64, n_heads, tm)
    k, v = _mla_kv(ckv, w_ukv, krope_rot, n_heads, tm)
    n_past = cache_ckv.shape[0] * cache_ckv.shape[1]
    kc, vc = _mla_kv(cache_ckv.reshape(n_past, kv_lora), w_ukv,
                     cache_krope.reshape(n_past, QK_ROPE)[:, perm_d].astype(BF16), n_heads,
                     _tile(512, cache_ckv.shape[1]))
    o_ctx = _attn_context(q, k, v, lay)
    o_lat = _attn_latent(q, k, v, kc, vc, lay, cache_ckv.shape[1])
    attn = jnp.concatenate([o_ctx, o_lat], axis=0)
    x = _mm_resid(attn, w_o, None, x, mod3, 2, lay, tm=tm, tn=512, name="mla_out")
    ckv_ctx = ckv[:lay.n_ctx].reshape(lay.n_ctx_seq, lay.ctx_len, kv_lora)
    krope_ctx = krope_raw[:lay.n_ctx].reshape(lay.n_ctx_seq, lay.ctx_len, QK_ROPE)
    return x, ckv_ctx, krope_ctx


def _conv_dw(u, w_dw, b_dw, ln_g, ln_b, lay):
    m, d = u.shape
    width = w_dw.shape[0]
    pad = width // 2
    t = lay.tile(256)
    halo = 2 * SUBLANES
    assert pad <= halo
    rb = _tile(64, t)

    def body(prev_ref, cur_ref, nxt_ref, w_ref, bdw_ref, g_ref, b_ref, o_ref, ext_ref, dw_ref):
        pos, ln = lay.seq_pos(pl.program_id(0) * t)
        ext_ref[0:halo, :] = jnp.where(pos > 0, prev_ref[...], 0.0)
        ext_ref[halo:halo + t, :] = cur_ref[...]
        ext_ref[halo + t:halo + t + halo, :] = jnp.where(pos + t < ln, nxt_ref[...], 0.0)

        def chunk(c, carry):
            ls = pl.ds(pl.multiple_of(c * LANES, LANES), LANES)
            for r in range(t // rb):
                acc = jnp.zeros((rb, LANES), F32)
                for j in range(width):
                    acc = acc + ext_ref[pl.ds(halo - pad + j + r * rb, rb), ls] * w_ref[pl.ds(j, 1), ls]
                dw_ref[pl.ds(r * rb, rb), ls] = acc + bdw_ref[:, ls]
            return carry

        lax.fori_loop(0, d // LANES, chunk, 0)
        xv = dw_ref[...]
        mu = jnp.mean(xv, axis=-1, keepdims=True)
        xc = xv - mu
        var = jnp.mean(xc * xc, axis=-1, keepdims=True)
        o_ref[...] = _silu(xc * lax.rsqrt(var + LN_EPS) * g_ref[...] + b_ref[...]).astype(o_ref.dtype)

    hb = t // halo
    vec = pl.BlockSpec((1, d), lambda i: (0, 0))
    return pl.pallas_call(
        body, grid=(m // t,),
        in_specs=[pl.BlockSpec((halo, d), lambda i: (jnp.maximum(i * hb - 1, 0), 0)),
                  pl.BlockSpec((t, d), lambda i: (i, 0)),
                  pl.BlockSpec((halo, d), lambda i: (jnp.minimum((i + 1) * hb, m // halo - 1), 0)),
                  pl.BlockSpec((width, d), lambda i: (0, 0)), vec, vec, vec],
        out_specs=pl.BlockSpec((t, d), lambda i: (i, 0)),
        out_shape=jax.ShapeDtypeStruct((m, d), BF16),
        scratch_shapes=[pltpu.VMEM((t + 2 * halo, d), F32), pltpu.VMEM((t, d), F32)],
        compiler_params=_cp(1), name="conv_dw",
    )(u, u, u, w_dw, b_dw.reshape(1, d), ln_g.reshape(1, d), ln_b.reshape(1, d))


def _conformer_layer(x, h, mod3, lay, w_in, b_in, w_dw, b_dw, ln_g, ln_b, w_out, b_out):
    d = x.shape[1]
    tm = lay.tile(1024)
    u = _mm_dual(h, w_in, w_in, b_in, b_in, lambda a, b: a * jax.nn.sigmoid(b), F32, n=d, col2=d // 512,
                 tm=tm, tn=512, name="conv_in")
    cv = _conv_dw(u, w_dw, b_dw, ln_g, ln_b, lay)
    return _mm_resid(cv, w_out, b_out, x, mod3, 2, lay, tm=tm, tn=512, name="conv_out")


def _sgu_mix(z, ln_g, ln_b, w_s, b_s, lay):
    m, d2 = z.shape
    d = d2 // 2
    groups, chunk, _ = w_s.shape
    gd = d // groups
    tm = lay.tile(2 * chunk)
    bs_t = jnp.transpose(b_s)

    def body(u_ref, v_ref, g_ref, b_ref, ws_ref, bs_ref, o_ref):
        xv = v_ref[...]
        mu = jnp.mean(xv, axis=-1, keepdims=True)
        xc = xv - mu
        var = jnp.mean(xc * xc, axis=-1, keepdims=True)
        vn = (xc * lax.rsqrt(var + LN_EPS) * g_ref[...] + b_ref[...]).astype(BF16)
        for c in range(tm // chunk):
            rs = slice(c * chunk, (c + 1) * chunk)
            for g in range(groups):
                ls = slice(g * gd, (g + 1) * gd)
                mixed = _dot(ws_ref[g].astype(BF16), vn[rs, ls]) + bs_ref[:, g:g + 1]
                o_ref[rs, ls] = (u_ref[rs, ls] * mixed).astype(o_ref.dtype)

    vec = pl.BlockSpec((1, d), lambda i: (0, 0))
    return pl.pallas_call(
        body, grid=(m // tm,),
        in_specs=[pl.BlockSpec((tm, d), lambda i: (i, 0)), pl.BlockSpec((tm, d), lambda i: (i, 1)), vec, vec,
                  pl.BlockSpec((groups, chunk, chunk), lambda i: (0, 0, 0)),
                  pl.BlockSpec((chunk, groups), lambda i: (0, 0))],
        out_specs=pl.BlockSpec((tm, d), lambda i: (i, 0)),
        out_shape=jax.ShapeDtypeStruct((m, d), BF16), compiler_params=_cp(1), name="sgu_mix",
    )(z, z, ln_g.reshape(1, d), ln_b.reshape(1, d), w_s, bs_t)


def _sgu_layer(x, h, mod3, lay, w_in, b_in, ln_g, ln_b, w_s, b_s, w_out, b_out):
    tm = lay.tile(1024)
    z = _mm_act(h, w_in, b_in, _gelu_tanh, F32, tm=tm, tn=512, name="sgu_in", n_outer=True)
    sm = _sgu_mix(z, ln_g, ln_b, w_s, b_s, lay)
    return _mm_resid(sm, w_out, b_out, x, mod3, 2, lay, tm=tm, tn=512, name="sgu_out")


def _shift_mix(h, mu, lay):
    m, d = h.shape
    t = lay.tile(256)
    halo = SUBLANES

    def body(prev_ref, cur_ref, nxt_ref, mu_ref, *rest):
        o_refs, ext_ref = rest[:6], rest[6]
        pos, ln = lay.seq_pos(pl.program_id(0) * t)
        ext_ref[0:halo, :] = jnp.where(pos > 0, prev_ref[...], 0.0)
        ext_ref[halo:halo + t, :] = cur_ref[...]
        ext_ref[halo + t:halo + t + halo, :] = jnp.where(pos + t < ln, nxt_ref[...], 0.0)
        xv = cur_ref[...]
        xx = 0.5 * (ext_ref[pl.ds(halo - 1, t), :] + ext_ref[pl.ds(halo + 1, t), :]) - xv
        for j in range(6):
            o_refs[j][...] = (xv + xx * mu_ref[j:j + 1, :]).astype(BF16)

    hb = t // halo
    row = pl.BlockSpec((t, d), lambda i: (i, 0))
    return pl.pallas_call(
        body, grid=(m // t,),
        in_specs=[pl.BlockSpec((halo, d), lambda i: (jnp.maximum(i * hb - 1, 0), 0)), row,
                  pl.BlockSpec((halo, d), lambda i: (jnp.minimum((i + 1) * hb, m // halo - 1), 0)),
                  pl.BlockSpec((6, d), lambda i: (0, 0))],
        out_specs=[row] * 6, out_shape=[jax.ShapeDtypeStruct((m, d), BF16)] * 6,
        scratch_shapes=[pltpu.VMEM((t + 2 * halo, d), F32)],
        compiler_params=_cp(1), name="rwkv_shift_mix",
    )(h, h, h, mu)


def _wkv_prep(r, k, v, wl, al, k_k, k_a, r_k):
    m, d = r.shape
    tm = _tile(128, m)

    def body(r_ref, k_ref, v_ref, wl0_ref, wl1_ref, al0_ref, al1_ref, kk_ref, ka_ref, rk_ref,
             okk_ref, olw_ref, ob_ref, okt_ref, obon_ref):
        rv, kv, vv = r_ref[...], k_ref[...], v_ref[...]
        kk = kv * kk_ref[...]
        kk = kk / jnp.maximum(jnp.sqrt(_head_sum(kk * kk)), 1e-12)
        okk_ref[...] = kk
        bonus = jnp.zeros_like(rv)
        for dr, (wl_ref, al_ref) in enumerate(((wl0_ref, al0_ref), (wl1_ref, al1_ref))):
            w_log = -_softplus(-wl_ref[...]) - 0.5
            olw_ref[dr] = -jnp.exp(w_log)
            a = jax.nn.sigmoid(al_ref[...])
            kt = kv * (1.0 + (a - 1.0) * ka_ref[...])
            ob_ref[dr] = kk * a
            okt_ref[dr] = kt
            bonus = bonus + _head_sum(rv * kt * rk_ref[...]) * vv
        obon_ref[...] = bonus

    row = pl.BlockSpec((tm, d), lambda i: (i, 0))
    vec = pl.BlockSpec((1, d), lambda i: (0, 0))
    two = pl.BlockSpec((2, tm, d), lambda i: (0, i, 0))
    return pl.pallas_call(
        body, grid=(m // tm,),
        in_specs=[row] * 7 + [vec] * 3,
        out_specs=[row, two, two, two, row],
        out_shape=[jax.ShapeDtypeStruct((m, d), F32)] + [jax.ShapeDtypeStruct((2, m, d), F32)] * 3
        + [jax.ShapeDtypeStruct((m, d), F32)],
        compiler_params=_cp(1), name="wkv_prep",
    )(r, k, v, wl[0], wl[1], al[0], al[1], k_k.reshape(1, d), k_a.reshape(1, d), r_k.reshape(1, d))


def _wkv_scan(r, kk, v, lw, b, kt, t_init, lay):
    m, d = r.shape
    c = WKV_CHUNK
    n_pairs = d // LANES
    pps = math.gcd(WKV_PAIRS_PER_STEP, n_pairs)
    pw = pps * LANES
    n_chunks = m // c
    ctx_chunks = lay.n_ctx // c
    cps_ctx, cps_lat = lay.ctx_len // c, lay.lat_len // c
    n_seq = lay.n_ctx_seq + lay.n_lat_seq
    c2 = 2 * c

    def chunk_of(dr, s):
        return s + dr * (n_chunks - 1 - 2 * s)

    def seq_of(tc):
        is_ctx = tc < ctx_chunks
        seq = jnp.where(is_ctx, tc // cps_ctx, lay.n_ctx_seq + (tc - ctx_chunks) // cps_lat)
        pos = jnp.where(is_ctx, tc % cps_ctx, (tc - ctx_chunks) % cps_lat)
        return is_ctx, seq, pos, jnp.where(is_ctx, cps_ctx, cps_lat)

    def body(r_ref, kk_ref, v_ref, lw_ref, b_ref, kt_ref, ti_ref, y_ref, to_ref, t_scr):
        dr = pl.program_id(0)
        fwd = dr == 0
        tc = chunk_of(dr, pl.program_id(2))
        is_ctx, _, pos, cps = seq_of(tc)
        first = jnp.where(fwd, pos == 0, pos == cps - 1)
        last = jnp.where(fwd, pos == cps - 1, pos == 0)

        @pl.when(first)
        def _():
            t_scr[...] = jnp.where(is_ctx, 0.0, ti_ref[0, 0])

        ri = lax.broadcasted_iota(jnp.int32, (c2, c2), 0)
        ci = lax.broadcasted_iota(jnp.int32, (c2, c2), 1)
        same = (ri // c) == (ci // c)
        sgn = 1 - 2 * dr
        strict = same & ((ri - ci) * sgn > 0)
        incl = same & ((ri - ci) * sgn >= 0)
        eye = jnp.where(ri == ci, 1.0, 0.0)
        ti = lax.broadcasted_iota(jnp.int32, (c, c), 0)
        tj = lax.broadcasted_iota(jnp.int32, (c, c), 1)
        cum = jnp.where((ti - tj) * sgn >= 0, 1.0, 0.0).astype(BF16)
        ones_c = jnp.ones((c, LANES), BF16)
        head_of_row = lax.broadcasted_iota(jnp.int32, (c2, LANES), 0) // c
        head_of_lane = lax.broadcasted_iota(jnp.int32, (c2, LANES), 1) // RWKV_N
        own = head_of_row == head_of_lane

        def stack(xv):
            return jnp.where(own, jnp.concatenate([xv, xv], axis=0), 0.0).astype(BF16)

        for p in range(pps):
            ls = slice(p * LANES, (p + 1) * LANES)
            lwv = lw_ref[0, :, ls]
            lw_parts = _split_bf16(lwv, 3)
            cs = sum(_dot(cum, part) for part in lw_parts)
            tot = sum(_dot_tn(part, ones_c) for part in lw_parts)
            g, ginv, gprev = jnp.exp(cs), jnp.exp(-cs), jnp.exp(cs - lwv)
            xa = stack(-kk_ref[:, ls] * gprev)
            xr = stack(r_ref[:, ls] * g)
            xb = stack(b_ref[0, :, ls] * ginv)
            xk = stack(kt_ref[0, :, ls] * ginv)
            xv = stack(v_ref[:, ls])
            m_ab = jnp.where(strict, _dot_nt(xa, xb), 0.0)
            m_ak = jnp.where(strict, _dot_nt(xa, xk), 0.0).astype(BF16)
            m_rb = jnp.where(incl, _dot_nt(xr, xb), 0.0).astype(BF16)
            m_rk = jnp.where(incl, _dot_nt(xr, xk), 0.0).astype(BF16)
            tinv = eye + m_ab
            pw2 = m_ab
            for _ in range(int(math.log2(c)) - 1):
                pb = pw2.astype(BF16)
                pw2 = _dot(pb, pb)
                tinv = tinv + _dot(tinv.astype(BF16), pw2.astype(BF16))
            tinv = tinv.astype(BF16)
            pm = _dot(tinv, xa).astype(BF16)
            q = _dot(tinv, _dot(m_ak, xv).astype(BF16)).astype(BF16)
            gc = jnp.exp(tot)
            a_c = gc * (eye + _dot_tn(xb, pm))
            b_c = gc * (_dot_tn(xb, q) + _dot_tn(xk, xv))
            c_c = xr.astype(F32) + _dot(m_rb, pm)
            d_c = _dot(m_rb, q) + _dot(m_rk, xv)
            t_old = t_scr[p].astype(BF16)
            yy = _dot(c_c.astype(BF16), t_old) + d_c
            t_scr[p] = _dot(a_c.astype(BF16), t_old) + b_c
            y_ref[0, :, ls] = yy[:c] + yy[c:]

        @pl.when(last)
        def _():
            to_ref[0, 0] = t_scr[...]

    tok = lambda dr, lb, s: (chunk_of(dr, s), lb)
    tok3 = lambda dr, lb, s: (dr, chunk_of(dr, s), lb)
    shared = pl.BlockSpec((c, pw), tok)
    per_dir = pl.BlockSpec((1, c, pw), tok3)
    st = (1, 1, pps, LANES, LANES)
    y, t_fin = pl.pallas_call(
        body, grid=(2, n_pairs // pps, n_chunks),
        in_specs=[shared, shared, shared, per_dir, per_dir, per_dir,
                  pl.BlockSpec(st, lambda dr, lb, s: (
                      dr, jnp.clip(seq_of(chunk_of(dr, s))[1] - lay.n_ctx_seq, 0, lay.n_lat_seq - 1), lb, 0, 0))],
        out_specs=[per_dir, pl.BlockSpec(st, lambda dr, lb, s: (dr, seq_of(chunk_of(dr, s))[1], lb, 0, 0))],
        out_shape=[jax.ShapeDtypeStruct((2, m, d), F32),
                   jax.ShapeDtypeStruct((2, n_seq, n_pairs, LANES, LANES), F32)],
        scratch_shapes=[pltpu.VMEM((pps, LANES, LANES), F32)],
        compiler_params=_cp(3, arbitrary_last=True), name="wkv_scan",
    )(r, kk, v, lw, b, kt, t_init)
    return y, t_fin


def _wkv_post(y, bonus, g, ln_g, ln_b):
    m, d = bonus.shape
    tm = _tile(128, m)

    def body(y_ref, bon_ref, g_ref, lg_ref, lb_ref, o_ref):
        yv = y_ref[0] + y_ref[1]
        mu = _head_sum(yv) * (1.0 / RWKV_N)
        yc = yv - mu
        var = _head_sum(yc * yc) * (1.0 / RWKV_N)
        o = yc * lax.rsqrt(var + GN_EPS) * lg_ref[...] + lb_ref[...] + bon_ref[...]
        o_ref[...] = (o * g_ref[...]).astype(o_ref.dtype)

    row = pl.BlockSpec((tm, d), lambda i: (i, 0))
    vec = pl.BlockSpec((1, d), lambda i: (0, 0))
    return pl.pallas_call(
        body, grid=(m // tm,),
        in_specs=[pl.BlockSpec((2, tm, d), lambda i: (0, i, 0)), row, row, vec, vec],
        out_specs=row, out_shape=jax.ShapeDtypeStruct((m, d), BF16), compiler_params=_cp(1), name="wkv_post",
    )(y, bonus, g, ln_g.reshape(1, d), ln_b.reshape(1, d))


def _pad_cols(w, n):
    return jnp.pad(w, ((0, 0), (0, n - w.shape[1])))


def _pad_rows(w, n):
    return jnp.pad(w, ((0, n - w.shape[0]), (0, 0)))


def _rwkv_layer(x, hf, mod3, lay, s_fwd, s_bwd, mu, w_r, w_k, w_v, w_o, w0, w1, w2, a0, a1, a2, g1, g2,
                k_k, k_a, r_k, ln_g, ln_b):
    m, d = x.shape
    tm = lay.tile(1024)
    xr, xw, xk, xv, xa, xg = _shift_mix(hf, mu, lay)
    r = _mm_act(xr, w_r, None, None, F32, tm=tm, tn=512, name="rwkv_r")
    k = _mm_act(xk, w_k, None, None, F32, tm=tm, tn=512, name="rwkv_k")
    v = _mm_act(xv, w_v, None, None, F32, tm=tm, tn=512, name="rwkv_v")
    gh = _mm_act(xg, g1, None, jax.nn.sigmoid, BF16, tm=tm, tn=g1.shape[1], name="rwkv_g1")
    g = _mm_act(gh, g2, None, None, F32, tm=tm, tn=512, name="rwkv_g2")
    wl, al = [], []
    for dr in range(2):
        th = _mm_act(xw, _pad_cols(w1[dr], LANES), None, jnp.tanh, BF16, tm=tm, tn=LANES, name="rwkv_w1")
        wl.append(_mm_act(th, _pad_rows(w2[dr], LANES), w0[dr], None, F32, tm=tm, tn=512, name="rwkv_w2"))
        ah = _mm_act(xa, _pad_cols(a1[dr], LANES), None, None, BF16, tm=tm, tn=LANES, name="rwkv_a1")
        al.append(_mm_act(ah, _pad_rows(a2[dr], LANES), a0[dr], None, F32, tm=tm, tn=512, name="rwkv_a2"))
    kk, lw, b, kt, bonus = _wkv_prep(r, k, v, wl, al, k_k, k_a, r_k.reshape(-1))
    n_pairs = d // LANES
    st = jnp.stack([s_fwd, s_bwd]).astype(F32)
    st = jnp.swapaxes(st, -1, -2).reshape(2, -1, n_pairs, 2, RWKV_N, RWKV_N)
    t_init = jnp.zeros((2, st.shape[1], n_pairs, LANES, LANES), F32)
    t_init = t_init.at[..., :RWKV_N, :RWKV_N].set(st[:, :, :, 0]).at[..., RWKV_N:, RWKV_N:].set(st[:, :, :, 1])
    y, t_fin = _wkv_scan(r, kk, v, lw, b, kt, t_init, lay)
    og = _wkv_post(y, bonus, g, ln_g, ln_b)
    x = _mm_resid(og, w_o, None, x, mod3, 2, lay, tm=tm, tn=512, name="rwkv_out")
    tf = t_fin[:, :lay.n_ctx_seq]
    s_pair = jnp.stack([tf[..., :RWKV_N, :RWKV_N], tf[..., RWKV_N:, RWKV_N:]], axis=3)
    s_out = jnp.swapaxes(s_pair, -1, -2).reshape(2, lay.n_ctx_seq, 2 * n_pairs, RWKV_N, RWKV_N)
    return x, s_out[0], s_out[1]


def kernel(x_prompt, x_sample, cache_ckv_l0, cache_krope_l0, state_wkv_fwd_l3, state_wkv_bwd_l3, c, c_ctx, l0_w_mod, l0_b_mod, l0_norm_mix, l0_norm_ffn, l0_mla_w_down, l0_mla_q_norm, l0_mla_kv_norm, l0_mla_w_uq, l0_mla_w_ukv, l0_mla_w_o, l0_ffn_w_gate, l0_ffn_w_up, l0_ffn_w_down, l1_w_mod, l1_b_mod, l1_norm_mix, l1_norm_ffn, l1_conv_w_in, l1_conv_b_in, l1_conv_w_dw, l1_conv_b_dw, l1_conv_ln_g, l1_conv_ln_b, l1_conv_w_out, l1_conv_b_out, l1_moe_w_router, l1_moe_b_router, l1_moe_w_gate, l1_moe_w_up, l1_moe_w_down, l2_w_mod, l2_b_mod, l2_norm_mix, l2_norm_ffn, l2_sgu_w_in, l2_sgu_b_in, l2_sgu_ln_g, l2_sgu_ln_b, l2_sgu_w_s, l2_sgu_b_s, l2_sgu_w_out, l2_sgu_b_out, l2_ffn_w_gate, l2_ffn_w_up, l2_ffn_w_down, l3_w_mod, l3_b_mod, l3_norm_mix, l3_norm_ffn, l3_rwkv_mu, l3_rwkv_w_r, l3_rwkv_w_k, l3_rwkv_w_v, l3_rwkv_w_o, l3_rwkv_w0, l3_rwkv_w1, l3_rwkv_w2, l3_rwkv_a0, l3_rwkv_a1, l3_rwkv_a2, l3_rwkv_g1, l3_rwkv_g2, l3_rwkv_k_k, l3_rwkv_k_a, l3_rwkv_r_k, l3_rwkv_ln_g, l3_rwkv_ln_b, l3_moe_w_router, l3_moe_b_router, l3_moe_w_gate, l3_moe_w_up, l3_moe_w_down, norm_out):
    n_ctx_seq, ctx_len, d = x_prompt.shape
    n_lat_seq, lat_len, _ = x_sample.shape
    lay = _Lay(n_ctx_seq, ctx_len, n_lat_seq, lat_len)
    assert n_lat_seq + 1 <= SUBLANES
    x = jnp.concatenate([x_prompt.reshape(-1, d), x_sample.reshape(-1, d)], axis=0)
    cond8 = jnp.zeros((SUBLANES, d), F32).at[0].set(c_ctx).at[1:1 + n_lat_seq].set(c)

    mod3 = _ada_mod(cond8, l0_w_mod, l0_b_mod)
    h = _norm_mod(x, l0_norm_mix, mod3, (0, 1), lay, BF16)
    x, ckv_l0, krope_l0 = _mla_layer(x, h, mod3, lay, cache_ckv_l0, cache_krope_l0, l0_mla_w_down, l0_mla_q_norm,
                                     l0_mla_kv_norm, l0_mla_w_uq, l0_mla_w_ukv, l0_mla_w_o)
    h = _norm_mod(x, l0_norm_ffn, mod3, (3, 4), lay, BF16)
    x = _dense_ffn(x, h, l0_ffn_w_gate, l0_ffn_w_up, l0_ffn_w_down, mod3, lay)

    mod3 = _ada_mod(cond8, l1_w_mod, l1_b_mod)
    h = _norm_mod(x, l1_norm_mix, mod3, (0, 1), lay, BF16)
    x = _conformer_layer(x, h, mod3, lay, l1_conv_w_in, l1_conv_b_in, l1_conv_w_dw, l1_conv_b_dw, l1_conv_ln_g,
                         l1_conv_ln_b, l1_conv_w_out, l1_conv_b_out)
    x = _moe_ffn(x, l1_norm_ffn, mod3, lay, l1_moe_w_router, l1_moe_b_router, l1_moe_w_gate, l1_moe_w_up,
                 l1_moe_w_down)

    mod3 = _ada_mod(cond8, l2_w_mod, l2_b_mod)
    h = _norm_mod(x, l2_norm_mix, mod3, (0, 1), lay, BF16)
    x = _sgu_layer(x, h, mod3, lay, l2_sgu_w_in, l2_sgu_b_in, l2_sgu_ln_g, l2_sgu_ln_b, l2_sgu_w_s, l2_sgu_b_s,
                   l2_sgu_w_out, l2_sgu_b_out)
    h = _norm_mod(x, l2_norm_ffn, mod3, (3, 4), lay, BF16)
    x = _dense_ffn(x, h, l2_ffn_w_gate, l2_ffn_w_up, l2_ffn_w_down, mod3, lay)

    mod3 = _ada_mod(cond8, l3_w_mod, l3_b_mod)
    hf = _norm_mod(x, l3_norm_mix, mod3, (0, 1), lay, F32)
    x, wkv_fwd, wkv_bwd = _rwkv_layer(x, hf, mod3, lay, state_wkv_fwd_l3, state_wkv_bwd_l3, l3_rwkv_mu, l3_rwkv_w_r,
                                      l3_rwkv_w_k, l3_rwkv_w_v, l3_rwkv_w_o, l3_rwkv_w0, l3_rwkv_w1, l3_rwkv_w2,
                                      l3_rwkv_a0, l3_rwkv_a1, l3_rwkv_a2, l3_rwkv_g1, l3_rwkv_g2, l3_rwkv_k_k,
                                      l3_rwkv_k_a, l3_rwkv_r_k, l3_rwkv_ln_g, l3_rwkv_ln_b)
    x = _moe_ffn(x, l3_norm_ffn, mod3, lay, l3_moe_w_router, l3_moe_b_router, l3_moe_w_gate, l3_moe_w_up,
                 l3_moe_w_down)

    y = _norm_mod(x, norm_out, None, None, lay, F32, name="norm_out")
    y_prompt = y[:lay.n_ctx].reshape(n_ctx_seq, ctx_len, d)
    y_sample = y[lay.n_ctx:].reshape(n_lat_seq, lat_len, d)
    return (y_prompt, y_sample, ckv_l0, krope_l0, wkv_fwd, wkv_bwd)
```

```python
import functools
import math

import jax
import jax.numpy as jnp
import numpy as np
from jax import lax
from jax.experimental import pallas as pl
from jax.experimental.pallas import tpu as pltpu

F32 = jnp.float32
BF16 = jnp.bfloat16

NORM_EPS = 1e-6
LN_EPS = 1e-5
GN_EPS = 64e-5
QK_NOPE = 128
QK_ROPE = 64
V_DIM = 128
ROPE_THETA = 10000.0
GRID_W = 64
RWKV_N = 64
N_MOD = 6
TOP_K = 2

LANES = 128
SUBLANES = 8
VMEM_LIMIT_BYTES = 56 * 1024 * 1024

WKV_CHUNK = 64
WKV_PAIRS_PER_STEP = 4
WKV_CHUNKS_PER_STEP = 4
MOE_ROWS = 512


def _cp(n_axes, arbitrary_last=False):
    sem = ["parallel"] * n_axes
    if arbitrary_last:
        sem[-1] = "arbitrary"
    return pltpu.CompilerParams(dimension_semantics=tuple(sem), vmem_limit_bytes=VMEM_LIMIT_BYTES)


def _tile(pref, *sizes):
    t = pref
    while any(s % t for s in sizes):
        t //= 2
    assert t >= SUBLANES
    return t


class _Lay:
    def __init__(self, n_ctx_seq, ctx_len, n_lat_seq, lat_len):
        self.n_ctx_seq, self.ctx_len, self.n_lat_seq, self.lat_len = n_ctx_seq, ctx_len, n_lat_seq, lat_len
        self.n_ctx = n_ctx_seq * ctx_len
        self.n_tok = self.n_ctx + n_lat_seq * lat_len

    def tile(self, pref):
        return _tile(pref, self.n_ctx, self.lat_len)

    def group(self, start):
        return jnp.where(start < self.n_ctx, 0, 1 + (start - self.n_ctx) // self.lat_len)

    def seq_pos(self, start):
        is_ctx = start < self.n_ctx
        pos = jnp.where(is_ctx, start % self.ctx_len, (start - self.n_ctx) % self.lat_len)
        return pos, jnp.where(is_ctx, self.ctx_len, self.lat_len)


def _silu(x):
    return x * jax.nn.sigmoid(x)


def _gelu_tanh(x):
    return 0.5 * x * (1.0 + jnp.tanh(math.sqrt(2.0 / math.pi) * (x + 0.044715 * (x * x * x))))


def _softplus(x):
    return jnp.maximum(x, 0.0) + jnp.log(1.0 + jnp.exp(-jnp.abs(x)))


def _split_bf16(x, n):
    parts = []
    r = x
    for _ in range(n):
        h = r.astype(BF16)
        parts.append(h)
        r = r - h.astype(F32)
    return parts


def _dot(a, b):
    return jnp.dot(a, b, preferred_element_type=F32)


def _dot_nt(a, b):
    return lax.dot_general(a, b, (((1,), (1,)), ((), ())), preferred_element_type=F32)


def _dot_tn(a, b):
    return lax.dot_general(a, b, (((0,), (0,)), ((), ())), preferred_element_type=F32)


def _pair_ones():
    r = lax.broadcasted_iota(jnp.int32, (LANES, LANES), 0) // RWKV_N
    c = lax.broadcasted_iota(jnp.int32, (LANES, LANES), 1) // RWKV_N
    return jnp.where(r == c, 1.0, 0.0).astype(BF16)


def _head_sum(x):
    ones = _pair_ones()
    cols = []
    for t in range(x.shape[1] // LANES):
        xt = x[:, t * LANES:(t + 1) * LANES]
        cols.append(sum(_dot(p, ones) for p in _split_bf16(xt, 3)))
    return jnp.concatenate(cols, axis=1)


def _mm(a, ws, *, tm, tn, gn, epi, outs, vecs=(), wcols=None, n_outer=False, name="mm"):
    m, k = a.shape
    gm = m // tm
    nw, nv = len(ws), len(vecs)
    wcols = wcols or [lambda j: j] * nw
    if n_outer:
        grid = (gn, gm)
        wrap = lambda fn: (lambda g0, g1: fn(g1, g0))
    else:
        grid = (gm, gn)
        wrap = lambda fn: (lambda g0, g1: fn(g0, g1))
    in_specs = [pl.BlockSpec((tm, k), wrap(lambda i, j: (i, 0)))]
    for wc in wcols:
        in_specs.append(pl.BlockSpec((k, tn), wrap(lambda i, j, wc=wc: (0, wc(j)))))
    for _, bshape, fn in vecs:
        in_specs.append(pl.BlockSpec(bshape, wrap(fn)))
    out_shape = [jax.ShapeDtypeStruct(s, d) for s, d, _, _ in outs]
    out_specs = [pl.BlockSpec(b, wrap(fn)) for _, _, b, fn in outs]

    def body(*refs):
        a_ref = refs[0]
        w_refs = refs[1:1 + nw]
        v_refs = refs[1 + nw:1 + nw + nv]
        o_refs = refs[1 + nw + nv:]
        av = a_ref[...]
        if av.dtype != BF16:
            av = av.astype(BF16)
        accs = [_dot(av, w[...].astype(BF16)) for w in w_refs]
        for o, r in zip(o_refs, epi(accs, v_refs)):
            o[...] = r.reshape(o.shape).astype(o.dtype)

    res = pl.pallas_call(body, grid=grid, in_specs=in_specs, out_specs=out_specs, out_shape=out_shape,
                         compiler_params=_cp(2), name=name)(a, *ws, *[v[0] for v in vecs])
    return res


def _bias_vec(b, tn):
    return (b.reshape(1, -1), (1, tn), lambda i, j: (0, j))


def _mm_act(a, w, bias, act, out_dtype, *, tm, tn, name, n_outer=False):
    m, n = a.shape[0], w.shape[1]
    vecs = [] if bias is None else [_bias_vec(bias, tn)]

    def epi(accs, v):
        x = accs[0]
        if bias is not None:
            x = x + v[0][...]
        return [act(x) if act is not None else x]

    return _mm(a, [w], tm=tm, tn=tn, gn=pl.cdiv(n, tn), epi=epi, vecs=vecs, n_outer=n_outer, name=name,
               outs=[((m, n), out_dtype, (tm, tn), lambda i, j: (i, j))])[0]


def _mm_dual(a, w1, w2, b1, b2, fn, out_dtype, *, n, col2, tm, tn, name, n_outer=True):
    m = a.shape[0]
    gn = pl.cdiv(n, tn)
    vecs = []
    if b1 is not None:
        vecs = [(b1.reshape(1, -1), (1, tn), lambda i, j: (0, j)),
                (b2.reshape(1, -1), (1, tn), lambda i, j: (0, j + col2))]

    def epi(accs, v):
        x, y = accs
        if b1 is not None:
            x, y = x + v[0][...], y + v[1][...]
        return [fn(x, y)]

    return _mm(a, [w1, w2], tm=tm, tn=tn, gn=gn, epi=epi, vecs=vecs, n_outer=n_outer, name=name,
               wcols=[lambda j: j, lambda j: j + col2],
               outs=[((m, n), out_dtype, (tm, tn), lambda i, j: (i, j))])[0]


def _mm_resid(a, w, bias, resid, mod3, gate_row, lay, *, tm, tn, name, n_outer=False):
    m, n = resid.shape
    vecs = [(mod3, (1, N_MOD, tn), lambda i, j: (lay.group(i * tm), 0, j)),
            (resid, (tm, tn), lambda i, j: (i, j))]
    if bias is not None:
        vecs.append(_bias_vec(bias, tn))

    def epi(accs, v):
        x = accs[0]
        if bias is not None:
            x = x + v[2][...]
        return [v[1][...] + v[0][0, gate_row:gate_row + 1, :] * x]

    return _mm(a, [w], tm=tm, tn=tn, gn=n // tn, epi=epi, vecs=vecs, n_outer=n_outer, name=name,
               outs=[((m, n), F32, (tm, tn), lambda i, j: (i, j))])[0]


def _ada_mod(cond8, w_mod, b_mod):
    d, n = w_mod.shape
    tn = _tile(1536, n)

    def body(c_ref, w_ref, b_ref, o_ref):
        c = c_ref[...]
        o_ref[...] = _dot(_silu(c).astype(BF16), w_ref[...].astype(BF16)) + b_ref[...]

    m = pl.pallas_call(
        body, grid=(n // tn,),
        in_specs=[pl.BlockSpec((SUBLANES, d), lambda j: (0, 0)), pl.BlockSpec((d, tn), lambda j: (0, j)),
                  pl.BlockSpec((1, tn), lambda j: (0, j))],
        out_specs=pl.BlockSpec((SUBLANES, tn), lambda j: (0, j)),
        out_shape=jax.ShapeDtypeStruct((SUBLANES, n), F32), compiler_params=_cp(1), name="ada_mod",
    )(cond8, w_mod, b_mod.reshape(1, n))
    return m.reshape(SUBLANES, N_MOD, d)


def _norm_mod(x, gain, mod3, rows, lay, out_dtype, *, router=None, name="norm_mod"):
    m, d = x.shape
    tm = lay.tile(256)
    n_e = None if router is None else router[3]

    def body(*refs):
        x_ref, g_ref = refs[0], refs[1]
        pos = 2
        xf = x_ref[...]
        y = xf * lax.rsqrt(jnp.mean(xf * xf, axis=-1, keepdims=True) + NORM_EPS) * g_ref[...]
        if rows is not None:
            mod_ref = refs[pos]
            pos += 1
            y = y * (1.0 + mod_ref[0, rows[1]:rows[1] + 1, :]) + mod_ref[0, rows[0]:rows[0] + 1, :]
        if router is not None:
            whi_ref, wlo_ref, rb_ref = refs[pos:pos + 3]
            pos += 3
        o_ref = refs[pos]
        o_ref[...] = y.astype(o_ref.dtype)
        if router is not None:
            e_ref, p_ref = refs[pos + 1], refs[pos + 2]
            y_hi, y_lo = _split_bf16(y, 2)
            logits = _dot(y_hi, whi_ref[...]) + _dot(y_lo, whi_ref[...]) + _dot(y_hi, wlo_ref[...]) + rb_ref[...]
            lane = lax.broadcasted_iota(jnp.int32, logits.shape, 1)
            logits = jnp.where(lane < n_e, logits, -jnp.inf)
            m1 = jnp.max(logits, axis=-1, keepdims=True)
            i1 = jnp.min(jnp.where(logits == m1, lane, LANES), axis=-1, keepdims=True)
            rest = jnp.where(lane == i1, -jnp.inf, logits)
            m2 = jnp.max(rest, axis=-1, keepdims=True)
            i2 = jnp.min(jnp.where(rest == m2, lane, LANES), axis=-1, keepdims=True)
            e2 = jnp.exp(m2 - m1)
            p1 = 1.0 / (1.0 + e2)
            e_ref[...] = jnp.where(lane == 0, i1, jnp.where(lane == 1, i2, 0))
            p_ref[...] = jnp.where(lane == 0, p1, jnp.where(lane == 1, e2 * p1, 0.0))

    in_specs = [pl.BlockSpec((tm, d), lambda i: (i, 0)), pl.BlockSpec((1, d), lambda i: (0, 0))]
    args = [x, gain.reshape(1, d)]
    if rows is not None:
        in_specs.append(pl.BlockSpec((1, N_MOD, d), lambda i: (lay.group(i * tm), 0, 0)))
        args.append(mod3)
    out_shape = [jax.ShapeDtypeStruct((m, d), out_dtype)]
    out_specs = [pl.BlockSpec((tm, d), lambda i: (i, 0))]
    if router is not None:
        in_specs += [pl.BlockSpec((d, LANES), lambda i: (0, 0)), pl.BlockSpec((d, LANES), lambda i: (0, 0)),
                     pl.BlockSpec((1, LANES), lambda i: (0, 0))]
        args += list(router[:3])
        out_shape += [jax.ShapeDtypeStruct((m, LANES), jnp.int32), jax.ShapeDtypeStruct((m, LANES), F32)]
        out_specs += [pl.BlockSpec((tm, LANES), lambda i: (i, 0))] * 2
    res = pl.pallas_call(body, grid=(m // tm,), in_specs=in_specs, out_specs=out_specs, out_shape=out_shape,
                         compiler_params=_cp(1), name=name)(*args)
    return res if router is not None else res[0]


def _dense_ffn(x, h, w_gate, w_up, w_down, mod3, lay):
    d_ff = w_gate.shape[1]
    hid = _mm_dual(h, w_gate, w_up, None, None, lambda g, u: _silu(g) * u, BF16, n=d_ff, col2=0,
                   tm=lay.tile(1024), tn=512, name="ffn_up")
    return _mm_resid(hid, w_down, None, x, mod3, 5, lay, tm=lay.tile(512), tn=512, n_outer=True,
                     name="ffn_down")


def _moe_up(xs, w_gate, w_up, block_e, block_src, n_used, *, tm):
    n_slots, d = xs.shape
    n_e, _, d_ff = w_gate.shape
    tn = _tile(1024, d_ff)
    n_blocks = n_slots // tm

    def body(be_ref, bs_ref, nu_ref, x_ref, wg_ref, wu_ref, o_ref):
        @pl.when(pl.program_id(1) < nu_ref[0])
        def _():
            xv = x_ref[...]
            g = _dot(xv, wg_ref[...].astype(BF16))
            u = _dot(xv, wu_ref[...].astype(BF16))
            o_ref[...] = (_silu(g) * u).astype(o_ref.dtype)

    grid_spec = pltpu.PrefetchScalarGridSpec(
        num_scalar_prefetch=3, grid=(d_ff // tn, n_blocks),
        in_specs=[pl.BlockSpec((tm, d), lambda j, i, be, bs, nu: (bs[i], 0)),
                  pl.BlockSpec((None, d, tn), lambda j, i, be, bs, nu: (be[i], 0, j)),
                  pl.BlockSpec((None, d, tn), lambda j, i, be, bs, nu: (be[i], 0, j))],
        out_specs=pl.BlockSpec((tm, tn), lambda j, i, be, bs, nu: (bs[i], j)))
    return pl.pallas_call(body, grid_spec=grid_spec, out_shape=jax.ShapeDtypeStruct((n_slots, d_ff), BF16),
                          compiler_params=_cp(2, arbitrary_last=True), name="moe_up"
                          )(block_e, block_src, n_used, xs, w_gate, w_up)


def _moe_down(hid, w_down, block_e, block_src, n_used, *, tm):
    n_slots, d_ff = hid.shape
    d = w_down.shape[2]
    tn = _tile(512, d)
    n_blocks = n_slots // tm

    def body(be_ref, bs_ref, nu_ref, h_ref, w_ref, o_ref):
        @pl.when(pl.program_id(1) < nu_ref[0])
        def _():
            o_ref[...] = _dot(h_ref[...], w_ref[...].astype(BF16))

    grid_spec = pltpu.PrefetchScalarGridSpec(
        num_scalar_prefetch=3, grid=(d // tn, n_blocks),
        in_specs=[pl.BlockSpec((tm, d_ff), lambda j, i, be, bs, nu: (bs[i], 0)),
                  pl.BlockSpec((None, d_ff, tn), lambda j, i, be, bs, nu: (be[i], 0, j))],
        out_specs=pl.BlockSpec((tm, tn), lambda j, i, be, bs, nu: (bs[i], j)))
    return pl.pallas_call(body, grid_spec=grid_spec, out_shape=jax.ShapeDtypeStruct((n_slots, d), F32),
                          compiler_params=_cp(2, arbitrary_last=True), name="moe_down"
                          )(block_e, block_src, n_used, hid, w_down)


def _moe_combine(x, y0, y1, top_p, mod3, lay):
    m, d = x.shape
    tm = lay.tile(256)

    def body(x_ref, a_ref, b_ref, p_ref, mod_ref, o_ref):
        mix = p_ref[:, 0:1] * a_ref[...] + p_ref[:, 1:2] * b_ref[...]
        o_ref[...] = x_ref[...] + mod_ref[0, 5:6, :] * mix

    row = pl.BlockSpec((tm, d), lambda i: (i, 0))
    return pl.pallas_call(
        body, grid=(m // tm,),
        in_specs=[row, row, row, pl.BlockSpec((tm, LANES), lambda i: (i, 0)),
                  pl.BlockSpec((1, N_MOD, d), lambda i: (lay.group(i * tm), 0, 0))],
        out_specs=row, out_shape=jax.ShapeDtypeStruct((m, d), F32), compiler_params=_cp(1), name="moe_combine",
    )(x, y0, y1, top_p, mod3)


def _moe_ffn(x, norm_g, mod3, lay, w_router, b_router, w_gate, w_up, w_down):
    n_tok, d = x.shape
    n_e = w_router.shape[1]
    wr = jnp.zeros((d, LANES), F32).at[:, :n_e].set(w_router)
    wr_hi = wr.astype(BF16)
    wr_lo = (wr - wr_hi.astype(F32)).astype(BF16)
    rb = jnp.zeros((1, LANES), F32).at[0, :n_e].set(b_router.astype(F32))
    h, top_e, top_p = _norm_mod(x, norm_g, mod3, (3, 4), lay, BF16, router=(wr_hi, wr_lo, rb, n_e),
                                name="norm_router")
    tm = MOE_ROWS
    nk = n_tok * TOP_K
    flat_e = top_e[:, :TOP_K].reshape(nk)
    onehot = (flat_e[:, None] == jnp.arange(n_e, dtype=jnp.int32)[None, :]).astype(jnp.int32)
    rank = jnp.take_along_axis(jnp.cumsum(onehot, axis=0) - onehot, flat_e[:, None], axis=1)[:, 0]
    counts = jnp.sum(onehot, axis=0)
    padded = (counts + tm - 1) // tm * tm
    pad_end = jnp.cumsum(padded)
    dest = (pad_end - padded)[flat_e] + rank
    n_blocks = -(-nk // tm) + n_e
    n_slots = n_blocks * tm
    slot_tok = jnp.zeros((n_slots,), jnp.int32).at[dest].set(jnp.arange(nk, dtype=jnp.int32) // TOP_K)
    n_used = (pad_end[-1] // tm).astype(jnp.int32)
    blk = jnp.arange(n_blocks, dtype=jnp.int32)
    block_src = jnp.minimum(blk, n_used - 1)
    block_e = jnp.minimum(jnp.searchsorted(pad_end, block_src * tm, side="right"), n_e - 1).astype(jnp.int32)
    xs = jnp.take(h, slot_tok, axis=0, mode="clip")
    hid = _moe_up(xs, w_gate, w_up, block_e, block_src, n_used.reshape(1), tm=tm)
    ys = _moe_down(hid, w_down, block_e, block_src, n_used.reshape(1), tm=tm)
    pos = dest.reshape(n_tok, TOP_K)
    return _moe_combine(x, jnp.take(ys, pos[:, 0], axis=0, mode="clip"),
                        jnp.take(ys, pos[:, 1], axis=0, mode="clip"), top_p, mod3, lay)


def _mla_mid(dq, q_norm, kv_norm, cos64, sin64, q_lora, kv_lora, lay):
    m, n = dq.shape
    tm = lay.tile(512)
    o_kd, o_ko, o_ks = q_lora + kv_lora, q_lora + kv_lora + QK_ROPE, q_lora + kv_lora + 2 * QK_ROPE

    def body(x_ref, qg_ref, kg_ref, cos_ref, sin_ref, cq_ref, ckv_ref, kro_ref, krr_ref):
        cq = x_ref[:, 0:q_lora]
        cq_ref[...] = (cq * lax.rsqrt(jnp.mean(cq * cq, axis=-1, keepdims=True) + NORM_EPS)
                       * qg_ref[...]).astype(cq_ref.dtype)
        ckv = x_ref[:, q_lora:q_lora + kv_lora]
        ckv_ref[...] = ckv * lax.rsqrt(jnp.mean(ckv * ckv, axis=-1, keepdims=True) + NORM_EPS) * kg_ref[...]
        kro_ref[...] = x_ref[:, o_ko:o_ko + QK_ROPE]
        krr_ref[...] = (x_ref[:, o_kd:o_kd + QK_ROPE] * cos_ref[...]
                        + x_ref[:, o_ks:o_ks + QK_ROPE] * sin_ref[...]).astype(krr_ref.dtype)

    rope = pl.BlockSpec((tm, QK_ROPE), lambda i: (i, 0))
    return pl.pallas_call(
        body, grid=(m // tm,),
        in_specs=[pl.BlockSpec((tm, n), lambda i: (i, 0)), pl.BlockSpec((1, q_lora), lambda i: (0, 0)),
                  pl.BlockSpec((1, kv_lora), lambda i: (0, 0)), rope, rope],
        out_specs=[pl.BlockSpec((tm, q_lora), lambda i: (i, 0)), pl.BlockSpec((tm, kv_lora), lambda i: (i, 0)),
                   rope, rope],
        out_shape=[jax.ShapeDtypeStruct((m, q_lora), BF16), jax.ShapeDtypeStruct((m, kv_lora), F32),
                   jax.ShapeDtypeStruct((m, QK_ROPE), F32), jax.ShapeDtypeStruct((m, QK_ROPE), BF16)],
        compiler_params=_cp(1), name="mla_mid",
    )(dq, q_norm.reshape(1, -1), kv_norm.reshape(1, -1), cos64, sin64)


def _mla_q(cq, w_uq_ext, cos64, sin64, n_heads, tm):
    m = cq.shape[0]
    hw = QK_NOPE + 2 * QK_ROPE
    dqk = QK_NOPE + QK_ROPE
    scale = dqk ** -0.5

    def epi(accs, v):
        x = accs[0]
        rot = x[:, QK_NOPE:dqk] * v[0][...] + x[:, dqk:hw] * v[1][...]
        return [jnp.concatenate([x[:, :QK_NOPE], rot], axis=1) * scale]

    rope = lambda arr: (arr, (tm, QK_ROPE), lambda i, j: (i, 0))
    return _mm(cq, [w_uq_ext], tm=tm, tn=hw, gn=n_heads, epi=epi, vecs=[rope(cos64), rope(sin64)], name="mla_q",
               outs=[((n_heads, m, dqk), BF16, (1, tm, dqk), lambda i, j: (j, i, 0))])[0]


def _mla_kv(ckv, w_ukv, krope, n_heads, tm):
    m = ckv.shape[0]
    dqk = QK_NOPE + QK_ROPE

    def epi(accs, v):
        x = accs[0]
        return [jnp.concatenate([x[:, :QK_NOPE], v[0][...].astype(F32)], axis=1), x[:, QK_NOPE:]]

    return _mm(ckv, [w_ukv], tm=tm, tn=QK_NOPE + V_DIM, gn=n_heads, epi=epi, name="mla_kv",
               vecs=[(krope, (tm, QK_ROPE), lambda i, j: (i, 0))],
               outs=[((n_heads, m, dqk), BF16, (1, tm, dqk), lambda i, j: (j, i, 0)),
                     ((n_heads, m, V_DIM), BF16, (1, tm, V_DIM), lambda i, j: (j, i, 0))])


def _attn_context(q, k, v, lay):
    n_heads = q.shape[0]
    t = lay.ctx_len
    dqk = q.shape[2]

    def body(q_ref, k_ref, v_ref, o_ref):
        for h in range(n_heads):
            s = _dot_nt(q_ref[h], k_ref[h])
            p = jnp.exp(s - jnp.max(s, axis=-1, keepdims=True))
            l = jnp.sum(p, axis=-1, keepdims=True)
            o = _dot(p.astype(BF16), v_ref[h]) / l
            o_ref[:, h * V_DIM:(h + 1) * V_DIM] = o.astype(o_ref.dtype)

    return pl.pallas_call(
        body, grid=(lay.n_ctx_seq,),
        in_specs=[pl.BlockSpec((n_heads, t, dqk), lambda b: (0, b, 0)),
                  pl.BlockSpec((n_heads, t, dqk), lambda b: (0, b, 0)),
                  pl.BlockSpec((n_heads, t, V_DIM), lambda b: (0, b, 0))],
        out_specs=pl.BlockSpec((t, n_heads * V_DIM), lambda b: (b, 0)),
        out_shape=jax.ShapeDtypeStruct((lay.n_ctx, n_heads * V_DIM), BF16),
        compiler_params=_cp(1), name="attn_context",
    )(q, k, v)


def _attn_latent(q, k, v, kc, vc, lay, past_len):
    n_heads = q.shape[0]
    t = lay.lat_len
    dqk = q.shape[2]
    tq = _tile(512, t)
    lat0 = lay.n_ctx // t
    q0 = lay.n_ctx // tq

    kch = _tile(1024, t)

    def body(q_ref, k_ref, v_ref, kc_ref, vc_ref, o_ref):
        qv = q_ref[0]
        s = [_dot_nt(qv, k_ref[0, j * kch:(j + 1) * kch, :]) for j in range(t // kch)] + [_dot_nt(qv, kc_ref[0])]
        vs = [v_ref[0, j * kch:(j + 1) * kch, :] for j in range(t // kch)] + [vc_ref[0]]
        mx = functools.reduce(jnp.maximum, [jnp.max(x, axis=-1, keepdims=True) for x in s])
        l = jnp.zeros((tq, 1), F32)
        acc = jnp.zeros((tq, V_DIM), F32)
        for x, vv in zip(s, vs):
            p = jnp.exp(x - mx)
            l = l + jnp.sum(p, axis=-1, keepdims=True)
            acc = acc + _dot(p.astype(BF16), vv)
        o_ref[...] = (acc / l).astype(o_ref.dtype)

    return pl.pallas_call(
        body, grid=(lay.n_lat_seq, n_heads, t // tq),
        in_specs=[pl.BlockSpec((1, tq, dqk), lambda b, h, i: (h, q0 + b * (t // tq) + i, 0)),
                  pl.BlockSpec((1, t, dqk), lambda b, h, i: (h, lat0 + b, 0)),
                  pl.BlockSpec((1, t, V_DIM), lambda b, h, i: (h, lat0 + b, 0)),
                  pl.BlockSpec((1, past_len, dqk), lambda b, h, i: (h, b, 0)),
                  pl.BlockSpec((1, past_len, V_DIM), lambda b, h, i: (h, b, 0))],
        out_specs=pl.BlockSpec((tq, V_DIM), lambda b, h, i: (b * (t // tq) + i, h)),
        out_shape=jax.ShapeDtypeStruct((lay.n_lat_seq * t, n_heads * V_DIM), BF16),
        compiler_params=_cp(3), name="attn_latent",
    )(q, k, v, kc, vc)


def _rope_tables(lay):
    n = lay.lat_len
    pairs = QK_ROPE // 4
    row_pos = (jnp.arange(n) // GRID_W).astype(F32)
    col_pos = (jnp.arange(n) % GRID_W).astype(F32)
    inv_freq = ROPE_THETA ** (-jnp.arange(pairs, dtype=F32) / pairs)
    ang = jnp.concatenate([row_pos[:, None] * inv_freq, col_pos[:, None] * inv_freq], axis=-1)
    cos, sin = jnp.cos(ang), jnp.sin(ang)
    cos64 = jnp.concatenate([cos, cos], axis=-1)
    sin64 = jnp.concatenate([-sin, sin], axis=-1)
    one = jnp.ones((lay.n_ctx, QK_ROPE), F32)
    cos_all = jnp.concatenate([one] + [cos64] * lay.n_lat_seq, axis=0)
    sin_all = jnp.concatenate([0.0 * one] + [sin64] * lay.n_lat_seq, axis=0)
    return cos_all, sin_all


def _mla_layer(x, h, mod3, lay, cache_ckv, cache_krope, w_down, q_norm, kv_norm, w_uq, w_ukv, w_o):
    d = x.shape[1]
    q_lora, kv_lora = q_norm.shape[0], kv_norm.shape[0]
    n_heads = w_o.shape[0] // V_DIM
    dqk = QK_NOPE + QK_ROPE
    perm_d = np.concatenate([np.arange(0, QK_ROPE, 2), np.arange(1, QK_ROPE, 2)])
    perm_s = np.concatenate([np.arange(1, QK_ROPE, 2), np.arange(0, QK_ROPE, 2)])
    kr0 = q_lora + kv_lora
    w_down_ext = jnp.concatenate(
        [w_down[:, :kr0], w_down[:, kr0 + perm_d], w_down[:, kr0:], w_down[:, kr0 + perm_s],
         jnp.zeros((d, QK_ROPE), F32)], axis=1)
    w_uq_h = w_uq.reshape(q_lora, n_heads, dqk)
    w_uq_ext = jnp.concatenate([w_uq_h[:, :, :QK_NOPE], w_uq_h[:, :, QK_NOPE + perm_d],
                                w_uq_h[:, :, QK_NOPE + perm_s]], axis=2).reshape(q_lora, -1)
    cos64, sin64 = _rope_tables(lay)
    tm = lay.tile(1024)
    dq = _mm_act(h, w_down_ext, None, None, F32, tm=tm, tn=512, name="mla_down")
    cq, ckv, krope_raw, krope_rot = _mla_mid(dq, q_norm, kv_norm, cos64, sin64, q_lora, kv_lora, lay)
    q = _mla_q(cq, w_uq_ext, cos64, sin64, n_heads, tm)
    k, v = _mla_kv(ckv, w_ukv, krope_rot, n_heads, tm)
    n_past = cache_ckv.shape[0] * cache_ckv.shape[1]
    kc, vc = _mla_kv(cache_ckv.reshape(n_past, kv_lora), w_ukv,
                     cache_krope.reshape(n_past, QK_ROPE)[:, perm_d].astype(BF16), n_heads,
                     _tile(512, cache_ckv.shape[1]))
    o_ctx = _attn_context(q, k, v, lay)
    o_lat = _attn_latent(q, k, v, kc, vc, lay, cache_ckv.shape[1])
    attn = jnp.concatenate([o_ctx, o_lat], axis=0)
    x = _mm_resid(attn, w_o, None, x, mod3, 2, lay, tm=tm, tn=512, name="mla_out")
    ckv_ctx = ckv[:lay.n_ctx].reshape(lay.n_ctx_seq, lay.ctx_len, kv_lora)
    krope_ctx = krope_raw[:lay.n_ctx].reshape(lay.n_ctx_seq, lay.ctx_len, QK_ROPE)
    return x, ckv_ctx, krope_ctx


def _conv_dw(u, w_dw, b_dw, ln_g, ln_b, lay):
    m, d = u.shape
    width = w_dw.shape[0]
    pad = width // 2
    t = lay.tile(256)
    halo = 2 * SUBLANES
    assert pad <= halo
    rb = _tile(64, t)

    sh_rows = t + halo + SUBLANES

    def body(prev_ref, cur_ref, nxt_ref, w_ref, bdw_ref, g_ref, b_ref, o_ref, ext_ref, dw_ref, sh_ref):
        pos, ln = lay.seq_pos(pl.program_id(0) * t)
        ext_ref[0:halo, :] = jnp.where(pos > 0, prev_ref[...], 0.0)
        ext_ref[halo:halo + t, :] = cur_ref[...]
        ext_ref[halo + t:halo + t + halo, :] = jnp.where(pos + t < ln, nxt_ref[...], 0.0)

        def chunk(c, carry):
            ls = pl.ds(pl.multiple_of(c * LANES, LANES), LANES)
            for s in range(SUBLANES):
                sh_ref[s] = ext_ref[pl.ds(s, sh_rows), ls]
            for r in range(t // rb):
                acc = jnp.zeros((rb, LANES), F32)
                for j in range(width):
                    off = halo - pad + j
                    acc = acc + (sh_ref[off % SUBLANES, pl.ds(off // SUBLANES * SUBLANES + r * rb, rb), :]
                                 * w_ref[pl.ds(j, 1), ls])
                dw_ref[pl.ds(r * rb, rb), ls] = acc + bdw_ref[:, ls]
            return carry

        lax.fori_loop(0, d // LANES, chunk, 0)
        xv = dw_ref[...]
        mu = jnp.mean(xv, axis=-1, keepdims=True)
        xc = xv - mu
        var = jnp.mean(xc * xc, axis=-1, keepdims=True)
        o_ref[...] = _silu(xc * lax.rsqrt(var + LN_EPS) * g_ref[...] + b_ref[...]).astype(o_ref.dtype)

    hb = t // halo
    vec = pl.BlockSpec((1, d), lambda i: (0, 0))
    return pl.pallas_call(
        body, grid=(m // t,),
        in_specs=[pl.BlockSpec((halo, d), lambda i: (jnp.maximum(i * hb - 1, 0), 0)),
                  pl.BlockSpec((t, d), lambda i: (i, 0)),
                  pl.BlockSpec((halo, d), lambda i: (jnp.minimum((i + 1) * hb, m // halo - 1), 0)),
                  pl.BlockSpec((width, d), lambda i: (0, 0)), vec, vec, vec],
        out_specs=pl.BlockSpec((t, d), lambda i: (i, 0)),
        out_shape=jax.ShapeDtypeStruct((m, d), BF16),
        scratch_shapes=[pltpu.VMEM((t + 2 * halo, d), F32), pltpu.VMEM((t, d), F32),
                        pltpu.VMEM((SUBLANES, sh_rows, LANES), F32)],
        compiler_params=_cp(1), name="conv_dw",
    )(u, u, u, w_dw, b_dw.reshape(1, d), ln_g.reshape(1, d), ln_b.reshape(1, d))


def _conformer_layer(x, h, mod3, lay, w_in, b_in, w_dw, b_dw, ln_g, ln_b, w_out, b_out):
    d = x.shape[1]
    tm = lay.tile(1024)
    u = _mm_dual(h, w_in, w_in, b_in, b_in, lambda a, b: a * jax.nn.sigmoid(b), F32, n=d, col2=d // 512,
                 tm=tm, tn=512, name="conv_in")
    cv = _conv_dw(u, w_dw, b_dw, ln_g, ln_b, lay)
    return _mm_resid(cv, w_out, b_out, x, mod3, 2, lay, tm=tm, tn=512, name="conv_out")


def _sgu_mix(z, ln_g, ln_b, w_s, b_s, lay):
    m, d2 = z.shape
    d = d2 // 2
    groups, chunk, _ = w_s.shape
    gd = d // groups
    tm = lay.tile(2 * chunk)
    bs_t = jnp.transpose(b_s)

    def body(u_ref, v_ref, g_ref, b_ref, ws_ref, bs_ref, o_ref):
        xv = v_ref[...]
        mu = jnp.mean(xv, axis=-1, keepdims=True)
        xc = xv - mu
        var = jnp.mean(xc * xc, axis=-1, keepdims=True)
        vn = (xc * lax.rsqrt(var + LN_EPS) * g_ref[...] + b_ref[...]).astype(BF16)
        for c in range(tm // chunk):
            rs = slice(c * chunk, (c + 1) * chunk)
            for g in range(groups):
                ls = slice(g * gd, (g + 1) * gd)
                mixed = _dot(ws_ref[g].astype(BF16), vn[rs, ls]) + bs_ref[:, g:g + 1]
                o_ref[rs, ls] = (u_ref[rs, ls] * mixed).astype(o_ref.dtype)

    vec = pl.BlockSpec((1, d), lambda i: (0, 0))
    return pl.pallas_call(
        body, grid=(m // tm,),
        in_specs=[pl.BlockSpec((tm, d), lambda i: (i, 0)), pl.BlockSpec((tm, d), lambda i: (i, 1)), vec, vec,
                  pl.BlockSpec((groups, chunk, chunk), lambda i: (0, 0, 0)),
                  pl.BlockSpec((chunk, groups), lambda i: (0, 0))],
        out_specs=pl.BlockSpec((tm, d), lambda i: (i, 0)),
        out_shape=jax.ShapeDtypeStruct((m, d), BF16), compiler_params=_cp(1), name="sgu_mix",
    )(z, z, ln_g.reshape(1, d), ln_b.reshape(1, d), w_s, bs_t)


def _sgu_layer(x, h, mod3, lay, w_in, b_in, ln_g, ln_b, w_s, b_s, w_out, b_out):
    tm = lay.tile(1024)
    z = _mm_act(h, w_in, b_in, _gelu_tanh, F32, tm=tm, tn=512, name="sgu_in", n_outer=True)
    sm = _sgu_mix(z, ln_g, ln_b, w_s, b_s, lay)
    return _mm_resid(sm, w_out, b_out, x, mod3, 2, lay, tm=tm, tn=512, name="sgu_out")


def _shift_mix(h, mu, lay):
    m, d = h.shape
    t = lay.tile(256)
    halo = SUBLANES

    def body(prev_ref, cur_ref, nxt_ref, mu_ref, *rest):
        o_refs, ext_ref = rest[:6], rest[6]
        pos, ln = lay.seq_pos(pl.program_id(0) * t)
        ext_ref[0:halo, :] = jnp.where(pos > 0, prev_ref[...], 0.0)
        ext_ref[halo:halo + t, :] = cur_ref[...]
        ext_ref[halo + t:halo + t + halo, :] = jnp.where(pos + t < ln, nxt_ref[...], 0.0)
        xv = cur_ref[...]
        xx = 0.5 * (ext_ref[pl.ds(halo - 1, t), :] + ext_ref[pl.ds(halo + 1, t), :]) - xv
        for j in range(6):
            o_refs[j][...] = (xv + xx * mu_ref[j:j + 1, :]).astype(BF16)

    hb = t // halo
    row = pl.BlockSpec((t, d), lambda i: (i, 0))
    return pl.pallas_call(
        body, grid=(m // t,),
        in_specs=[pl.BlockSpec((halo, d), lambda i: (jnp.maximum(i * hb - 1, 0), 0)), row,
                  pl.BlockSpec((halo, d), lambda i: (jnp.minimum((i + 1) * hb, m // halo - 1), 0)),
                  pl.BlockSpec((6, d), lambda i: (0, 0))],
        out_specs=[row] * 6, out_shape=[jax.ShapeDtypeStruct((m, d), BF16)] * 6,
        scratch_shapes=[pltpu.VMEM((t + 2 * halo, d), F32)],
        compiler_params=_cp(1), name="rwkv_shift_mix",
    )(h, h, h, mu)


def _wkv_prep(r, k, v, wl, al, k_k, k_a, r_k):
    m, d = r.shape
    tm = _tile(128, m)

    def body(r_ref, k_ref, v_ref, wl0_ref, wl1_ref, al0_ref, al1_ref, kk_ref, ka_ref, rk_ref,
             okk_ref, olw_ref, ob_ref, okt_ref, obon_ref):
        rv, kv, vv = r_ref[...], k_ref[...], v_ref[...]
        kk = kv * kk_ref[...]
        kk = kk / jnp.maximum(jnp.sqrt(_head_sum(kk * kk)), 1e-12)
        okk_ref[...] = kk
        bonus = jnp.zeros_like(rv)
        for dr, (wl_ref, al_ref) in enumerate(((wl0_ref, al0_ref), (wl1_ref, al1_ref))):
            w_log = -_softplus(-wl_ref[...]) - 0.5
            olw_ref[dr] = -jnp.exp(w_log)
            a = jax.nn.sigmoid(al_ref[...])
            kt = kv * (1.0 + (a - 1.0) * ka_ref[...])
            ob_ref[dr] = kk * a
            okt_ref[dr] = kt
            bonus = bonus + _head_sum(rv * kt * rk_ref[...]) * vv
        obon_ref[...] = bonus

    row = pl.BlockSpec((tm, d), lambda i: (i, 0))
    vec = pl.BlockSpec((1, d), lambda i: (0, 0))
    two = pl.BlockSpec((2, tm, d), lambda i: (0, i, 0))
    return pl.pallas_call(
        body, grid=(m // tm,),
        in_specs=[row] * 7 + [vec] * 3,
        out_specs=[row, two, two, two, row],
        out_shape=[jax.ShapeDtypeStruct((m, d), F32)] + [jax.ShapeDtypeStruct((2, m, d), F32)] * 3
        + [jax.ShapeDtypeStruct((m, d), F32)],
        compiler_params=_cp(1), name="wkv_prep",
    )(r, k, v, wl[0], wl[1], al[0], al[1], k_k.reshape(1, d), k_a.reshape(1, d), r_k.reshape(1, d))


def _wkv_scan_dir(r, kk, v, lw, b, kt, s_init, lay, reverse):
    m, d = r.shape
    c = WKV_CHUNK
    rows = c * WKV_CHUNKS_PER_STEP
    n_pairs = d // LANES
    pps = math.gcd(WKV_PAIRS_PER_STEP, n_pairs)
    pw = pps * LANES
    n_steps = m // rows
    ctx_steps = lay.n_ctx // rows
    sps_ctx, sps_lat = lay.ctx_len // rows, lay.lat_len // rows
    n_seq = lay.n_ctx_seq + lay.n_lat_seq
    c2 = 2 * c
    sgn = -1 if reverse else 1

    def block_of(s):
        return n_steps - 1 - s if reverse else s

    def seq_of(tb):
        is_ctx = tb < ctx_steps
        seq = jnp.where(is_ctx, tb // sps_ctx, lay.n_ctx_seq + (tb - ctx_steps) // sps_lat)
        pos = jnp.where(is_ctx, tb % sps_ctx, (tb - ctx_steps) % sps_lat)
        return is_ctx, seq, pos, jnp.where(is_ctx, sps_ctx, sps_lat)

    def body(r_ref, kk_ref, v_ref, lw_ref, b_ref, kt_ref, si_ref, y_ref, so_ref, s_scr):
        is_ctx, _, pos, sps = seq_of(block_of(pl.program_id(1)))
        first = pos == (sps - 1 if reverse else 0)
        last = pos == (0 if reverse else sps - 1)

        @pl.when(first)
        def _():
            s_scr[...] = jnp.where(is_ctx, 0.0, si_ref[0])

        ri = lax.broadcasted_iota(jnp.int32, (c2, c2), 0)
        ci = lax.broadcasted_iota(jnp.int32, (c2, c2), 1)
        same = (ri // c) == (ci // c)
        strict = same & ((ri - ci) * sgn > 0)
        incl = same & ((ri - ci) * sgn >= 0)
        eye = jnp.where(ri == ci, 1.0, 0.0)
        ti = lax.broadcasted_iota(jnp.int32, (c, c), 0)
        tj = lax.broadcasted_iota(jnp.int32, (c, c), 1)
        cum = jnp.where((ti - tj) * sgn >= 0, 1.0, 0.0).astype(BF16)
        end_row = 0 if reverse else c - 1
        own = (lax.broadcasted_iota(jnp.int32, (c2, LANES), 0) // c
               == lax.broadcasted_iota(jnp.int32, (c2, LANES), 1) // RWKV_N)

        def stack(xv):
            return jnp.where(own, jnp.concatenate([xv, xv], axis=0), 0.0).astype(BF16)

        chunk_order = list(range(WKV_CHUNKS_PER_STEP))[::sgn]
        probs = [(j, p) for j in chunk_order for p in range(pps)]
        at = lambda ref, j, p: ref[j * c:(j + 1) * c, p * LANES:(p + 1) * LANES]

        lwv = [at(lw_ref, j, p) for j, p in probs]
        cs3 = [_dot(cum, jnp.concatenate(_split_bf16(x, 3), axis=1)) for x in lwv]
        cs = [x[:, :LANES] + x[:, LANES:2 * LANES] + x[:, 2 * LANES:] for x in cs3]
        g = [jnp.exp(x) for x in cs]
        ginv = [jnp.exp(-x) for x in cs]
        gprev = [jnp.exp(x - w) for x, w in zip(cs, lwv)]
        gend = [jnp.exp(x[end_row:end_row + 1] - x) for x in cs]
        gall = [jnp.exp(x[end_row:end_row + 1]) for x in cs]
        kkv = [at(kk_ref, j, p) for j, p in probs]
        bv = [at(b_ref, j, p) for j, p in probs]
        ktv = [at(kt_ref, j, p) for j, p in probs]
        n = len(probs)
        xa = [stack(-kkv[i] * gprev[i]) for i in range(n)]
        xr = [stack(at(r_ref, *probs[i]) * g[i]) for i in range(n)]
        xb = [stack(bv[i] * ginv[i]) for i in range(n)]
        xk = [stack(ktv[i] * ginv[i]) for i in range(n)]
        xb_end = [stack(bv[i] * gend[i]) for i in range(n)]
        xk_end = [stack(ktv[i] * gend[i]) for i in range(n)]
        xv = [stack(at(v_ref, *probs[i])) for i in range(n)]
        mm = [_dot_nt(jnp.concatenate([xa[i], xr[i]], axis=0), jnp.concatenate([xb[i], xk[i]], axis=0))
              for i in range(n)]
        m_ab = [jnp.where(strict, x[:c2, :c2], 0.0) for x in mm]
        m_ak = [jnp.where(strict, x[:c2, c2:], 0.0).astype(BF16) for x in mm]
        m_rb = [jnp.where(incl, x[c2:, :c2], 0.0).astype(BF16) for x in mm]
        m_rk = [jnp.where(incl, x[c2:, c2:], 0.0).astype(BF16) for x in mm]
        akv = [_dot(m_ak[i], xv[i]).astype(BF16) for i in range(n)]
        pj = [x.astype(BF16) for x in m_ab]
        tinv = [eye + x for x in m_ab]
        pj = [_dot(x, x).astype(BF16) for x in pj]
        for _ in range(int(math.log2(c)) - 2):
            both = [_dot(jnp.concatenate([pj[i], tinv[i].astype(BF16)], axis=0), pj[i]) for i in range(n)]
            pj = [x[:c2].astype(BF16) for x in both]
            tinv = [tinv[i] + both[i][c2:] for i in range(n)]
        tinv = [(tinv[i] + _dot(tinv[i].astype(BF16), pj[i])).astype(BF16) for i in range(n)]
        pq = [_dot(tinv[i], jnp.concatenate([xa[i], akv[i]], axis=1)).astype(BF16) for i in range(n)]
        ab = [_dot_tn(pq[i], xb_end[i]) for i in range(n)]
        a_c = [(eye * gall[i] + ab[i][:LANES]).astype(BF16) for i in range(n)]
        b_c = [ab[i][LANES:] + _dot_tn(xv[i], xk_end[i]) for i in range(n)]
        cd = [_dot(m_rb[i], pq[i]) for i in range(n)]
        c_c = [(xr[i].astype(F32) + cd[i][:, :LANES]).astype(BF16) for i in range(n)]
        d_c = [cd[i][:, LANES:] + _dot(m_rk[i], xv[i]) for i in range(n)]
        for p in range(pps):
            s = s_scr[p]
            for i, (j, pp) in enumerate(probs):
                if pp != p:
                    continue
                s_hi, s_lo = _split_bf16(s, 2)
                yy = _dot_nt(c_c[i], s_hi) + _dot_nt(c_c[i], s_lo) + d_c[i]
                y_ref[j * c:(j + 1) * c, p * LANES:(p + 1) * LANES] = yy[:c] + yy[c:]
                s = _dot(s_hi, a_c[i]) + _dot(s_lo, a_c[i]) + b_c[i]
            s_scr[p] = s

        @pl.when(last)
        def _():
            so_ref[0] = s_scr[...]

    tok = pl.BlockSpec((rows, pw), lambda lb, s: (block_of(s), lb))
    st = (1, pps, LANES, LANES)
    return pl.pallas_call(
        body, grid=(n_pairs // pps, n_steps),
        in_specs=[tok] * 6 + [pl.BlockSpec(st, lambda lb, s: (
            jnp.clip(seq_of(block_of(s))[1] - lay.n_ctx_seq, 0, lay.n_lat_seq - 1), lb, 0, 0))],
        out_specs=[tok, pl.BlockSpec(st, lambda lb, s: (seq_of(block_of(s))[1], lb, 0, 0))],
        out_shape=[jax.ShapeDtypeStruct((m, d), F32), jax.ShapeDtypeStruct((n_seq, n_pairs, LANES, LANES), F32)],
        scratch_shapes=[pltpu.VMEM((pps, LANES, LANES), F32)],
        compiler_params=_cp(2, arbitrary_last=True), name="wkv_scan_bwd" if reverse else "wkv_scan_fwd",
    )(r, kk, v, lw, b, kt, s_init)


def _wkv_post(y_fwd, y_bwd, bonus, g, ln_g, ln_b):
    m, d = bonus.shape
    tm = _tile(128, m)

    def body(yf_ref, yb_ref, bon_ref, g_ref, lg_ref, lb_ref, o_ref):
        yv = yf_ref[...] + yb_ref[...]
        mu = _head_sum(yv) * (1.0 / RWKV_N)
        yc = yv - mu
        var = _head_sum(yc * yc) * (1.0 / RWKV_N)
        o = yc * lax.rsqrt(var + GN_EPS) * lg_ref[...] + lb_ref[...] + bon_ref[...]
        o_ref[...] = (o * g_ref[...]).astype(o_ref.dtype)

    row = pl.BlockSpec((tm, d), lambda i: (i, 0))
    vec = pl.BlockSpec((1, d), lambda i: (0, 0))
    return pl.pallas_call(
        body, grid=(m // tm,),
        in_specs=[row, row, row, row, vec, vec],
        out_specs=row, out_shape=jax.ShapeDtypeStruct((m, d), BF16), compiler_params=_cp(1), name="wkv_post",
    )(y_fwd, y_bwd, bonus, g, ln_g.reshape(1, d), ln_b.reshape(1, d))


def _pad_cols(w, n):
    return jnp.pad(w, ((0, 0), (0, n - w.shape[1])))


def _pad_rows(w, n):
    return jnp.pad(w, ((0, n - w.shape[0]), (0, 0)))


def _rwkv_layer(x, hf, mod3, lay, s_fwd, s_bwd, mu, w_r, w_k, w_v, w_o, w0, w1, w2, a0, a1, a2, g1, g2,
                k_k, k_a, r_k, ln_g, ln_b):
    m, d = x.shape
    tm = lay.tile(1024)
    xr, xw, xk, xv, xa, xg = _shift_mix(hf, mu, lay)
    r = _mm_act(xr, w_r, None, None, F32, tm=tm, tn=512, name="rwkv_r")
    k = _mm_act(xk, w_k, None, None, F32, tm=tm, tn=512, name="rwkv_k")
    v = _mm_act(xv, w_v, None, None, F32, tm=tm, tn=512, name="rwkv_v")
    gh = _mm_act(xg, g1, None, jax.nn.sigmoid, BF16, tm=tm, tn=g1.shape[1], name="rwkv_g1")
    g = _mm_act(gh, g2, None, None, F32, tm=tm, tn=512, name="rwkv_g2")
    wl, al = [], []
    for dr in range(2):
        th = _mm_act(xw, _pad_cols(w1[dr], LANES), None, jnp.tanh, BF16, tm=tm, tn=LANES, name="rwkv_w1")
        wl.append(_mm_act(th, _pad_rows(w2[dr], LANES), w0[dr], None, F32, tm=tm, tn=512, name="rwkv_w2"))
        ah = _mm_act(xa, _pad_cols(a1[dr], LANES), None, None, BF16, tm=tm, tn=LANES, name="rwkv_a1")
        al.append(_mm_act(ah, _pad_rows(a2[dr], LANES), a0[dr], None, F32, tm=tm, tn=512, name="rwkv_a2"))
    kk, lw, b, kt, bonus = _wkv_prep(r, k, v, wl, al, k_k, k_a, r_k.reshape(-1))
    n_pairs = d // LANES
    ys, finals = [], []
    for dr, s0 in enumerate((s_fwd, s_bwd)):
        st = s0.astype(F32).reshape(-1, n_pairs, 2, RWKV_N, RWKV_N)
        s_init = jnp.zeros((st.shape[0], n_pairs, LANES, LANES), F32)
        s_init = s_init.at[..., :RWKV_N, :RWKV_N].set(st[:, :, 0]).at[..., RWKV_N:, RWKV_N:].set(st[:, :, 1])
        y, s_fin = _wkv_scan_dir(r, kk, v, lw[dr], b[dr], kt[dr], s_init, lay, reverse=dr == 1)
        sf = s_fin[:lay.n_ctx_seq]
        s_pair = jnp.stack([sf[..., :RWKV_N, :RWKV_N], sf[..., RWKV_N:, RWKV_N:]], axis=2)
        ys.append(y)
        finals.append(s_pair.reshape(lay.n_ctx_seq, 2 * n_pairs, RWKV_N, RWKV_N))
    og = _wkv_post(ys[0], ys[1], bonus, g, ln_g, ln_b)
    x = _mm_resid(og, w_o, None, x, mod3, 2, lay, tm=tm, tn=512, name="rwkv_out")
    return x, finals[0], finals[1]


def kernel(x_prompt, x_sample, cache_ckv_l0, cache_krope_l0, state_wkv_fwd_l3, state_wkv_bwd_l3, c, c_ctx, l0_w_mod, l0_b_mod, l0_norm_mix, l0_norm_ffn, l0_mla_w_down, l0_mla_q_norm, l0_mla_kv_norm, l0_mla_w_uq, l0_mla_w_ukv, l0_mla_w_o, l0_ffn_w_gate, l0_ffn_w_up, l0_ffn_w_down, l1_w_mod, l1_b_mod, l1_norm_mix, l1_norm_ffn, l1_conv_w_in, l1_conv_b_in, l1_conv_w_dw, l1_conv_b_dw, l1_conv_ln_g, l1_conv_ln_b, l1_conv_w_out, l1_conv_b_out, l1_moe_w_router, l1_moe_b_router, l1_moe_w_gate, l1_moe_w_up, l1_moe_w_down, l2_w_mod, l2_b_mod, l2_norm_mix, l2_norm_ffn, l2_sgu_w_in, l2_sgu_b_in, l2_sgu_ln_g, l2_sgu_ln_b, l2_sgu_w_s, l2_sgu_b_s, l2_sgu_w_out, l2_sgu_b_out, l2_ffn_w_gate, l2_ffn_w_up, l2_ffn_w_down, l3_w_mod, l3_b_mod, l3_norm_mix, l3_norm_ffn, l3_rwkv_mu, l3_rwkv_w_r, l3_rwkv_w_k, l3_rwkv_w_v, l3_rwkv_w_o, l3_rwkv_w0, l3_rwkv_w1, l3_rwkv_w2, l3_rwkv_a0, l3_rwkv_a1, l3_rwkv_a2, l3_rwkv_g1, l3_rwkv_g2, l3_rwkv_k_k, l3_rwkv_k_a, l3_rwkv_r_k, l3_rwkv_ln_g, l3_rwkv_ln_b, l3_moe_w_router, l3_moe_b_router, l3_moe_w_gate, l3_moe_w_up, l3_moe_w_down, norm_out):
    n_ctx_seq, ctx_len, d = x_prompt.shape
    n_lat_seq, lat_len, _ = x_sample.shape
    lay = _Lay(n_ctx_seq, ctx_len, n_lat_seq, lat_len)
    assert n_lat_seq + 1 <= SUBLANES
    x = jnp.concatenate([x_prompt.reshape(-1, d), x_sample.reshape(-1, d)], axis=0)
    cond8 = jnp.zeros((SUBLANES, d), F32).at[0].set(c_ctx).at[1:1 + n_lat_seq].set(c)

    mod3 = _ada_mod(cond8, l0_w_mod, l0_b_mod)
    h = _norm_mod(x, l0_norm_mix, mod3, (0, 1), lay, BF16)
    x, ckv_l0, krope_l0 = _mla_layer(x, h, mod3, lay, cache_ckv_l0, cache_krope_l0, l0_mla_w_down, l0_mla_q_norm,
                                     l0_mla_kv_norm, l0_mla_w_uq, l0_mla_w_ukv, l0_mla_w_o)
    h = _norm_mod(x, l0_norm_ffn, mod3, (3, 4), lay, BF16)
    x = _dense_ffn(x, h, l0_ffn_w_gate, l0_ffn_w_up, l0_ffn_w_down, mod3, lay)

    mod3 = _ada_mod(cond8, l1_w_mod, l1_b_mod)
    h = _norm_mod(x, l1_norm_mix, mod3, (0, 1), lay, BF16)
    x = _conformer_layer(x, h, mod3, lay, l1_conv_w_in, l1_conv_b_in, l1_conv_w_dw, l1_conv_b_dw, l1_conv_ln_g,
                         l1_conv_ln_b, l1_conv_w_out, l1_conv_b_out)
    x = _moe_ffn(x, l1_norm_ffn, mod3, lay, l1_moe_w_router, l1_moe_b_router, l1_moe_w_gate, l1_moe_w_up,
                 l1_moe_w_down)

    mod3 = _ada_mod(cond8, l2_w_mod, l2_b_mod)
    h = _norm_mod(x, l2_norm_mix, mod3, (0, 1), lay, BF16)
    x = _sgu_layer(x, h, mod3, lay, l2_sgu_w_in, l2_sgu_b_in, l2_sgu_ln_g, l2_sgu_ln_b, l2_sgu_w_s, l2_sgu_b_s,
                   l2_sgu_w_out, l2_sgu_b_out)
    h = _norm_mod(x, l2_norm_ffn, mod3, (3, 4), lay, BF16)
    x = _dense_ffn(x, h, l2_ffn_w_gate, l2_ffn_w_up, l2_ffn_w_down, mod3, lay)

    mod3 = _ada_mod(cond8, l3_w_mod, l3_b_mod)
    hf = _norm_mod(x, l3_norm_mix, mod3, (0, 1), lay, F32)
    x, wkv_fwd, wkv_bwd = _rwkv_layer(x, hf, mod3, lay, state_wkv_fwd_l3, state_wkv_bwd_l3, l3_rwkv_mu, l3_rwkv_w_r,
                                      l3_rwkv_w_k, l3_rwkv_w_v, l3_rwkv_w_o, l3_rwkv_w0, l3_rwkv_w1, l3_rwkv_w2,
                                      l3_rwkv_a0, l3_rwkv_a1, l3_rwkv_a2, l3_rwkv_g1, l3_rwkv_g2, l3_rwkv_k_k,
                                      l3_rwkv_k_a, l3_rwkv_r_k, l3_rwkv_ln_g, l3_rwkv_ln_b)
    x = _moe_ffn(x, l3_norm_ffn, mod3, lay, l3_moe_w_router, l3_moe_b_router, l3_moe_w_gate, l3_moe_w_up,
                 l3_moe_w_down)

    y = _norm_mod(x, norm_out, None, None, lay, F32, name="norm_out")
    y_prompt = y[:lay.n_ctx].reshape(n_ctx_seq, ctx_len, d)
    y_sample = y[lay.n_ctx:].reshape(n_lat_seq, lat_len, d)
    return (y_prompt, y_sample, ckv_l0, krope_l0, wkv_fwd, wkv_bwd)
```

```python
import functools
import math

import jax
import jax.numpy as jnp
import numpy as np
from jax import lax
from jax.experimental import pallas as pl
from jax.experimental.pallas import tpu as pltpu

F32 = jnp.float32
BF16 = jnp.bfloat16

NORM_EPS = 1e-6
LN_EPS = 1e-5
GN_EPS = 64e-5
QK_NOPE = 128
QK_ROPE = 64
V_DIM = 128
ROPE_THETA = 10000.0
GRID_W = 64
RWKV_N = 64
N_MOD = 6
TOP_K = 2

LANES = 128
SUBLANES = 8
VMEM_LIMIT_BYTES = 56 * 1024 * 1024

WKV_CHUNK = 64
WKV_PAIRS_PER_STEP = 4
WKV_CHUNKS_PER_STEP = 4
MOE_ROWS = 512


def _cp(n_axes, arbitrary_last=False):
    sem = ["parallel"] * n_axes
    if arbitrary_last:
        sem[-1] = "arbitrary"
    return pltpu.CompilerParams(dimension_semantics=tuple(sem), vmem_limit_bytes=VMEM_LIMIT_BYTES)


def _tile(pref, *sizes):
    t = pref
    while any(s % t for s in sizes):
        t //= 2
    assert t >= SUBLANES
    return t


class _Lay:
    def __init__(self, n_ctx_seq, ctx_len, n_lat_seq, lat_len):
        self.n_ctx_seq, self.ctx_len, self.n_lat_seq, self.lat_len = n_ctx_seq, ctx_len, n_lat_seq, lat_len
        self.n_ctx = n_ctx_seq * ctx_len
        self.n_tok = self.n_ctx + n_lat_seq * lat_len

    def tile(self, pref):
        return _tile(pref, self.n_ctx, self.lat_len)

    def group(self, start):
        return jnp.where(start < self.n_ctx, 0, 1 + (start - self.n_ctx) // self.lat_len)

    def seq_pos(self, start):
        is_ctx = start < self.n_ctx
        pos = jnp.where(is_ctx, start % self.ctx_len, (start - self.n_ctx) % self.lat_len)
        return pos, jnp.where(is_ctx, self.ctx_len, self.lat_len)


def _silu(x):
    return x * jax.nn.sigmoid(x)


def _gelu_tanh(x):
    return 0.5 * x * (1.0 + jnp.tanh(math.sqrt(2.0 / math.pi) * (x + 0.044715 * (x * x * x))))


def _softplus(x):
    return jnp.maximum(x, 0.0) + jnp.log(1.0 + jnp.exp(-jnp.abs(x)))


def _split_bf16(x, n):
    parts = []
    r = x
    for _ in range(n):
        h = r.astype(BF16)
        parts.append(h)
        r = r - h.astype(F32)
    return parts


def _dot(a, b):
    return jnp.dot(a, b, preferred_element_type=F32)


def _dot_nt(a, b):
    return lax.dot_general(a, b, (((1,), (1,)), ((), ())), preferred_element_type=F32)


def _dot_tn(a, b):
    return lax.dot_general(a, b, (((0,), (0,)), ((), ())), preferred_element_type=F32)


def _pair_ones():
    r = lax.broadcasted_iota(jnp.int32, (LANES, LANES), 0) // RWKV_N
    c = lax.broadcasted_iota(jnp.int32, (LANES, LANES), 1) // RWKV_N
    return jnp.where(r == c, 1.0, 0.0).astype(BF16)


def _head_sum(x):
    ones = _pair_ones()
    cols = []
    for t in range(x.shape[1] // LANES):
        xt = x[:, t * LANES:(t + 1) * LANES]
        cols.append(sum(_dot(p, ones) for p in _split_bf16(xt, 3)))
    return jnp.concatenate(cols, axis=1)


def _mm(a, ws, *, tm, tn, gn, epi, outs, vecs=(), wcols=None, n_outer=False, name="mm"):
    m, k = a.shape
    gm = m // tm
    nw, nv = len(ws), len(vecs)
    wcols = wcols or [lambda j: j] * nw
    if n_outer:
        grid = (gn, gm)
        wrap = lambda fn: (lambda g0, g1: fn(g1, g0))
    else:
        grid = (gm, gn)
        wrap = lambda fn: (lambda g0, g1: fn(g0, g1))
    in_specs = [pl.BlockSpec((tm, k), wrap(lambda i, j: (i, 0)))]
    for wc in wcols:
        in_specs.append(pl.BlockSpec((k, tn), wrap(lambda i, j, wc=wc: (0, wc(j)))))
    for _, bshape, fn in vecs:
        in_specs.append(pl.BlockSpec(bshape, wrap(fn)))
    out_shape = [jax.ShapeDtypeStruct(s, d) for s, d, _, _ in outs]
    out_specs = [pl.BlockSpec(b, wrap(fn)) for _, _, b, fn in outs]

    def body(*refs):
        a_ref = refs[0]
        w_refs = refs[1:1 + nw]
        v_refs = refs[1 + nw:1 + nw + nv]
        o_refs = refs[1 + nw + nv:]
        av = a_ref[...]
        if av.dtype != BF16:
            av = av.astype(BF16)
        accs = [_dot(av, w[...].astype(BF16)) for w in w_refs]
        for o, r in zip(o_refs, epi(accs, v_refs)):
            o[...] = r.reshape(o.shape).astype(o.dtype)

    res = pl.pallas_call(body, grid=grid, in_specs=in_specs, out_specs=out_specs, out_shape=out_shape,
                         compiler_params=_cp(2), name=name)(a, *ws, *[v[0] for v in vecs])
    return res


def _bias_vec(b, tn):
    return (b.reshape(1, -1), (1, tn), lambda i, j: (0, j))


def _mm_act(a, w, bias, act, out_dtype, *, tm, tn, name, n_outer=False):
    m, n = a.shape[0], w.shape[1]
    vecs = [] if bias is None else [_bias_vec(bias, tn)]

    def epi(accs, v):
        x = accs[0]
        if bias is not None:
            x = x + v[0][...]
        return [act(x) if act is not None else x]

    return _mm(a, [w], tm=tm, tn=tn, gn=pl.cdiv(n, tn), epi=epi, vecs=vecs, n_outer=n_outer, name=name,
               outs=[((m, n), out_dtype, (tm, tn), lambda i, j: (i, j))])[0]


def _mm_dual(a, w1, w2, b1, b2, fn, out_dtype, *, n, col2, tm, tn, name, n_outer=True):
    m = a.shape[0]
    gn = pl.cdiv(n, tn)
    vecs = []
    if b1 is not None:
        vecs = [(b1.reshape(1, -1), (1, tn), lambda i, j: (0, j)),
                (b2.reshape(1, -1), (1, tn), lambda i, j: (0, j + col2))]

    def epi(accs, v):
        x, y = accs
        if b1 is not None:
            x, y = x + v[0][...], y + v[1][...]
        return [fn(x, y)]

    return _mm(a, [w1, w2], tm=tm, tn=tn, gn=gn, epi=epi, vecs=vecs, n_outer=n_outer, name=name,
               wcols=[lambda j: j, lambda j: j + col2],
               outs=[((m, n), out_dtype, (tm, tn), lambda i, j: (i, j))])[0]


def _mm_resid(a, w, bias, resid, mod3, gate_row, lay, *, tm, tn, name, n_outer=False):
    m, n = resid.shape
    vecs = [(mod3, (1, N_MOD, tn), lambda i, j: (lay.group(i * tm), 0, j)),
            (resid, (tm, tn), lambda i, j: (i, j))]
    if bias is not None:
        vecs.append(_bias_vec(bias, tn))

    def epi(accs, v):
        x = accs[0]
        if bias is not None:
            x = x + v[2][...]
        return [v[1][...] + v[0][0, gate_row:gate_row + 1, :] * x]

    return _mm(a, [w], tm=tm, tn=tn, gn=n // tn, epi=epi, vecs=vecs, n_outer=n_outer, name=name,
               outs=[((m, n), F32, (tm, tn), lambda i, j: (i, j))])[0]


def _ada_mod(cond8, w_mod, b_mod):
    d, n = w_mod.shape
    tn = _tile(1536, n)

    def body(c_ref, w_ref, b_ref, o_ref):
        c = c_ref[...]
        o_ref[...] = _dot(_silu(c).astype(BF16), w_ref[...].astype(BF16)) + b_ref[...]

    m = pl.pallas_call(
        body, grid=(n // tn,),
        in_specs=[pl.BlockSpec((SUBLANES, d), lambda j: (0, 0)), pl.BlockSpec((d, tn), lambda j: (0, j)),
                  pl.BlockSpec((1, tn), lambda j: (0, j))],
        out_specs=pl.BlockSpec((SUBLANES, tn), lambda j: (0, j)),
        out_shape=jax.ShapeDtypeStruct((SUBLANES, n), F32), compiler_params=_cp(1), name="ada_mod",
    )(cond8, w_mod, b_mod.reshape(1, n))
    return m.reshape(SUBLANES, N_MOD, d)


def _norm_mod(x, gain, mod3, rows, lay, out_dtype, *, router=None, name="norm_mod"):
    m, d = x.shape
    tm = lay.tile(256)
    n_e = None if router is None else router[3]

    def body(*refs):
        x_ref, g_ref = refs[0], refs[1]
        pos = 2
        xf = x_ref[...]
        y = xf * lax.rsqrt(jnp.mean(xf * xf, axis=-1, keepdims=True) + NORM_EPS) * g_ref[...]
        if rows is not None:
            mod_ref = refs[pos]
            pos += 1
            y = y * (1.0 + mod_ref[0, rows[1]:rows[1] + 1, :]) + mod_ref[0, rows[0]:rows[0] + 1, :]
        if router is not None:
            whi_ref, wlo_ref, rb_ref = refs[pos:pos + 3]
            pos += 3
        o_ref = refs[pos]
        o_ref[...] = y.astype(o_ref.dtype)
        if router is not None:
            e_ref, p_ref = refs[pos + 1], refs[pos + 2]
            y_hi, y_lo = _split_bf16(y, 2)
            logits = _dot(y_hi, whi_ref[...]) + _dot(y_lo, whi_ref[...]) + _dot(y_hi, wlo_ref[...]) + rb_ref[...]
            lane = lax.broadcasted_iota(jnp.int32, logits.shape, 1)
            logits = jnp.where(lane < n_e, logits, -jnp.inf)
            m1 = jnp.max(logits, axis=-1, keepdims=True)
            i1 = jnp.min(jnp.where(logits == m1, lane, LANES), axis=-1, keepdims=True)
            rest = jnp.where(lane == i1, -jnp.inf, logits)
            m2 = jnp.max(rest, axis=-1, keepdims=True)
            i2 = jnp.min(jnp.where(rest == m2, lane, LANES), axis=-1, keepdims=True)
            e2 = jnp.exp(m2 - m1)
            p1 = 1.0 / (1.0 + e2)
            e_ref[...] = jnp.where(lane == 0, i1, jnp.where(lane == 1, i2, 0))
            p_ref[...] = jnp.where(lane == 0, p1, jnp.where(lane == 1, e2 * p1, 0.0))

    in_specs = [pl.BlockSpec((tm, d), lambda i: (i, 0)), pl.BlockSpec((1, d), lambda i: (0, 0))]
    args = [x, gain.reshape(1, d)]
    if rows is not None:
        in_specs.append(pl.BlockSpec((1, N_MOD, d), lambda i: (lay.group(i * tm), 0, 0)))
        args.append(mod3)
    out_shape = [jax.ShapeDtypeStruct((m, d), out_dtype)]
    out_specs = [pl.BlockSpec((tm, d), lambda i: (i, 0))]
    if router is not None:
        in_specs += [pl.BlockSpec((d, LANES), lambda i: (0, 0)), pl.BlockSpec((d, LANES), lambda i: (0, 0)),
                     pl.BlockSpec((1, LANES), lambda i: (0, 0))]
        args += list(router[:3])
        out_shape += [jax.ShapeDtypeStruct((m, LANES), jnp.int32), jax.ShapeDtypeStruct((m, LANES), F32)]
        out_specs += [pl.BlockSpec((tm, LANES), lambda i: (i, 0))] * 2
    res = pl.pallas_call(body, grid=(m // tm,), in_specs=in_specs, out_specs=out_specs, out_shape=out_shape,
                         compiler_params=_cp(1), name=name)(*args)
    return res if router is not None else res[0]


def _dense_ffn(x, h, w_gate, w_up, w_down, mod3, lay):
    d_ff = w_gate.shape[1]
    hid = _mm_dual(h, w_gate, w_up, None, None, lambda g, u: _silu(g) * u, BF16, n=d_ff, col2=0,
                   tm=lay.tile(1024), tn=512, name="ffn_up")
    return _mm_resid(hid, w_down, None, x, mod3, 5, lay, tm=lay.tile(512), tn=512, n_outer=True,
                     name="ffn_down")


def _moe_up(xs, w_gate, w_up, block_e, block_src, n_used, *, tm):
    n_slots, d = xs.shape
    n_e, _, d_ff = w_gate.shape
    tn = _tile(1024, d_ff)
    n_blocks = n_slots // tm

    def body(be_ref, bs_ref, nu_ref, x_ref, wg_ref, wu_ref, o_ref):
        @pl.when(pl.program_id(1) < nu_ref[0])
        def _():
            xv = x_ref[...]
            g = _dot(xv, wg_ref[...].astype(BF16))
            u = _dot(xv, wu_ref[...].astype(BF16))
            o_ref[...] = (_silu(g) * u).astype(o_ref.dtype)

    grid_spec = pltpu.PrefetchScalarGridSpec(
        num_scalar_prefetch=3, grid=(d_ff // tn, n_blocks),
        in_specs=[pl.BlockSpec((tm, d), lambda j, i, be, bs, nu: (bs[i], 0)),
                  pl.BlockSpec((None, d, tn), lambda j, i, be, bs, nu: (be[i], 0, j)),
                  pl.BlockSpec((None, d, tn), lambda j, i, be, bs, nu: (be[i], 0, j))],
        out_specs=pl.BlockSpec((tm, tn), lambda j, i, be, bs, nu: (bs[i], j)))
    return pl.pallas_call(body, grid_spec=grid_spec, out_shape=jax.ShapeDtypeStruct((n_slots, d_ff), BF16),
                          compiler_params=_cp(2, arbitrary_last=True), name="moe_up"
                          )(block_e, block_src, n_used, xs, w_gate, w_up)


def _moe_down(hid, w_down, block_e, block_src, n_used, *, tm):
    n_slots, d_ff = hid.shape
    d = w_down.shape[2]
    tn = _tile(512, d)
    n_blocks = n_slots // tm

    def body(be_ref, bs_ref, nu_ref, h_ref, w_ref, o_ref):
        @pl.when(pl.program_id(1) < nu_ref[0])
        def _():
            o_ref[...] = _dot(h_ref[...], w_ref[...].astype(BF16))

    grid_spec = pltpu.PrefetchScalarGridSpec(
        num_scalar_prefetch=3, grid=(d // tn, n_blocks),
        in_specs=[pl.BlockSpec((tm, d_ff), lambda j, i, be, bs, nu: (bs[i], 0)),
                  pl.BlockSpec((None, d_ff, tn), lambda j, i, be, bs, nu: (be[i], 0, j))],
        out_specs=pl.BlockSpec((tm, tn), lambda j, i, be, bs, nu: (bs[i], j)))
    return pl.pallas_call(body, grid_spec=grid_spec, out_shape=jax.ShapeDtypeStruct((n_slots, d), F32),
                          compiler_params=_cp(2, arbitrary_last=True), name="moe_down"
                          )(block_e, block_src, n_used, hid, w_down)


def _moe_combine(x, y0, y1, top_p, mod3, lay):
    m, d = x.shape
    tm = lay.tile(256)

    def body(x_ref, a_ref, b_ref, p_ref, mod_ref, o_ref):
        mix = p_ref[:, 0:1] * a_ref[...] + p_ref[:, 1:2] * b_ref[...]
        o_ref[...] = x_ref[...] + mod_ref[0, 5:6, :] * mix

    row = pl.BlockSpec((tm, d), lambda i: (i, 0))
    return pl.pallas_call(
        body, grid=(m // tm,),
        in_specs=[row, row, row, pl.BlockSpec((tm, LANES), lambda i: (i, 0)),
                  pl.BlockSpec((1, N_MOD, d), lambda i: (lay.group(i * tm), 0, 0))],
        out_specs=row, out_shape=jax.ShapeDtypeStruct((m, d), F32), compiler_params=_cp(1), name="moe_combine",
    )(x, y0, y1, top_p, mod3)


def _moe_ffn(x, norm_g, mod3, lay, w_router, b_router, w_gate, w_up, w_down):
    n_tok, d = x.shape
    n_e = w_router.shape[1]
    wr = jnp.zeros((d, LANES), F32).at[:, :n_e].set(w_router)
    wr_hi = wr.astype(BF16)
    wr_lo = (wr - wr_hi.astype(F32)).astype(BF16)
    rb = jnp.zeros((1, LANES), F32).at[0, :n_e].set(b_router.astype(F32))
    h, top_e, top_p = _norm_mod(x, norm_g, mod3, (3, 4), lay, BF16, router=(wr_hi, wr_lo, rb, n_e),
                                name="norm_router")
    tm = MOE_ROWS
    nk = n_tok * TOP_K
    flat_e = top_e[:, :TOP_K].reshape(nk)
    onehot = (flat_e[:, None] == jnp.arange(n_e, dtype=jnp.int32)[None, :]).astype(jnp.int32)
    rank = jnp.take_along_axis(jnp.cumsum(onehot, axis=0) - onehot, flat_e[:, None], axis=1)[:, 0]
    counts = jnp.sum(onehot, axis=0)
    padded = (counts + tm - 1) // tm * tm
    pad_end = jnp.cumsum(padded)
    dest = (pad_end - padded)[flat_e] + rank
    n_blocks = -(-nk // tm) + n_e
    n_slots = n_blocks * tm
    slot_tok = jnp.zeros((n_slots,), jnp.int32).at[dest].set(jnp.arange(nk, dtype=jnp.int32) // TOP_K)
    n_used = (pad_end[-1] // tm).astype(jnp.int32)
    blk = jnp.arange(n_blocks, dtype=jnp.int32)
    block_src = jnp.minimum(blk, n_used - 1)
    block_e = jnp.minimum(jnp.searchsorted(pad_end, block_src * tm, side="right"), n_e - 1).astype(jnp.int32)
    xs = jnp.take(h, slot_tok, axis=0, mode="clip")
    hid = _moe_up(xs, w_gate, w_up, block_e, block_src, n_used.reshape(1), tm=tm)
    ys = _moe_down(hid, w_down, block_e, block_src, n_used.reshape(1), tm=tm)
    pos = dest.reshape(n_tok, TOP_K)
    return _moe_combine(x, jnp.take(ys, pos[:, 0], axis=0, mode="clip"),
                        jnp.take(ys, pos[:, 1], axis=0, mode="clip"), top_p, mod3, lay)


def _mla_mid(dq, q_norm, kv_norm, cos64, sin64, q_lora, kv_lora, lay):
    m, n = dq.shape
    tm = lay.tile(512)
    o_kd, o_ko, o_ks = q_lora + kv_lora, q_lora + kv_lora + QK_ROPE, q_lora + kv_lora + 2 * QK_ROPE

    def body(x_ref, qg_ref, kg_ref, cos_ref, sin_ref, cq_ref, ckv_ref, kro_ref, krr_ref):
        cq = x_ref[:, 0:q_lora]
        cq_ref[...] = (cq * lax.rsqrt(jnp.mean(cq * cq, axis=-1, keepdims=True) + NORM_EPS)
                       * qg_ref[...]).astype(cq_ref.dtype)
        ckv = x_ref[:, q_lora:q_lora + kv_lora]
        ckv_ref[...] = ckv * lax.rsqrt(jnp.mean(ckv * ckv, axis=-1, keepdims=True) + NORM_EPS) * kg_ref[...]
        kro_ref[...] = x_ref[:, o_ko:o_ko + QK_ROPE]
        krr_ref[...] = (x_ref[:, o_kd:o_kd + QK_ROPE] * cos_ref[...]
                        + x_ref[:, o_ks:o_ks + QK_ROPE] * sin_ref[...]).astype(krr_ref.dtype)

    rope = pl.BlockSpec((tm, QK_ROPE), lambda i: (i, 0))
    return pl.pallas_call(
        body, grid=(m // tm,),
        in_specs=[pl.BlockSpec((tm, n), lambda i: (i, 0)), pl.BlockSpec((1, q_lora), lambda i: (0, 0)),
                  pl.BlockSpec((1, kv_lora), lambda i: (0, 0)), rope, rope],
        out_specs=[pl.BlockSpec((tm, q_lora), lambda i: (i, 0)), pl.BlockSpec((tm, kv_lora), lambda i: (i, 0)),
                   rope, rope],
        out_shape=[jax.ShapeDtypeStruct((m, q_lora), BF16), jax.ShapeDtypeStruct((m, kv_lora), F32),
                   jax.ShapeDtypeStruct((m, QK_ROPE), F32), jax.ShapeDtypeStruct((m, QK_ROPE), BF16)],
        compiler_params=_cp(1), name="mla_mid",
    )(dq, q_norm.reshape(1, -1), kv_norm.reshape(1, -1), cos64, sin64)


def _mla_q(cq, w_uq_ext, cos64, sin64, n_heads, tm):
    m = cq.shape[0]
    hw = QK_NOPE + 2 * QK_ROPE
    dqk = QK_NOPE + QK_ROPE
    scale = dqk ** -0.5

    def epi(accs, v):
        x = accs[0]
        rot = x[:, QK_NOPE:dqk] * v[0][...] + x[:, dqk:hw] * v[1][...]
        return [jnp.concatenate([x[:, :QK_NOPE], rot], axis=1) * scale]

    rope = lambda arr: (arr, (tm, QK_ROPE), lambda i, j: (i, 0))
    return _mm(cq, [w_uq_ext], tm=tm, tn=hw, gn=n_heads, epi=epi, vecs=[rope(cos64), rope(sin64)], name="mla_q",
               outs=[((n_heads, m, dqk), BF16, (1, tm, dqk), lambda i, j: (j, i, 0))])[0]


def _mla_kv(ckv, w_ukv, krope, n_heads, tm):
    m = ckv.shape[0]
    dqk = QK_NOPE + QK_ROPE

    def epi(accs, v):
        x = accs[0]
        return [jnp.concatenate([x[:, :QK_NOPE], v[0][...].astype(F32)], axis=1), x[:, QK_NOPE:]]

    return _mm(ckv, [w_ukv], tm=tm, tn=QK_NOPE + V_DIM, gn=n_heads, epi=epi, name="mla_kv",
               vecs=[(krope, (tm, QK_ROPE), lambda i, j: (i, 0))],
               outs=[((n_heads, m, dqk), BF16, (1, tm, dqk), lambda i, j: (j, i, 0)),
                     ((n_heads, m, V_DIM), BF16, (1, tm, V_DIM), lambda i, j: (j, i, 0))])


def _attn_context(q, k, v, lay):
    n_heads = q.shape[0]
    t = lay.ctx_len
    dqk = q.shape[2]

    def body(q_ref, k_ref, v_ref, o_ref):
        for h in range(n_heads):
            s = _dot_nt(q_ref[h], k_ref[h])
            p = jnp.exp(s - jnp.max(s, axis=-1, keepdims=True))
            l = jnp.sum(p, axis=-1, keepdims=True)
            o = _dot(p.astype(BF16), v_ref[h]) / l
            o_ref[:, h * V_DIM:(h + 1) * V_DIM] = o.astype(o_ref.dtype)

    return pl.pallas_call(
        body, grid=(lay.n_ctx_seq,),
        in_specs=[pl.BlockSpec((n_heads, t, dqk), lambda b: (0, b, 0)),
                  pl.BlockSpec((n_heads, t, dqk), lambda b: (0, b, 0)),
                  pl.BlockSpec((n_heads, t, V_DIM), lambda b: (0, b, 0))],
        out_specs=pl.BlockSpec((t, n_heads * V_DIM), lambda b: (b, 0)),
        out_shape=jax.ShapeDtypeStruct((lay.n_ctx, n_heads * V_DIM), BF16),
        compiler_params=_cp(1), name="attn_context",
    )(q, k, v)


def _attn_latent(q, k, v, kc, vc, lay, past_len):
    n_heads = q.shape[0]
    t = lay.lat_len
    dqk = q.shape[2]
    hp = 2 if n_heads % 2 == 0 else 1
    tq = _tile(256, t)
    lat0 = lay.n_ctx // t
    q0 = lay.n_ctx // tq
    kch = _tile(1024, t)
    n_ch = t // kch

    def body(q_ref, k_ref, v_ref, kc_ref, vc_ref, o_ref):
        def scores(h):
            qv = q_ref[h]
            return ([_dot_nt(qv, k_ref[h, j * kch:(j + 1) * kch, :]) for j in range(n_ch)]
                    + [_dot_nt(qv, kc_ref[h])])

        def row_max(s):
            return functools.reduce(jnp.maximum, [jnp.max(x, axis=-1, keepdims=True) for x in s])

        def probs(s, mx):
            p = [jnp.exp(x - mx) for x in s]
            l = functools.reduce(jnp.add, [jnp.sum(x, axis=-1, keepdims=True) for x in p])
            return [x.astype(BF16) for x in p], l

        def weighted(h, p):
            vs = [v_ref[h, j * kch:(j + 1) * kch, :] for j in range(n_ch)] + [vc_ref[h]]
            return functools.reduce(jnp.add, [_dot(x, vv) for x, vv in zip(p, vs)])

        s = [scores(h) for h in range(hp)]
        outs = []
        p_prev = None
        for h in range(hp):
            p, l = probs(s[h], row_max(s[h]))
            if p_prev is not None:
                outs.append(weighted(h - 1, p_prev[0]) / p_prev[1])
            p_prev = (p, l)
        outs.append(weighted(hp - 1, p_prev[0]) / p_prev[1])
        o_ref[...] = jnp.concatenate(outs, axis=1).astype(o_ref.dtype)

    return pl.pallas_call(
        body, grid=(lay.n_lat_seq, n_heads // hp, t // tq),
        in_specs=[pl.BlockSpec((hp, tq, dqk), lambda b, h, i: (h, q0 + b * (t // tq) + i, 0)),
                  pl.BlockSpec((hp, t, dqk), lambda b, h, i: (h, lat0 + b, 0)),
                  pl.BlockSpec((hp, t, V_DIM), lambda b, h, i: (h, lat0 + b, 0)),
                  pl.BlockSpec((hp, past_len, dqk), lambda b, h, i: (h, b, 0)),
                  pl.BlockSpec((hp, past_len, V_DIM), lambda b, h, i: (h, b, 0))],
        out_specs=pl.BlockSpec((tq, hp * V_DIM), lambda b, h, i: (b * (t // tq) + i, h)),
        out_shape=jax.ShapeDtypeStruct((lay.n_lat_seq * t, n_heads * V_DIM), BF16),
        compiler_params=_cp(3), name="attn_latent",
    )(q, k, v, kc, vc)


def _rope_tables(lay):
    n = lay.lat_len
    pairs = QK_ROPE // 4
    row_pos = (jnp.arange(n) // GRID_W).astype(F32)
    col_pos = (jnp.arange(n) % GRID_W).astype(F32)
    inv_freq = ROPE_THETA ** (-jnp.arange(pairs, dtype=F32) / pairs)
    ang = jnp.concatenate([row_pos[:, None] * inv_freq, col_pos[:, None] * inv_freq], axis=-1)
    cos, sin = jnp.cos(ang), jnp.sin(ang)
    cos64 = jnp.concatenate([cos, cos], axis=-1)
    sin64 = jnp.concatenate([-sin, sin], axis=-1)
    one = jnp.ones((lay.n_ctx, QK_ROPE), F32)
    cos_all = jnp.concatenate([one] + [cos64] * lay.n_lat_seq, axis=0)
    sin_all = jnp.concatenate([0.0 * one] + [sin64] * lay.n_lat_seq, axis=0)
    return cos_all, sin_all


def _mla_layer(x, h, mod3, lay, cache_ckv, cache_krope, w_down, q_norm, kv_norm, w_uq, w_ukv, w_o):
    d = x.shape[1]
    q_lora, kv_lora = q_norm.shape[0], kv_norm.shape[0]
    n_heads = w_o.shape[0] // V_DIM
    dqk = QK_NOPE + QK_ROPE
    perm_d = np.concatenate([np.arange(0, QK_ROPE, 2), np.arange(1, QK_ROPE, 2)])
    perm_s = np.concatenate([np.arange(1, QK_ROPE, 2), np.arange(0, QK_ROPE, 2)])
    kr0 = q_lora + kv_lora
    w_down_ext = jnp.concatenate(
        [w_down[:, :kr0], w_down[:, kr0 + perm_d], w_down[:, kr0:], w_down[:, kr0 + perm_s],
         jnp.zeros((d, QK_ROPE), F32)], axis=1)
    w_uq_h = w_uq.reshape(q_lora, n_heads, dqk)
    w_uq_ext = jnp.concatenate([w_uq_h[:, :, :QK_NOPE], w_uq_h[:, :, QK_NOPE + perm_d],
                                w_uq_h[:, :, QK_NOPE + perm_s]], axis=2).reshape(q_lora, -1)
    cos64, sin64 = _rope_tables(lay)
    tm = lay.tile(1024)
    dq = _mm_act(h, w_down_ext, None, None, F32, tm=tm, tn=512, name="mla_down")
    cq, ckv, krope_raw, krope_rot = _mla_mid(dq, q_norm, kv_norm, cos64, sin64, q_lora, kv_lora, lay)
    q = _mla_q(cq, w_uq_ext, cos64, sin64, n_heads, lay.tile(2048))
    k, v = _mla_kv(ckv, w_ukv, krope_rot, n_heads, lay.tile(2048))
    n_past = cache_ckv.shape[0] * cache_ckv.shape[1]
    kc, vc = _mla_kv(cache_ckv.reshape(n_past, kv_lora), w_ukv,
                     cache_krope.reshape(n_past, QK_ROPE)[:, perm_d].astype(BF16), n_heads,
                     _tile(512, cache_ckv.shape[1]))
    o_ctx = _attn_context(q, k, v, lay)
    o_lat = _attn_latent(q, k, v, kc, vc, lay, cache_ckv.shape[1])
    attn = jnp.concatenate([o_ctx, o_lat], axis=0)
    x = _mm_resid(attn, w_o, None, x, mod3, 2, lay, tm=lay.tile(512), tn=_tile(1024, d), n_outer=True,
                  name="mla_out")
    ckv_ctx = ckv[:lay.n_ctx].reshape(lay.n_ctx_seq, lay.ctx_len, kv_lora)
    krope_ctx = krope_raw[:lay.n_ctx].reshape(lay.n_ctx_seq, lay.ctx_len, QK_ROPE)
    return x, ckv_ctx, krope_ctx


def _conv_dw(u, w_dw, b_dw, ln_g, ln_b, lay):
    m, d = u.shape
    width = w_dw.shape[0]
    pad = width // 2
    t = lay.tile(256)
    halo = 2 * SUBLANES
    assert pad <= halo
    rb = _tile(64, t)

    sh_rows = t + halo + SUBLANES

    def body(prev_ref, cur_ref, nxt_ref, w_ref, bdw_ref, g_ref, b_ref, o_ref, ext_ref, dw_ref, sh_ref):
        pos, ln = lay.seq_pos(pl.program_id(0) * t)
        ext_ref[0:halo, :] = jnp.where(pos > 0, prev_ref[...], 0.0)
        ext_ref[halo:halo + t, :] = cur_ref[...]
        ext_ref[halo + t:halo + t + halo, :] = jnp.where(pos + t < ln, nxt_ref[...], 0.0)

        def chunk(c, carry):
            ls = pl.ds(pl.multiple_of(c * LANES, LANES), LANES)
            for s in range(SUBLANES):
                sh_ref[s] = ext_ref[pl.ds(s, sh_rows), ls]
            for r in range(t // rb):
                acc = jnp.zeros((rb, LANES), F32)
                for j in range(width):
                    off = halo - pad + j
                    acc = acc + (sh_ref[off % SUBLANES, pl.ds(off // SUBLANES * SUBLANES + r * rb, rb), :]
                                 * w_ref[pl.ds(j, 1), ls])
                dw_ref[pl.ds(r * rb, rb), ls] = acc + bdw_ref[:, ls]
            return carry

        lax.fori_loop(0, d // LANES, chunk, 0)
        xv = dw_ref[...]
        mu = jnp.mean(xv, axis=-1, keepdims=True)
        xc = xv - mu
        var = jnp.mean(xc * xc, axis=-1, keepdims=True)
        o_ref[...] = _silu(xc * lax.rsqrt(var + LN_EPS) * g_ref[...] + b_ref[...]).astype(o_ref.dtype)

    hb = t // halo
    vec = pl.BlockSpec((1, d), lambda i: (0, 0))
    return pl.pallas_call(
        body, grid=(m // t,),
        in_specs=[pl.BlockSpec((halo, d), lambda i: (jnp.maximum(i * hb - 1, 0), 0)),
                  pl.BlockSpec((t, d), lambda i: (i, 0)),
                  pl.BlockSpec((halo, d), lambda i: (jnp.minimum((i + 1) * hb, m // halo - 1), 0)),
                  pl.BlockSpec((width, d), lambda i: (0, 0)), vec, vec, vec],
        out_specs=pl.BlockSpec((t, d), lambda i: (i, 0)),
        out_shape=jax.ShapeDtypeStruct((m, d), BF16),
        scratch_shapes=[pltpu.VMEM((t + 2 * halo, d), F32), pltpu.VMEM((t, d), F32),
                        pltpu.VMEM((SUBLANES, sh_rows, LANES), F32)],
        compiler_params=_cp(1), name="conv_dw",
    )(u, u, u, w_dw, b_dw.reshape(1, d), ln_g.reshape(1, d), ln_b.reshape(1, d))


def _conformer_layer(x, h, mod3, lay, w_in, b_in, w_dw, b_dw, ln_g, ln_b, w_out, b_out):
    d = x.shape[1]
    tm = lay.tile(1024)
    u = _mm_dual(h, w_in, w_in, b_in, b_in, lambda a, b: a * jax.nn.sigmoid(b), F32, n=d, col2=d // 512,
                 tm=tm, tn=512, name="conv_in")
    cv = _conv_dw(u, w_dw, b_dw, ln_g, ln_b, lay)
    return _mm_resid(cv, w_out, b_out, x, mod3, 2, lay, tm=lay.tile(512), tn=_tile(1024, d), n_outer=True,
                     name="conv_out")


def _sgu_mix(z, ln_g, ln_b, w_s, b_s, lay):
    m, d2 = z.shape
    d = d2 // 2
    groups, chunk, _ = w_s.shape
    gd = d // groups
    tm = lay.tile(2 * chunk)
    bs_t = jnp.transpose(b_s)

    def body(u_ref, v_ref, g_ref, b_ref, ws_ref, bs_ref, o_ref):
        xv = v_ref[...]
        mu = jnp.mean(xv, axis=-1, keepdims=True)
        xc = xv - mu
        var = jnp.mean(xc * xc, axis=-1, keepdims=True)
        vn = (xc * lax.rsqrt(var + LN_EPS) * g_ref[...] + b_ref[...]).astype(BF16)
        for c in range(tm // chunk):
            rs = slice(c * chunk, (c + 1) * chunk)
            for g in range(groups):
                ls = slice(g * gd, (g + 1) * gd)
                mixed = _dot(ws_ref[g].astype(BF16), vn[rs, ls]) + bs_ref[:, g:g + 1]
                o_ref[rs, ls] = (u_ref[rs, ls] * mixed).astype(o_ref.dtype)

    vec = pl.BlockSpec((1, d), lambda i: (0, 0))
    return pl.pallas_call(
        body, grid=(m // tm,),
        in_specs=[pl.BlockSpec((tm, d), lambda i: (i, 0)), pl.BlockSpec((tm, d), lambda i: (i, 1)), vec, vec,
                  pl.BlockSpec((groups, chunk, chunk), lambda i: (0, 0, 0)),
                  pl.BlockSpec((chunk, groups), lambda i: (0, 0))],
        out_specs=pl.BlockSpec((tm, d), lambda i: (i, 0)),
        out_shape=jax.ShapeDtypeStruct((m, d), BF16), compiler_params=_cp(1), name="sgu_mix",
    )(z, z, ln_g.reshape(1, d), ln_b.reshape(1, d), w_s, bs_t)


def _sgu_layer(x, h, mod3, lay, w_in, b_in, ln_g, ln_b, w_s, b_s, w_out, b_out):
    tm = lay.tile(1024)
    d = x.shape[1]
    z = _mm_act(h, w_in, b_in, _gelu_tanh, F32, tm=lay.tile(512), tn=_tile(1024, d), name="sgu_in", n_outer=True)
    sm = _sgu_mix(z, ln_g, ln_b, w_s, b_s, lay)
    return _mm_resid(sm, w_out, b_out, x, mod3, 2, lay, tm=lay.tile(512), tn=_tile(1024, d), n_outer=True,
                     name="sgu_out")


def _shift_mix(h, mu, lay):
    m, d = h.shape
    t = lay.tile(256)
    halo = SUBLANES

    def body(prev_ref, cur_ref, nxt_ref, mu_ref, *rest):
        o_refs, ext_ref = rest[:6], rest[6]
        pos, ln = lay.seq_pos(pl.program_id(0) * t)
        ext_ref[0:halo, :] = jnp.where(pos > 0, prev_ref[...], 0.0)
        ext_ref[halo:halo + t, :] = cur_ref[...]
        ext_ref[halo + t:halo + t + halo, :] = jnp.where(pos + t < ln, nxt_ref[...], 0.0)
        xv = cur_ref[...]
        xx = 0.5 * (ext_ref[pl.ds(halo - 1, t), :] + ext_ref[pl.ds(halo + 1, t), :]) - xv
        for j in range(6):
            o_refs[j][...] = (xv + xx * mu_ref[j:j + 1, :]).astype(BF16)

    hb = t // halo
    row = pl.BlockSpec((t, d), lambda i: (i, 0))
    return pl.pallas_call(
        body, grid=(m // t,),
        in_specs=[pl.BlockSpec((halo, d), lambda i: (jnp.maximum(i * hb - 1, 0), 0)), row,
                  pl.BlockSpec((halo, d), lambda i: (jnp.minimum((i + 1) * hb, m // halo - 1), 0)),
                  pl.BlockSpec((6, d), lambda i: (0, 0))],
        out_specs=[row] * 6, out_shape=[jax.ShapeDtypeStruct((m, d), BF16)] * 6,
        scratch_shapes=[pltpu.VMEM((t + 2 * halo, d), F32)],
        compiler_params=_cp(1), name="rwkv_shift_mix",
    )(h, h, h, mu)


def _wkv_prep(r, k, v, wl, al, k_k, k_a, r_k):
    m, d = r.shape
    tm = _tile(128, m)

    def body(r_ref, k_ref, v_ref, wl0_ref, wl1_ref, al0_ref, al1_ref, kk_ref, ka_ref, rk_ref,
             okk_ref, olw0_ref, olw1_ref, ob0_ref, ob1_ref, okt0_ref, okt1_ref, obon_ref):
        rv, kv, vv = r_ref[...], k_ref[...], v_ref[...]
        kk = kv * kk_ref[...]
        kk = kk / jnp.maximum(jnp.sqrt(_head_sum(kk * kk)), 1e-12)
        okk_ref[...] = kk.astype(okk_ref.dtype)
        bonus = jnp.zeros_like(rv)
        for wl_ref, al_ref, olw_ref, ob_ref, okt_ref in ((wl0_ref, al0_ref, olw0_ref, ob0_ref, okt0_ref),
                                                          (wl1_ref, al1_ref, olw1_ref, ob1_ref, okt1_ref)):
            w_log = -_softplus(-wl_ref[...]) - 0.5
            olw_ref[...] = -jnp.exp(w_log)
            a = jax.nn.sigmoid(al_ref[...])
            kt = kv * (1.0 + (a - 1.0) * ka_ref[...])
            ob_ref[...] = (kk * a).astype(ob_ref.dtype)
            okt_ref[...] = kt.astype(okt_ref.dtype)
            bonus = bonus + _head_sum(rv * kt * rk_ref[...]) * vv
        obon_ref[...] = bonus

    row = pl.BlockSpec((tm, d), lambda i: (i, 0))
    vec = pl.BlockSpec((1, d), lambda i: (0, 0))
    f32 = jax.ShapeDtypeStruct((m, d), F32)
    bf16 = jax.ShapeDtypeStruct((m, d), BF16)
    kk, lw0, lw1, b0, b1, kt0, kt1, bonus = pl.pallas_call(
        body, grid=(m // tm,),
        in_specs=[row] * 7 + [vec] * 3,
        out_specs=[row] * 8,
        out_shape=[bf16, f32, f32, bf16, bf16, bf16, bf16, f32],
        compiler_params=_cp(1), name="wkv_prep",
    )(r, k, v, wl[0], wl[1], al[0], al[1], k_k.reshape(1, d), k_a.reshape(1, d), r_k.reshape(1, d))
    return kk, (lw0, lw1), (b0, b1), (kt0, kt1), bonus


def _wkv_scan_dir(r, kk, v, lw, b, kt, s_init, lay, reverse):
    m, d = r.shape
    c = WKV_CHUNK
    rows = c * WKV_CHUNKS_PER_STEP
    n_pairs = d // LANES
    pps = math.gcd(WKV_PAIRS_PER_STEP, n_pairs)
    pw = pps * LANES
    n_steps = m // rows
    ctx_steps = lay.n_ctx // rows
    sps_ctx, sps_lat = lay.ctx_len // rows, lay.lat_len // rows
    n_seq = lay.n_ctx_seq + lay.n_lat_seq
    c2 = 2 * c
    sgn = -1 if reverse else 1

    def block_of(s):
        return n_steps - 1 - s if reverse else s

    def seq_of(tb):
        is_ctx = tb < ctx_steps
        seq = jnp.where(is_ctx, tb // sps_ctx, lay.n_ctx_seq + (tb - ctx_steps) // sps_lat)
        pos = jnp.where(is_ctx, tb % sps_ctx, (tb - ctx_steps) % sps_lat)
        return is_ctx, seq, pos, jnp.where(is_ctx, sps_ctx, sps_lat)

    def body(r_ref, kk_ref, v_ref, lw_ref, b_ref, kt_ref, si_ref, y_ref, so_ref, s_scr):
        is_ctx, _, pos, sps = seq_of(block_of(pl.program_id(1)))
        first = pos == (sps - 1 if reverse else 0)
        last = pos == (0 if reverse else sps - 1)

        @pl.when(first)
        def _():
            s_scr[...] = jnp.where(is_ctx, 0.0, si_ref[0])

        ri = lax.broadcasted_iota(jnp.int32, (c2, c2), 0)
        ci = lax.broadcasted_iota(jnp.int32, (c2, c2), 1)
        same = (ri // c) == (ci // c)
        strict = same & ((ri - ci) * sgn > 0)
        incl = same & ((ri - ci) * sgn >= 0)
        eye = jnp.where(ri == ci, 1.0, 0.0)
        ti = lax.broadcasted_iota(jnp.int32, (c, c), 0)
        tj = lax.broadcasted_iota(jnp.int32, (c, c), 1)
        cum = jnp.where((ti - tj) * sgn >= 0, 1.0, 0.0).astype(BF16)
        end_row = 0 if reverse else c - 1
        own = (lax.broadcasted_iota(jnp.int32, (c2, LANES), 0) // c
               == lax.broadcasted_iota(jnp.int32, (c2, LANES), 1) // RWKV_N)

        def stack(xv):
            return jnp.where(own, jnp.concatenate([xv, xv], axis=0), 0.0).astype(BF16)

        chunk_order = list(range(WKV_CHUNKS_PER_STEP))[::sgn]
        probs = [(j, p) for j in chunk_order for p in range(pps)]
        at = lambda ref, j, p: ref[j * c:(j + 1) * c, p * LANES:(p + 1) * LANES].astype(F32)

        lwv = [at(lw_ref, j, p) for j, p in probs]
        cs3 = [_dot(cum, jnp.concatenate(_split_bf16(x, 3), axis=1)) for x in lwv]
        cs = [x[:, :LANES] + x[:, LANES:2 * LANES] + x[:, 2 * LANES:] for x in cs3]
        g = [jnp.exp(x) for x in cs]
        ginv = [jnp.exp(-x) for x in cs]
        gprev = [jnp.exp(x - w) for x, w in zip(cs, lwv)]
        gend = [jnp.exp(x[end_row:end_row + 1] - x) for x in cs]
        gall = [jnp.exp(x[end_row:end_row + 1]) for x in cs]
        kkv = [at(kk_ref, j, p) for j, p in probs]
        bv = [at(b_ref, j, p) for j, p in probs]
        ktv = [at(kt_ref, j, p) for j, p in probs]
        n = len(probs)
        xa = [stack(-kkv[i] * gprev[i]) for i in range(n)]
        xr = [stack(at(r_ref, *probs[i]) * g[i]) for i in range(n)]
        xb = [stack(bv[i] * ginv[i]) for i in range(n)]
        xk = [stack(ktv[i] * ginv[i]) for i in range(n)]
        xb_end = [stack(bv[i] * gend[i]) for i in range(n)]
        xk_end = [stack(ktv[i] * gend[i]) for i in range(n)]
        xv = [stack(at(v_ref, *probs[i])) for i in range(n)]
        mm = [_dot_nt(jnp.concatenate([xa[i], xr[i]], axis=0), jnp.concatenate([xb[i], xk[i]], axis=0))
              for i in range(n)]
        m_ab = [jnp.where(strict, x[:c2, :c2], 0.0) for x in mm]
        m_ak = [jnp.where(strict, x[:c2, c2:], 0.0).astype(BF16) for x in mm]
        m_rb = [jnp.where(incl, x[c2:, :c2], 0.0).astype(BF16) for x in mm]
        m_rk = [jnp.where(incl, x[c2:, c2:], 0.0).astype(BF16) for x in mm]
        akv = [_dot(m_ak[i], xv[i]).astype(BF16) for i in range(n)]
        pj = [x.astype(BF16) for x in m_ab]
        tinv = [eye + x for x in m_ab]
        pj = [_dot(x, x).astype(BF16) for x in pj]
        for _ in range(int(math.log2(c)) - 2):
            both = [_dot(jnp.concatenate([pj[i], tinv[i].astype(BF16)], axis=0), pj[i]) for i in range(n)]
            pj = [x[:c2].astype(BF16) for x in both]
            tinv = [tinv[i] + both[i][c2:] for i in range(n)]
        tinv = [(tinv[i] + _dot(tinv[i].astype(BF16), pj[i])).astype(BF16) for i in range(n)]
        pq = [_dot(tinv[i], jnp.concatenate([xa[i], akv[i]], axis=1)).astype(BF16) for i in range(n)]
        ab = [_dot_tn(pq[i], xb_end[i]) for i in range(n)]
        a_c = [(eye * gall[i] + ab[i][:LANES]).astype(BF16) for i in range(n)]
        b_c = [ab[i][LANES:] + _dot_tn(xv[i], xk_end[i]) for i in range(n)]
        cd = [_dot(m_rb[i], pq[i]) for i in range(n)]
        c_c = [(xr[i].astype(F32) + cd[i][:, :LANES]).astype(BF16) for i in range(n)]
        d_c = [cd[i][:, LANES:] + _dot(m_rk[i], xv[i]) for i in range(n)]
        for p in range(pps):
            s = s_scr[p]
            for i, (j, pp) in enumerate(probs):
                if pp != p:
                    continue
                s_hi, s_lo = _split_bf16(s, 2)
                yy = _dot_nt(c_c[i], s_hi) + _dot_nt(c_c[i], s_lo) + d_c[i]
                y_ref[j * c:(j + 1) * c, p * LANES:(p + 1) * LANES] = yy[:c] + yy[c:]
                s = _dot(s_hi, a_c[i]) + _dot(s_lo, a_c[i]) + b_c[i]
            s_scr[p] = s

        @pl.when(last)
        def _():
            so_ref[0] = s_scr[...]

    tok = pl.BlockSpec((rows, pw), lambda lb, s: (block_of(s), lb))
    st = (1, pps, LANES, LANES)
    return pl.pallas_call(
        body, grid=(n_pairs // pps, n_steps),
        in_specs=[tok] * 6 + [pl.BlockSpec(st, lambda lb, s: (
            jnp.clip(seq_of(block_of(s))[1] - lay.n_ctx_seq, 0, lay.n_lat_seq - 1), lb, 0, 0))],
        out_specs=[tok, pl.BlockSpec(st, lambda lb, s: (seq_of(block_of(s))[1], lb, 0, 0))],
        out_shape=[jax.ShapeDtypeStruct((m, d), F32), jax.ShapeDtypeStruct((n_seq, n_pairs, LANES, LANES), F32)],
        scratch_shapes=[pltpu.VMEM((pps, LANES, LANES), F32)],
        compiler_params=_cp(2, arbitrary_last=True), name="wkv_scan_bwd" if reverse else "wkv_scan_fwd",
    )(r, kk, v, lw, b, kt, s_init)


def _wkv_post(y_fwd, y_bwd, bonus, g, ln_g, ln_b):
    m, d = bonus.shape
    tm = _tile(128, m)

    def body(yf_ref, yb_ref, bon_ref, g_ref, lg_ref, lb_ref, o_ref):
        yv = yf_ref[...] + yb_ref[...]
        mu = _head_sum(yv) * (1.0 / RWKV_N)
        yc = yv - mu
        var = _head_sum(yc * yc) * (1.0 / RWKV_N)
        o = yc * lax.rsqrt(var + GN_EPS) * lg_ref[...] + lb_ref[...] + bon_ref[...]
        o_ref[...] = (o * g_ref[...]).astype(o_ref.dtype)

    row = pl.BlockSpec((tm, d), lambda i: (i, 0))
    vec = pl.BlockSpec((1, d), lambda i: (0, 0))
    return pl.pallas_call(
        body, grid=(m // tm,),
        in_specs=[row, row, row, row, vec, vec],
        out_specs=row, out_shape=jax.ShapeDtypeStruct((m, d), BF16), compiler_params=_cp(1), name="wkv_post",
    )(y_fwd, y_bwd, bonus, g, ln_g.reshape(1, d), ln_b.reshape(1, d))


def _pad_cols(w, n):
    return jnp.pad(w, ((0, 0), (0, n - w.shape[1])))


def _pad_rows(w, n):
    return jnp.pad(w, ((0, n - w.shape[0]), (0, 0)))


def _rwkv_layer(x, hf, mod3, lay, s_fwd, s_bwd, mu, w_r, w_k, w_v, w_o, w0, w1, w2, a0, a1, a2, g1, g2,
                k_k, k_a, r_k, ln_g, ln_b):
    m, d = x.shape
    tm = lay.tile(1024)
    xr, xw, xk, xv, xa, xg = _shift_mix(hf, mu, lay)
    sq = dict(tm=lay.tile(512), tn=_tile(1024, d), n_outer=True)
    r = _mm_act(xr, w_r, None, None, F32, name="rwkv_r", **sq)
    k = _mm_act(xk, w_k, None, None, F32, name="rwkv_k", **sq)
    v = _mm_act(xv, w_v, None, None, F32, name="rwkv_v", **sq)
    gh = _mm_act(xg, g1, None, jax.nn.sigmoid, BF16, tm=tm, tn=g1.shape[1], name="rwkv_g1")
    g = _mm_act(gh, g2, None, None, F32, name="rwkv_g2", **sq)
    both = lambda w: jnp.concatenate([_pad_cols(w[0], LANES), _pad_cols(w[1], LANES)], axis=1)
    th = _mm_act(xw, both(w1), None, jnp.tanh, BF16, tm=tm, tn=2 * LANES, name="rwkv_w1")
    ah = _mm_act(xa, both(a1), None, None, BF16, tm=tm, tn=2 * LANES, name="rwkv_a1")
    wl, al = [], []
    for dr in range(2):
        cols = slice(dr * LANES, (dr + 1) * LANES)
        wl.append(_mm_act(th[:, cols], _pad_rows(w2[dr], LANES), w0[dr], None, F32, name="rwkv_w2", **sq))
        al.append(_mm_act(ah[:, cols], _pad_rows(a2[dr], LANES), a0[dr], None, F32, name="rwkv_a2", **sq))
    kk, lw, b, kt, bonus = _wkv_prep(r, k, v, wl, al, k_k, k_a, r_k.reshape(-1))
    n_pairs = d // LANES
    ys, finals = [], []
    for dr, s0 in enumerate((s_fwd, s_bwd)):
        st = s0.astype(F32).reshape(-1, n_pairs, 2, RWKV_N, RWKV_N)
        s_init = jnp.zeros((st.shape[0], n_pairs, LANES, LANES), F32)
        s_init = s_init.at[..., :RWKV_N, :RWKV_N].set(st[:, :, 0]).at[..., RWKV_N:, RWKV_N:].set(st[:, :, 1])
        y, s_fin = _wkv_scan_dir(r, kk, v, lw[dr], b[dr], kt[dr], s_init, lay, reverse=dr == 1)
        sf = s_fin[:lay.n_ctx_seq]
        s_pair = jnp.stack([sf[..., :RWKV_N, :RWKV_N], sf[..., RWKV_N:, RWKV_N:]], axis=2)
        ys.append(y)
        finals.append(s_pair.reshape(lay.n_ctx_seq, 2 * n_pairs, RWKV_N, RWKV_N))
    og = _wkv_post(ys[0], ys[1], bonus, g, ln_g, ln_b)
    x = _mm_resid(og, w_o, None, x, mod3, 2, lay, name="rwkv_out", **sq)
    return x, finals[0], finals[1]


def kernel(x_prompt, x_sample, cache_ckv_l0, cache_krope_l0, state_wkv_fwd_l3, state_wkv_bwd_l3, c, c_ctx, l0_w_mod, l0_b_mod, l0_norm_mix, l0_norm_ffn, l0_mla_w_down, l0_mla_q_norm, l0_mla_kv_norm, l0_mla_w_uq, l0_mla_w_ukv, l0_mla_w_o, l0_ffn_w_gate, l0_ffn_w_up, l0_ffn_w_down, l1_w_mod, l1_b_mod, l1_norm_mix, l1_norm_ffn, l1_conv_w_in, l1_conv_b_in, l1_conv_w_dw, l1_conv_b_dw, l1_conv_ln_g, l1_conv_ln_b, l1_conv_w_out, l1_conv_b_out, l1_moe_w_router, l1_moe_b_router, l1_moe_w_gate, l1_moe_w_up, l1_moe_w_down, l2_w_mod, l2_b_mod, l2_norm_mix, l2_norm_ffn, l2_sgu_w_in, l2_sgu_b_in, l2_sgu_ln_g, l2_sgu_ln_b, l2_sgu_w_s, l2_sgu_b_s, l2_sgu_w_out, l2_sgu_b_out, l2_ffn_w_gate, l2_ffn_w_up, l2_ffn_w_down, l3_w_mod, l3_b_mod, l3_norm_mix, l3_norm_ffn, l3_rwkv_mu, l3_rwkv_w_r, l3_rwkv_w_k, l3_rwkv_w_v, l3_rwkv_w_o, l3_rwkv_w0, l3_rwkv_w1, l3_rwkv_w2, l3_rwkv_a0, l3_rwkv_a1, l3_rwkv_a2, l3_rwkv_g1, l3_rwkv_g2, l3_rwkv_k_k, l3_rwkv_k_a, l3_rwkv_r_k, l3_rwkv_ln_g, l3_rwkv_ln_b, l3_moe_w_router, l3_moe_b_router, l3_moe_w_gate, l3_moe_w_up, l3_moe_w_down, norm_out):
    n_ctx_seq, ctx_len, d = x_prompt.shape
    n_lat_seq, lat_len, _ = x_sample.shape
    lay = _Lay(n_ctx_seq, ctx_len, n_lat_seq, lat_len)
    assert n_lat_seq + 1 <= SUBLANES
    x = jnp.concatenate([x_prompt.reshape(-1, d), x_sample.reshape(-1, d)], axis=0)
    cond8 = jnp.zeros((SUBLANES, d), F32).at[0].set(c_ctx).at[1:1 + n_lat_seq].set(c)

    mod3 = _ada_mod(cond8, l0_w_mod, l0_b_mod)
    h = _norm_mod(x, l0_norm_mix, mod3, (0, 1), lay, BF16)
    x, ckv_l0, krope_l0 = _mla_layer(x, h, mod3, lay, cache_ckv_l0, cache_krope_l0, l0_mla_w_down, l0_mla_q_norm,
                                     l0_mla_kv_norm, l0_mla_w_uq, l0_mla_w_ukv, l0_mla_w_o)
    h = _norm_mod(x, l0_norm_ffn, mod3, (3, 4), lay, BF16)
    x = _dense_ffn(x, h, l0_ffn_w_gate, l0_ffn_w_up, l0_ffn_w_down, mod3, lay)

    mod3 = _ada_mod(cond8, l1_w_mod, l1_b_mod)
    h = _norm_mod(x, l1_norm_mix, mod3, (0, 1), lay, BF16)
    x = _conformer_layer(x, h, mod3, lay, l1_conv_w_in, l1_conv_b_in, l1_conv_w_dw, l1_conv_b_dw, l1_conv_ln_g,
                         l1_conv_ln_b, l1_conv_w_out, l1_conv_b_out)
    x = _moe_ffn(x, l1_norm_ffn, mod3, lay, l1_moe_w_router, l1_moe_b_router, l1_moe_w_gate, l1_moe_w_up,
                 l1_moe_w_down)

    mod3 = _ada_mod(cond8, l2_w_mod, l2_b_mod)
    h = _norm_mod(x, l2_norm_mix, mod3, (0, 1), lay, BF16)
    x = _sgu_layer(x, h, mod3, lay, l2_sgu_w_in, l2_sgu_b_in, l2_sgu_ln_g, l2_sgu_ln_b, l2_sgu_w_s, l2_sgu_b_s,
                   l2_sgu_w_out, l2_sgu_b_out)
    h = _norm_mod(x, l2_norm_ffn, mod3, (3, 4), lay, BF16)
    x = _dense_ffn(x, h, l2_ffn_w_gate, l2_ffn_w_up, l2_ffn_w_down, mod3, lay)

    mod3 = _ada_mod(cond8, l3_w_mod, l3_b_mod)
    hf = _norm_mod(x, l3_norm_mix, mod3, (0, 1), lay, F32)
    x, wkv_fwd, wkv_bwd = _rwkv_layer(x, hf, mod3, lay, state_wkv_fwd_l3, state_wkv_bwd_l3, l3_rwkv_mu, l3_rwkv_w_r,
                                      l3_rwkv_w_k, l3_rwkv_w_v, l3_rwkv_w_o, l3_rwkv_w0, l3_rwkv_w1, l3_rwkv_w2,
                                      l3_rwkv_a0, l3_rwkv_a1, l3_rwkv_a2, l3_rwkv_g1, l3_rwkv_g2, l3_rwkv_k_k,
                                      l3_rwkv_k_a, l3_rwkv_r_k, l3_rwkv_ln_g, l3_rwkv_ln_b)
    x = _moe_ffn(x, l3_norm_ffn, mod3, lay, l3_moe_w_router, l3_moe_b_router, l3_moe_w_gate, l3_moe_w_up,
                 l3_moe_w_down)

    y = _norm_mod(x, norm_out, None, None, lay, F32, name="norm_out")
    y_prompt = y[:lay.n_ctx].reshape(n_ctx_seq, ctx_len, d)
    y_sample = y[lay.n_ctx:].reshape(n_lat_seq, lat_len, d)
    return (y_prompt, y_sample, ckv_l0, krope_l0, wkv_fwd, wkv_bwd)
```

```python
import functools
import math

import jax
import jax.numpy as jnp
import numpy as np
from jax import lax
from jax.experimental import pallas as pl
from jax.experimental.pallas import tpu as pltpu

F32 = jnp.float32
BF16 = jnp.bfloat16

NORM_EPS = 1e-6
LN_EPS = 1e-5
GN_EPS = 64e-5
QK_NOPE = 128
QK_ROPE = 64
V_DIM = 128
ROPE_THETA = 10000.0
GRID_W = 64
RWKV_N = 64
N_MOD = 6
TOP_K = 2

LANES = 128
SUBLANES = 8
VMEM_LIMIT_BYTES = 56 * 1024 * 1024

WKV_CHUNK = 64
WKV_PAIRS_PER_STEP = 4
WKV_CHUNKS_PER_STEP = 4
MOE_ROWS = 512


def _cp(n_axes, arbitrary_last=False):
    sem = ["parallel"] * n_axes
    if arbitrary_last:
        sem[-1] = "arbitrary"
    return pltpu.CompilerParams(dimension_semantics=tuple(sem), vmem_limit_bytes=VMEM_LIMIT_BYTES)


def _tile(pref, *sizes):
    t = pref
    while any(s % t for s in sizes):
        t //= 2
    assert t >= SUBLANES
    return t


class _Lay:
    def __init__(self, n_ctx_seq, ctx_len, n_lat_seq, lat_len):
        self.n_ctx_seq, self.ctx_len, self.n_lat_seq, self.lat_len = n_ctx_seq, ctx_len, n_lat_seq, lat_len
        self.n_ctx = n_ctx_seq * ctx_len
        self.n_tok = self.n_ctx + n_lat_seq * lat_len

    def tile(self, pref):
        return _tile(pref, self.n_ctx, self.lat_len)

    def group(self, start):
        return jnp.where(start < self.n_ctx, 0, 1 + (start - self.n_ctx) // self.lat_len)

    def seq_pos(self, start):
        is_ctx = start < self.n_ctx
        pos = jnp.where(is_ctx, start % self.ctx_len, (start - self.n_ctx) % self.lat_len)
        return pos, jnp.where(is_ctx, self.ctx_len, self.lat_len)


def _silu(x):
    return x * jax.nn.sigmoid(x)


def _gelu_tanh(x):
    return 0.5 * x * (1.0 + jnp.tanh(math.sqrt(2.0 / math.pi) * (x + 0.044715 * (x * x * x))))


def _softplus(x):
    return jnp.maximum(x, 0.0) + jnp.log(1.0 + jnp.exp(-jnp.abs(x)))


def _split_bf16(x, n):
    parts = []
    r = x
    for _ in range(n):
        h = r.astype(BF16)
        parts.append(h)
        r = r - h.astype(F32)
    return parts


def _dot(a, b):
    return jnp.dot(a, b, preferred_element_type=F32)


def _dot_nt(a, b):
    return lax.dot_general(a, b, (((1,), (1,)), ((), ())), preferred_element_type=F32)


def _dot_tn(a, b):
    return lax.dot_general(a, b, (((0,), (0,)), ((), ())), preferred_element_type=F32)


def _pair_ones():
    r = lax.broadcasted_iota(jnp.int32, (LANES, LANES), 0) // RWKV_N
    c = lax.broadcasted_iota(jnp.int32, (LANES, LANES), 1) // RWKV_N
    return jnp.where(r == c, 1.0, 0.0).astype(BF16)


def _head_sum(x):
    ones = _pair_ones()
    cols = []
    for t in range(x.shape[1] // LANES):
        xt = x[:, t * LANES:(t + 1) * LANES]
        cols.append(sum(_dot(p, ones) for p in _split_bf16(xt, 3)))
    return jnp.concatenate(cols, axis=1)


def _mm(a, ws, *, tm, tn, gn, epi, outs, vecs=(), wcols=None, n_outer=False, name="mm"):
    m, k = a.shape
    gm = m // tm
    nw, nv = len(ws), len(vecs)
    wcols = wcols or [lambda j: j] * nw
    if n_outer:
        grid = (gn, gm)
        wrap = lambda fn: (lambda g0, g1: fn(g1, g0))
    else:
        grid = (gm, gn)
        wrap = lambda fn: (lambda g0, g1: fn(g0, g1))
    in_specs = [pl.BlockSpec((tm, k), wrap(lambda i, j: (i, 0)))]
    for wc in wcols:
        in_specs.append(pl.BlockSpec((k, tn), wrap(lambda i, j, wc=wc: (0, wc(j)))))
    for _, bshape, fn in vecs:
        in_specs.append(pl.BlockSpec(bshape, wrap(fn)))
    out_shape = [jax.ShapeDtypeStruct(s, d) for s, d, _, _ in outs]
    out_specs = [pl.BlockSpec(b, wrap(fn)) for _, _, b, fn in outs]

    def body(*refs):
        a_ref = refs[0]
        w_refs = refs[1:1 + nw]
        v_refs = refs[1 + nw:1 + nw + nv]
        o_refs = refs[1 + nw + nv:]
        av = a_ref[...]
        if av.dtype != BF16:
            av = av.astype(BF16)
        accs = [_dot(av, w[...].astype(BF16)) for w in w_refs]
        for o, r in zip(o_refs, epi(accs, v_refs)):
            o[...] = r.reshape(o.shape).astype(o.dtype)

    res = pl.pallas_call(body, grid=grid, in_specs=in_specs, out_specs=out_specs, out_shape=out_shape,
                         compiler_params=_cp(2), name=name)(a, *ws, *[v[0] for v in vecs])
    return res


def _bias_vec(b, tn):
    return (b.reshape(1, -1), (1, tn), lambda i, j: (0, j))


def _mm_act(a, w, bias, act, out_dtype, *, tm, tn, name, n_outer=False):
    m, n = a.shape[0], w.shape[1]
    vecs = [] if bias is None else [_bias_vec(bias, tn)]

    def epi(accs, v):
        x = accs[0]
        if bias is not None:
            x = x + v[0][...]
        return [act(x) if act is not None else x]

    return _mm(a, [w], tm=tm, tn=tn, gn=pl.cdiv(n, tn), epi=epi, vecs=vecs, n_outer=n_outer, name=name,
               outs=[((m, n), out_dtype, (tm, tn), lambda i, j: (i, j))])[0]


def _mm_dual(a, w1, w2, b1, b2, fn, out_dtype, *, n, col2, tm, tn, name, n_outer=True):
    m = a.shape[0]
    gn = pl.cdiv(n, tn)
    vecs = []
    if b1 is not None:
        vecs = [(b1.reshape(1, -1), (1, tn), lambda i, j: (0, j)),
                (b2.reshape(1, -1), (1, tn), lambda i, j: (0, j + col2))]

    def epi(accs, v):
        x, y = accs
        if b1 is not None:
            x, y = x + v[0][...], y + v[1][...]
        return [fn(x, y)]

    return _mm(a, [w1, w2], tm=tm, tn=tn, gn=gn, epi=epi, vecs=vecs, n_outer=n_outer, name=name,
               wcols=[lambda j: j, lambda j: j + col2],
               outs=[((m, n), out_dtype, (tm, tn), lambda i, j: (i, j))])[0]


def _mm_resid(a, w, bias, resid, mod3, gate_row, lay, *, tm, tn, name, n_outer=False):
    m, n = resid.shape
    vecs = [(mod3, (1, N_MOD, tn), lambda i, j: (lay.group(i * tm), 0, j)),
            (resid, (tm, tn), lambda i, j: (i, j))]
    if bias is not None:
        vecs.append(_bias_vec(bias, tn))

    def epi(accs, v):
        x = accs[0]
        if bias is not None:
            x = x + v[2][...]
        return [v[1][...] + v[0][0, gate_row:gate_row + 1, :] * x]

    return _mm(a, [w], tm=tm, tn=tn, gn=n // tn, epi=epi, vecs=vecs, n_outer=n_outer, name=name,
               outs=[((m, n), F32, (tm, tn), lambda i, j: (i, j))])[0]


def _ada_mod(cond8, w_mod, b_mod):
    d, n = w_mod.shape
    tn = _tile(1536, n)

    def body(c_ref, w_ref, b_ref, o_ref):
        c = c_ref[...]
        o_ref[...] = _dot(_silu(c).astype(BF16), w_ref[...].astype(BF16)) + b_ref[...]

    m = pl.pallas_call(
        body, grid=(n // tn,),
        in_specs=[pl.BlockSpec((SUBLANES, d), lambda j: (0, 0)), pl.BlockSpec((d, tn), lambda j: (0, j)),
                  pl.BlockSpec((1, tn), lambda j: (0, j))],
        out_specs=pl.BlockSpec((SUBLANES, tn), lambda j: (0, j)),
        out_shape=jax.ShapeDtypeStruct((SUBLANES, n), F32), compiler_params=_cp(1), name="ada_mod",
    )(cond8, w_mod, b_mod.reshape(1, n))
    return m.reshape(SUBLANES, N_MOD, d)


def _norm_rows(pre_fn, pre_ops, gain, mod3, rows, lay, out_dtype, *, m, d, emit_x=False, router=None,
               name="norm_mod"):
    tm = lay.tile(256)
    n_e = None if router is None else router[3]
    n_pre = len(pre_ops)

    def body(*refs):
        g_ref = refs[n_pre]
        pos = n_pre + 1
        xf = pre_fn(*refs[:n_pre])
        y = xf * lax.rsqrt(jnp.mean(xf * xf, axis=-1, keepdims=True) + NORM_EPS) * g_ref[...]
        if rows is not None:
            mod_ref = refs[pos]
            pos += 1
            y = y * (1.0 + mod_ref[0, rows[1]:rows[1] + 1, :]) + mod_ref[0, rows[0]:rows[0] + 1, :]
        if router is not None:
            whi_ref, wlo_ref, rb_ref = refs[pos:pos + 3]
            pos += 3
        if emit_x:
            refs[pos][...] = xf
            pos += 1
        o_ref = refs[pos]
        o_ref[...] = y.astype(o_ref.dtype)
        if router is not None:
            e_ref, p_ref = refs[pos + 1], refs[pos + 2]
            y_hi, y_lo = _split_bf16(y, 2)
            logits = _dot(y_hi, whi_ref[...]) + _dot(y_lo, whi_ref[...]) + _dot(y_hi, wlo_ref[...]) + rb_ref[...]
            lane = lax.broadcasted_iota(jnp.int32, logits.shape, 1)
            logits = jnp.where(lane < n_e, logits, -jnp.inf)
            m1 = jnp.max(logits, axis=-1, keepdims=True)
            i1 = jnp.min(jnp.where(logits == m1, lane, LANES), axis=-1, keepdims=True)
            rest = jnp.where(lane == i1, -jnp.inf, logits)
            m2 = jnp.max(rest, axis=-1, keepdims=True)
            i2 = jnp.min(jnp.where(rest == m2, lane, LANES), axis=-1, keepdims=True)
            e2 = jnp.exp(m2 - m1)
            p1 = 1.0 / (1.0 + e2)
            e_ref[...] = jnp.where(lane == 0, i1, jnp.where(lane == 1, i2, 0))
            p_ref[...] = jnp.where(lane == 0, p1, jnp.where(lane == 1, e2 * p1, 0.0))

    in_specs = [pl.BlockSpec(bshape, fn) for _, bshape, fn in pre_ops] + [pl.BlockSpec((1, d), lambda i: (0, 0))]
    args = [op[0] for op in pre_ops] + [gain.reshape(1, d)]
    if rows is not None:
        in_specs.append(pl.BlockSpec((1, N_MOD, d), lambda i: (lay.group(i * tm), 0, 0)))
        args.append(mod3)
    row = pl.BlockSpec((tm, d), lambda i: (i, 0))
    out_shape = ([jax.ShapeDtypeStruct((m, d), F32)] if emit_x else []) + [jax.ShapeDtypeStruct((m, d), out_dtype)]
    out_specs = [row] * len(out_shape)
    if router is not None:
        in_specs += [pl.BlockSpec((d, LANES), lambda i: (0, 0)), pl.BlockSpec((d, LANES), lambda i: (0, 0)),
                     pl.BlockSpec((1, LANES), lambda i: (0, 0))]
        args += list(router[:3])
        out_shape += [jax.ShapeDtypeStruct((m, LANES), jnp.int32), jax.ShapeDtypeStruct((m, LANES), F32)]
        out_specs += [pl.BlockSpec((tm, LANES), lambda i: (i, 0))] * 2
    res = pl.pallas_call(body, grid=(m // tm,), in_specs=in_specs, out_specs=out_specs, out_shape=out_shape,
                         compiler_params=_cp(1), name=name)(*args)
    return res if len(res) > 1 else res[0]


def _norm_mod(x, gain, mod3, rows, lay, out_dtype, *, name="norm_mod"):
    m, d = x.shape
    tm = lay.tile(256)
    return _norm_rows(lambda x_ref: x_ref[...], [(x, (tm, d), lambda i: (i, 0))], gain, mod3, rows, lay, out_dtype,
                      m=m, d=d, name=name)


def _out_norm(a, w, bias, resid, mod3, lay, gain, out_dtype, *, router=None, name):
    m, d = resid.shape
    k = a.shape[1]
    tm = lay.tile(256)
    has_bias = bias is not None

    def pre(a_ref, w_ref, mod_ref, r_ref, *b_ref):
        y = _dot(a_ref[...], w_ref[...])
        if has_bias:
            y = y + b_ref[0][...]
        return r_ref[...] + mod_ref[0, 2:3, :] * y

    ops = [(a, (tm, k), lambda i: (i, 0)), (w.astype(BF16), (k, d), lambda i: (0, 0)),
           (mod3, (1, N_MOD, d), lambda i: (lay.group(i * tm), 0, 0)), (resid, (tm, d), lambda i: (i, 0))]
    if has_bias:
        ops.append((bias.reshape(1, d), (1, d), lambda i: (0, 0)))
    return _norm_rows(pre, ops, gain, mod3, (3, 4), lay, out_dtype, m=m, d=d, emit_x=True, router=router, name=name)


def _dense_ffn(x, h, w_gate, w_up, w_down, mod3, lay):
    d_ff = w_gate.shape[1]
    hid = _mm_dual(h, w_gate, w_up, None, None, lambda g, u: _silu(g) * u, BF16, n=d_ff, col2=0,
                   tm=lay.tile(1024), tn=512, name="ffn_up")
    return _mm_resid(hid, w_down, None, x, mod3, 5, lay, tm=lay.tile(512), tn=512, n_outer=True,
                     name="ffn_down")


def _moe_up(xs, w_gate, w_up, block_e, block_src, n_used, *, tm):
    n_slots, d = xs.shape
    n_e, _, d_ff = w_gate.shape
    tn = _tile(1024, d_ff)
    n_blocks = n_slots // tm

    def body(be_ref, bs_ref, nu_ref, x_ref, wg_ref, wu_ref, o_ref):
        @pl.when(pl.program_id(1) < nu_ref[0])
        def _():
            xv = x_ref[...]
            g = _dot(xv, wg_ref[...].astype(BF16))
            u = _dot(xv, wu_ref[...].astype(BF16))
            o_ref[...] = (_silu(g) * u).astype(o_ref.dtype)

    grid_spec = pltpu.PrefetchScalarGridSpec(
        num_scalar_prefetch=3, grid=(d_ff // tn, n_blocks),
        in_specs=[pl.BlockSpec((tm, d), lambda j, i, be, bs, nu: (bs[i], 0)),
                  pl.BlockSpec((None, d, tn), lambda j, i, be, bs, nu: (be[i], 0, j)),
                  pl.BlockSpec((None, d, tn), lambda j, i, be, bs, nu: (be[i], 0, j))],
        out_specs=pl.BlockSpec((tm, tn), lambda j, i, be, bs, nu: (bs[i], j)))
    return pl.pallas_call(body, grid_spec=grid_spec, out_shape=jax.ShapeDtypeStruct((n_slots, d_ff), BF16),
                          compiler_params=_cp(2, arbitrary_last=True), name="moe_up"
                          )(block_e, block_src, n_used, xs, w_gate, w_up)


def _moe_down(hid, w_down, block_e, block_src, n_used, *, tm):
    n_slots, d_ff = hid.shape
    d = w_down.shape[2]
    tn = _tile(512, d)
    n_blocks = n_slots // tm

    def body(be_ref, bs_ref, nu_ref, h_ref, w_ref, o_ref):
        @pl.when(pl.program_id(1) < nu_ref[0])
        def _():
            o_ref[...] = _dot(h_ref[...], w_ref[...].astype(BF16)).astype(o_ref.dtype)

    grid_spec = pltpu.PrefetchScalarGridSpec(
        num_scalar_prefetch=3, grid=(d // tn, n_blocks),
        in_specs=[pl.BlockSpec((tm, d_ff), lambda j, i, be, bs, nu: (bs[i], 0)),
                  pl.BlockSpec((None, d_ff, tn), lambda j, i, be, bs, nu: (be[i], 0, j))],
        out_specs=pl.BlockSpec((tm, tn), lambda j, i, be, bs, nu: (bs[i], j)))
    return pl.pallas_call(body, grid_spec=grid_spec, out_shape=jax.ShapeDtypeStruct((n_slots, d), BF16),
                          compiler_params=_cp(2, arbitrary_last=True), name="moe_down"
                          )(block_e, block_src, n_used, hid, w_down)


def _moe_combine(x, y0, y1, top_p, mod3, lay, nxt):
    m, d = x.shape
    tm = lay.tile(256)
    gain, next_mod3, rows, out_dtype, emit_x = nxt

    def pre(x_ref, a_ref, b_ref, p_ref, mod_ref):
        mix = p_ref[:, 0:1] * a_ref[...].astype(F32) + p_ref[:, 1:2] * b_ref[...].astype(F32)
        return x_ref[...] + mod_ref[0, 5:6, :] * mix

    row = lambda arr: (arr, (tm, d), lambda i: (i, 0))
    ops = [row(x), row(y0), row(y1), (top_p, (tm, LANES), lambda i: (i, 0)),
           (mod3, (1, N_MOD, d), lambda i: (lay.group(i * tm), 0, 0))]
    return _norm_rows(pre, ops, gain, next_mod3, rows, lay, out_dtype, m=m, d=d, emit_x=emit_x,
                      name="moe_combine")


def _router_ops(w_router, b_router):
    d, n_e = w_router.shape
    wr = jnp.zeros((d, LANES), F32).at[:, :n_e].set(w_router)
    wr_hi = wr.astype(BF16)
    wr_lo = (wr - wr_hi.astype(F32)).astype(BF16)
    rb = jnp.zeros((1, LANES), F32).at[0, :n_e].set(b_router.astype(F32))
    return wr_hi, wr_lo, rb, n_e


def _moe_ffn(x, h, top_e, top_p, mod3, lay, n_e, w_gate, w_up, w_down, nxt):
    n_tok, d = x.shape
    tm = MOE_ROWS
    nk = n_tok * TOP_K
    flat_e = top_e[:, :TOP_K].reshape(nk)
    onehot = (flat_e[:, None] == jnp.arange(n_e, dtype=jnp.int32)[None, :]).astype(jnp.int32)
    rank = jnp.take_along_axis(jnp.cumsum(onehot, axis=0) - onehot, flat_e[:, None], axis=1)[:, 0]
    counts = jnp.sum(onehot, axis=0)
    padded = (counts + tm - 1) // tm * tm
    pad_end = jnp.cumsum(padded)
    dest = (pad_end - padded)[flat_e] + rank
    n_blocks = -(-nk // tm) + n_e
    n_slots = n_blocks * tm
    slot_tok = jnp.zeros((n_slots,), jnp.int32).at[dest].set(jnp.arange(nk, dtype=jnp.int32) // TOP_K)
    n_used = (pad_end[-1] // tm).astype(jnp.int32)
    blk = jnp.arange(n_blocks, dtype=jnp.int32)
    block_src = jnp.minimum(blk, n_used - 1)
    block_e = jnp.minimum(jnp.searchsorted(pad_end, block_src * tm, side="right"), n_e - 1).astype(jnp.int32)
    xs = jnp.take(h, slot_tok, axis=0, mode="clip")
    hid = _moe_up(xs, w_gate, w_up, block_e, block_src, n_used.reshape(1), tm=tm)
    ys = _moe_down(hid, w_down, block_e, block_src, n_used.reshape(1), tm=tm)
    pos = dest.reshape(n_tok, TOP_K)
    return _moe_combine(x, jnp.take(ys, pos[:, 0], axis=0, mode="clip"),
                        jnp.take(ys, pos[:, 1], axis=0, mode="clip"), top_p, mod3, lay, nxt)


def _mla_mid(dq, q_norm, kv_norm, cos64, sin64, q_lora, kv_lora, lay):
    m, n = dq.shape
    tm = lay.tile(512)
    o_kd, o_ko, o_ks = q_lora + kv_lora, q_lora + kv_lora + QK_ROPE, q_lora + kv_lora + 2 * QK_ROPE

    def body(x_ref, qg_ref, kg_ref, cos_ref, sin_ref, cq_ref, ckv_ref, kro_ref, krr_ref):
        cq = x_ref[:, 0:q_lora]
        cq_ref[...] = (cq * lax.rsqrt(jnp.mean(cq * cq, axis=-1, keepdims=True) + NORM_EPS)
                       * qg_ref[...]).astype(cq_ref.dtype)
        ckv = x_ref[:, q_lora:q_lora + kv_lora]
        ckv_ref[...] = ckv * lax.rsqrt(jnp.mean(ckv * ckv, axis=-1, keepdims=True) + NORM_EPS) * kg_ref[...]
        kro_ref[...] = x_ref[:, o_ko:o_ko + QK_ROPE]
        krr_ref[...] = (x_ref[:, o_kd:o_kd + QK_ROPE] * cos_ref[...]
                        + x_ref[:, o_ks:o_ks + QK_ROPE] * sin_ref[...]).astype(krr_ref.dtype)

    rope = pl.BlockSpec((tm, QK_ROPE), lambda i: (i, 0))
    return pl.pallas_call(
        body, grid=(m // tm,),
        in_specs=[pl.BlockSpec((tm, n), lambda i: (i, 0)), pl.BlockSpec((1, q_lora), lambda i: (0, 0)),
                  pl.BlockSpec((1, kv_lora), lambda i: (0, 0)), rope, rope],
        out_specs=[pl.BlockSpec((tm, q_lora), lambda i: (i, 0)), pl.BlockSpec((tm, kv_lora), lambda i: (i, 0)),
                   rope, rope],
        out_shape=[jax.ShapeDtypeStruct((m, q_lora), BF16), jax.ShapeDtypeStruct((m, kv_lora), F32),
                   jax.ShapeDtypeStruct((m, QK_ROPE), F32), jax.ShapeDtypeStruct((m, QK_ROPE), BF16)],
        compiler_params=_cp(1), name="mla_mid",
    )(dq, q_norm.reshape(1, -1), kv_norm.reshape(1, -1), cos64, sin64)


def _mla_q(cq, w_uq_ext, cos64, sin64, n_heads, tm):
    m = cq.shape[0]
    hw = QK_NOPE + 2 * QK_ROPE
    dqk = QK_NOPE + QK_ROPE
    scale = dqk ** -0.5

    def epi(accs, v):
        x = accs[0]
        rot = x[:, QK_NOPE:dqk] * v[0][...] + x[:, dqk:hw] * v[1][...]
        return [jnp.concatenate([x[:, :QK_NOPE], rot], axis=1) * scale]

    rope = lambda arr: (arr, (tm, QK_ROPE), lambda i, j: (i, 0))
    return _mm(cq, [w_uq_ext], tm=tm, tn=hw, gn=n_heads, epi=epi, vecs=[rope(cos64), rope(sin64)], name="mla_q",
               outs=[((n_heads, m, dqk), BF16, (1, tm, dqk), lambda i, j: (j, i, 0))])[0]


def _mla_kv(ckv, w_ukv, krope, n_heads, tm):
    m = ckv.shape[0]
    dqk = QK_NOPE + QK_ROPE

    def epi(accs, v):
        x = accs[0]
        return [jnp.concatenate([x[:, :QK_NOPE], v[0][...].astype(F32)], axis=1), x[:, QK_NOPE:]]

    return _mm(ckv, [w_ukv], tm=tm, tn=QK_NOPE + V_DIM, gn=n_heads, epi=epi, name="mla_kv",
               vecs=[(krope, (tm, QK_ROPE), lambda i, j: (i, 0))],
               outs=[((n_heads, m, dqk), BF16, (1, tm, dqk), lambda i, j: (j, i, 0)),
                     ((n_heads, m, V_DIM), BF16, (1, tm, V_DIM), lambda i, j: (j, i, 0))])


def _attn_context(q, k, v, lay):
    n_heads = q.shape[0]
    t = lay.ctx_len
    dqk = q.shape[2]

    def body(q_ref, k_ref, v_ref, o_ref):
        for h in range(n_heads):
            s = _dot_nt(q_ref[h], k_ref[h])
            p = jnp.exp(s - jnp.max(s, axis=-1, keepdims=True))
            l = jnp.sum(p, axis=-1, keepdims=True)
            o = _dot(p.astype(BF16), v_ref[h]) / l
            o_ref[:, h * V_DIM:(h + 1) * V_DIM] = o.astype(o_ref.dtype)

    return pl.pallas_call(
        body, grid=(lay.n_ctx_seq,),
        in_specs=[pl.BlockSpec((n_heads, t, dqk), lambda b: (0, b, 0)),
                  pl.BlockSpec((n_heads, t, dqk), lambda b: (0, b, 0)),
                  pl.BlockSpec((n_heads, t, V_DIM), lambda b: (0, b, 0))],
        out_specs=pl.BlockSpec((t, n_heads * V_DIM), lambda b: (b, 0)),
        out_shape=jax.ShapeDtypeStruct((lay.n_ctx, n_heads * V_DIM), BF16),
        compiler_params=_cp(1), name="attn_context",
    )(q, k, v)


def _attn_latent(q, k, v, kc, vc, lay, past_len):
    n_heads = q.shape[0]
    t = lay.lat_len
    dqk = q.shape[2]
    hp = 2 if n_heads % 2 == 0 else 1
    tq = _tile(256, t)
    lat0 = lay.n_ctx // t
    q0 = lay.n_ctx // tq
    kch = _tile(1024, t)
    n_ch = t // kch

    def body(q_ref, k_ref, v_ref, kc_ref, vc_ref, o_ref):
        def scores(h):
            qv = q_ref[h]
            return ([_dot_nt(qv, k_ref[h, j * kch:(j + 1) * kch, :]) for j in range(n_ch)]
                    + [_dot_nt(qv, kc_ref[h])])

        def row_max(s):
            return functools.reduce(jnp.maximum, [jnp.max(x, axis=-1, keepdims=True) for x in s])

        def probs(s, mx):
            p = [jnp.exp(x - mx) for x in s]
            l = functools.reduce(jnp.add, [jnp.sum(x, axis=-1, keepdims=True) for x in p])
            return [x.astype(BF16) for x in p], l

        def weighted(h, p):
            vs = [v_ref[h, j * kch:(j + 1) * kch, :] for j in range(n_ch)] + [vc_ref[h]]
            return functools.reduce(jnp.add, [_dot(x, vv) for x, vv in zip(p, vs)])

        s = [scores(h) for h in range(hp)]
        outs = []
        p_prev = None
        for h in range(hp):
            p, l = probs(s[h], row_max(s[h]))
            if p_prev is not None:
                outs.append(weighted(h - 1, p_prev[0]) / p_prev[1])
            p_prev = (p, l)
        outs.append(weighted(hp - 1, p_prev[0]) / p_prev[1])
        o_ref[...] = jnp.concatenate(outs, axis=1).astype(o_ref.dtype)

    return pl.pallas_call(
        body, grid=(lay.n_lat_seq, n_heads // hp, t // tq),
        in_specs=[pl.BlockSpec((hp, tq, dqk), lambda b, h, i: (h, q0 + b * (t // tq) + i, 0)),
                  pl.BlockSpec((hp, t, dqk), lambda b, h, i: (h, lat0 + b, 0)),
                  pl.BlockSpec((hp, t, V_DIM), lambda b, h, i: (h, lat0 + b, 0)),
                  pl.BlockSpec((hp, past_len, dqk), lambda b, h, i: (h, b, 0)),
                  pl.BlockSpec((hp, past_len, V_DIM), lambda b, h, i: (h, b, 0))],
        out_specs=pl.BlockSpec((tq, hp * V_DIM), lambda b, h, i: (b * (t // tq) + i, h)),
        out_shape=jax.ShapeDtypeStruct((lay.n_lat_seq * t, n_heads * V_DIM), BF16),
        compiler_params=_cp(3), name="attn_latent",
    )(q, k, v, kc, vc)


def _rope_tables(lay):
    n = lay.lat_len
    pairs = QK_ROPE // 4
    row_pos = (jnp.arange(n) // GRID_W).astype(F32)
    col_pos = (jnp.arange(n) % GRID_W).astype(F32)
    inv_freq = ROPE_THETA ** (-jnp.arange(pairs, dtype=F32) / pairs)
    ang = jnp.concatenate([row_pos[:, None] * inv_freq, col_pos[:, None] * inv_freq], axis=-1)
    cos, sin = jnp.cos(ang), jnp.sin(ang)
    cos64 = jnp.concatenate([cos, cos], axis=-1)
    sin64 = jnp.concatenate([-sin, sin], axis=-1)
    one = jnp.ones((lay.n_ctx, QK_ROPE), F32)
    cos_all = jnp.concatenate([one] + [cos64] * lay.n_lat_seq, axis=0)
    sin_all = jnp.concatenate([0.0 * one] + [sin64] * lay.n_lat_seq, axis=0)
    return cos_all, sin_all


def _mla_layer(x, h, mod3, lay, norm_ffn, cache_ckv, cache_krope, w_down, q_norm, kv_norm, w_uq, w_ukv, w_o):
    d = x.shape[1]
    q_lora, kv_lora = q_norm.shape[0], kv_norm.shape[0]
    n_heads = w_o.shape[0] // V_DIM
    dqk = QK_NOPE + QK_ROPE
    perm_d = np.concatenate([np.arange(0, QK_ROPE, 2), np.arange(1, QK_ROPE, 2)])
    perm_s = np.concatenate([np.arange(1, QK_ROPE, 2), np.arange(0, QK_ROPE, 2)])
    kr0 = q_lora + kv_lora
    w_down_ext = jnp.concatenate(
        [w_down[:, :kr0], w_down[:, kr0 + perm_d], w_down[:, kr0:], w_down[:, kr0 + perm_s],
         jnp.zeros((d, QK_ROPE), F32)], axis=1)
    w_uq_h = w_uq.reshape(q_lora, n_heads, dqk)
    w_uq_ext = jnp.concatenate([w_uq_h[:, :, :QK_NOPE], w_uq_h[:, :, QK_NOPE + perm_d],
                                w_uq_h[:, :, QK_NOPE + perm_s]], axis=2).reshape(q_lora, -1)
    cos64, sin64 = _rope_tables(lay)
    tm = lay.tile(1024)
    dq = _mm_act(h, w_down_ext, None, None, F32, tm=tm, tn=512, name="mla_down")
    cq, ckv, krope_raw, krope_rot = _mla_mid(dq, q_norm, kv_norm, cos64, sin64, q_lora, kv_lora, lay)
    q = _mla_q(cq, w_uq_ext, cos64, sin64, n_heads, lay.tile(2048))
    k, v = _mla_kv(ckv, w_ukv, krope_rot, n_heads, lay.tile(2048))
    n_past = cache_ckv.shape[0] * cache_ckv.shape[1]
    kc, vc = _mla_kv(cache_ckv.reshape(n_past, kv_lora), w_ukv,
                     cache_krope.reshape(n_past, QK_ROPE)[:, perm_d].astype(BF16), n_heads,
                     _tile(512, cache_ckv.shape[1]))
    o_ctx = _attn_context(q, k, v, lay)
    o_lat = _attn_latent(q, k, v, kc, vc, lay, cache_ckv.shape[1])
    attn = jnp.concatenate([o_ctx, o_lat], axis=0)
    x, h_ffn = _out_norm(attn, w_o, None, x, mod3, lay, norm_ffn, BF16, name="mla_out")
    ckv_ctx = ckv[:lay.n_ctx].reshape(lay.n_ctx_seq, lay.ctx_len, kv_lora)
    krope_ctx = krope_raw[:lay.n_ctx].reshape(lay.n_ctx_seq, lay.ctx_len, QK_ROPE)
    return x, h_ffn, ckv_ctx, krope_ctx


def _conv_dw(u, w_dw, b_dw, ln_g, ln_b, lay):
    m, d = u.shape
    width = w_dw.shape[0]
    pad = width // 2
    t = lay.tile(256)
    halo = 2 * SUBLANES
    assert pad <= halo
    rb = _tile(64, t)

    sh_rows = t + halo + SUBLANES

    def body(prev_ref, cur_ref, nxt_ref, w_ref, bdw_ref, g_ref, b_ref, o_ref, ext_ref, dw_ref, sh_ref):
        pos, ln = lay.seq_pos(pl.program_id(0) * t)
        ext_ref[0:halo, :] = jnp.where(pos > 0, prev_ref[...], 0.0)
        ext_ref[halo:halo + t, :] = cur_ref[...]
        ext_ref[halo + t:halo + t + halo, :] = jnp.where(pos + t < ln, nxt_ref[...], 0.0)

        def chunk(c, carry):
            ls = pl.ds(pl.multiple_of(c * LANES, LANES), LANES)
            for s in range(SUBLANES):
                sh_ref[s] = ext_ref[pl.ds(s, sh_rows), ls]
            for r in range(t // rb):
                acc = jnp.zeros((rb, LANES), F32)
                for j in range(width):
                    off = halo - pad + j
                    acc = acc + (sh_ref[off % SUBLANES, pl.ds(off // SUBLANES * SUBLANES + r * rb, rb), :]
                                 * w_ref[pl.ds(j, 1), ls])
                dw_ref[pl.ds(r * rb, rb), ls] = acc + bdw_ref[:, ls]
            return carry

        lax.fori_loop(0, d // LANES, chunk, 0)
        xv = dw_ref[...]
        mu = jnp.mean(xv, axis=-1, keepdims=True)
        xc = xv - mu
        var = jnp.mean(xc * xc, axis=-1, keepdims=True)
        o_ref[...] = _silu(xc * lax.rsqrt(var + LN_EPS) * g_ref[...] + b_ref[...]).astype(o_ref.dtype)

    hb = t // halo
    vec = pl.BlockSpec((1, d), lambda i: (0, 0))
    return pl.pallas_call(
        body, grid=(m // t,),
        in_specs=[pl.BlockSpec((halo, d), lambda i: (jnp.maximum(i * hb - 1, 0), 0)),
                  pl.BlockSpec((t, d), lambda i: (i, 0)),
                  pl.BlockSpec((halo, d), lambda i: (jnp.minimum((i + 1) * hb, m // halo - 1), 0)),
                  pl.BlockSpec((width, d), lambda i: (0, 0)), vec, vec, vec],
        out_specs=pl.BlockSpec((t, d), lambda i: (i, 0)),
        out_shape=jax.ShapeDtypeStruct((m, d), BF16),
        scratch_shapes=[pltpu.VMEM((t + 2 * halo, d), F32), pltpu.VMEM((t, d), F32),
                        pltpu.VMEM((SUBLANES, sh_rows, LANES), F32)],
        compiler_params=_cp(1), name="conv_dw",
    )(u, u, u, w_dw, b_dw.reshape(1, d), ln_g.reshape(1, d), ln_b.reshape(1, d))


def _conformer_layer(x, h, mod3, lay, norm_ffn, router, w_in, b_in, w_dw, b_dw, ln_g, ln_b, w_out, b_out):
    d = x.shape[1]
    tm = lay.tile(1024)
    u = _mm_dual(h, w_in, w_in, b_in, b_in, lambda a, b: a * jax.nn.sigmoid(b), F32, n=d, col2=d // 512,
                 tm=tm, tn=512, name="conv_in")
    cv = _conv_dw(u, w_dw, b_dw, ln_g, ln_b, lay)
    return _out_norm(cv, w_out, b_out, x, mod3, lay, norm_ffn, BF16, router=router, name="conv_out")


def _sgu_mix(z, ln_g, ln_b, w_s, b_s, lay):
    m, d2 = z.shape
    d = d2 // 2
    groups, chunk, _ = w_s.shape
    gd = d // groups
    tm = lay.tile(2 * chunk)
    bs_t = jnp.transpose(b_s)

    def body(u_ref, v_ref, g_ref, b_ref, ws_ref, bs_ref, o_ref):
        xv = v_ref[...]
        mu = jnp.mean(xv, axis=-1, keepdims=True)
        xc = xv - mu
        var = jnp.mean(xc * xc, axis=-1, keepdims=True)
        vn = (xc * lax.rsqrt(var + LN_EPS) * g_ref[...] + b_ref[...]).astype(BF16)
        for c in range(tm // chunk):
            rs = slice(c * chunk, (c + 1) * chunk)
            for g in range(groups):
                ls = slice(g * gd, (g + 1) * gd)
                mixed = _dot(ws_ref[g].astype(BF16), vn[rs, ls]) + bs_ref[:, g:g + 1]
                o_ref[rs, ls] = (u_ref[rs, ls] * mixed).astype(o_ref.dtype)

    vec = pl.BlockSpec((1, d), lambda i: (0, 0))
    return pl.pallas_call(
        body, grid=(m // tm,),
        in_specs=[pl.BlockSpec((tm, d), lambda i: (i, 0)), pl.BlockSpec((tm, d), lambda i: (i, 1)), vec, vec,
                  pl.BlockSpec((groups, chunk, chunk), lambda i: (0, 0, 0)),
                  pl.BlockSpec((chunk, groups), lambda i: (0, 0))],
        out_specs=pl.BlockSpec((tm, d), lambda i: (i, 0)),
        out_shape=jax.ShapeDtypeStruct((m, d), BF16), compiler_params=_cp(1), name="sgu_mix",
    )(z, z, ln_g.reshape(1, d), ln_b.reshape(1, d), w_s, bs_t)


def _sgu_layer(x, h, mod3, lay, norm_ffn, w_in, b_in, ln_g, ln_b, w_s, b_s, w_out, b_out):
    tm = lay.tile(1024)
    d = x.shape[1]
    z = _mm_act(h, w_in, b_in, _gelu_tanh, F32, tm=lay.tile(512), tn=_tile(1024, d), name="sgu_in", n_outer=True)
    sm = _sgu_mix(z, ln_g, ln_b, w_s, b_s, lay)
    return _out_norm(sm, w_out, b_out, x, mod3, lay, norm_ffn, BF16, name="sgu_out")


def _shift_mix(h, mu, lay):
    m, d = h.shape
    t = lay.tile(256)
    halo = SUBLANES

    def body(prev_ref, cur_ref, nxt_ref, mu_ref, *rest):
        o_refs, ext_ref = rest[:6], rest[6]
        pos, ln = lay.seq_pos(pl.program_id(0) * t)
        ext_ref[0:halo, :] = jnp.where(pos > 0, prev_ref[...], 0.0)
        ext_ref[halo:halo + t, :] = cur_ref[...]
        ext_ref[halo + t:halo + t + halo, :] = jnp.where(pos + t < ln, nxt_ref[...], 0.0)
        xv = cur_ref[...]
        xx = 0.5 * (ext_ref[pl.ds(halo - 1, t), :] + ext_ref[pl.ds(halo + 1, t), :]) - xv
        for j in range(6):
            o_refs[j][...] = (xv + xx * mu_ref[j:j + 1, :]).astype(BF16)

    hb = t // halo
    row = pl.BlockSpec((t, d), lambda i: (i, 0))
    return pl.pallas_call(
        body, grid=(m // t,),
        in_specs=[pl.BlockSpec((halo, d), lambda i: (jnp.maximum(i * hb - 1, 0), 0)), row,
                  pl.BlockSpec((halo, d), lambda i: (jnp.minimum((i + 1) * hb, m // halo - 1), 0)),
                  pl.BlockSpec((6, d), lambda i: (0, 0))],
        out_specs=[row] * 6, out_shape=[jax.ShapeDtypeStruct((m, d), BF16)] * 6,
        scratch_shapes=[pltpu.VMEM((t + 2 * halo, d), F32)],
        compiler_params=_cp(1), name="rwkv_shift_mix",
    )(h, h, h, mu)


def _wkv_prep(r, k, v, wl, al, k_k, k_a, r_k):
    m, d = r.shape
    tm = _tile(128, m)

    def body(r_ref, k_ref, v_ref, wl0_ref, wl1_ref, al0_ref, al1_ref, kk_ref, ka_ref, rk_ref,
             okk_ref, olw0_ref, olw1_ref, ob0_ref, ob1_ref, okt0_ref, okt1_ref, obon_ref):
        rv, kv, vv = r_ref[...], k_ref[...], v_ref[...]
        kk = kv * kk_ref[...]
        kk = kk / jnp.maximum(jnp.sqrt(_head_sum(kk * kk)), 1e-12)
        okk_ref[...] = kk.astype(okk_ref.dtype)
        bonus = jnp.zeros_like(rv)
        for wl_ref, al_ref, olw_ref, ob_ref, okt_ref in ((wl0_ref, al0_ref, olw0_ref, ob0_ref, okt0_ref),
                                                          (wl1_ref, al1_ref, olw1_ref, ob1_ref, okt1_ref)):
            w_log = -_softplus(-wl_ref[...]) - 0.5
            olw_ref[...] = -jnp.exp(w_log)
            a = jax.nn.sigmoid(al_ref[...])
            kt = kv * (1.0 + (a - 1.0) * ka_ref[...])
            ob_ref[...] = (kk * a).astype(ob_ref.dtype)
            okt_ref[...] = kt.astype(okt_ref.dtype)
            bonus = bonus + _head_sum(rv * kt * rk_ref[...]) * vv
        obon_ref[...] = bonus

    row = pl.BlockSpec((tm, d), lambda i: (i, 0))
    vec = pl.BlockSpec((1, d), lambda i: (0, 0))
    f32 = jax.ShapeDtypeStruct((m, d), F32)
    bf16 = jax.ShapeDtypeStruct((m, d), BF16)
    kk, lw0, lw1, b0, b1, kt0, kt1, bonus = pl.pallas_call(
        body, grid=(m // tm,),
        in_specs=[row] * 7 + [vec] * 3,
        out_specs=[row] * 8,
        out_shape=[bf16, f32, f32, bf16, bf16, bf16, bf16, f32],
        compiler_params=_cp(1), name="wkv_prep",
    )(r, k, v, wl[0], wl[1], al[0], al[1], k_k.reshape(1, d), k_a.reshape(1, d), r_k.reshape(1, d))
    return kk, (lw0, lw1), (b0, b1), (kt0, kt1), bonus


def _wkv_scan_dir(r, kk, v, lw, b, kt, s_init, lay, reverse):
    m, d = r.shape
    c = WKV_CHUNK
    rows = c * WKV_CHUNKS_PER_STEP
    n_pairs = d // LANES
    pps = math.gcd(WKV_PAIRS_PER_STEP, n_pairs)
    pw = pps * LANES
    n_steps = m // rows
    ctx_steps = lay.n_ctx // rows
    sps_ctx, sps_lat = lay.ctx_len // rows, lay.lat_len // rows
    n_seq = lay.n_ctx_seq + lay.n_lat_seq
    c2 = 2 * c
    sgn = -1 if reverse else 1

    def block_of(s):
        return n_steps - 1 - s if reverse else s

    def seq_of(tb):
        is_ctx = tb < ctx_steps
        seq = jnp.where(is_ctx, tb // sps_ctx, lay.n_ctx_seq + (tb - ctx_steps) // sps_lat)
        pos = jnp.where(is_ctx, tb % sps_ctx, (tb - ctx_steps) % sps_lat)
        return is_ctx, seq, pos, jnp.where(is_ctx, sps_ctx, sps_lat)

    def body(r_ref, kk_ref, v_ref, lw_ref, b_ref, kt_ref, si_ref, y_ref, so_ref, s_scr):
        is_ctx, _, pos, sps = seq_of(block_of(pl.program_id(1)))
        first = pos == (sps - 1 if reverse else 0)
        last = pos == (0 if reverse else sps - 1)

        @pl.when(first)
        def _():
            s_scr[...] = jnp.where(is_ctx, 0.0, si_ref[0])

        ri = lax.broadcasted_iota(jnp.int32, (c2, c2), 0)
        ci = lax.broadcasted_iota(jnp.int32, (c2, c2), 1)
        same = (ri // c) == (ci // c)
        strict = same & ((ri - ci) * sgn > 0)
        incl = same & ((ri - ci) * sgn >= 0)
        eye = jnp.where(ri == ci, 1.0, 0.0)
        end_row = 0 if reverse else c - 1
        step_row = lax.broadcasted_iota(jnp.int32, (c, LANES), 0)

        def running_sum(x):
            s = 1
            while s < c:
                if reverse:
                    x = x + jnp.where(step_row < c - s, pltpu.roll(x, c - s, axis=0), 0.0)
                else:
                    x = x + jnp.where(step_row >= s, pltpu.roll(x, s, axis=0), 0.0)
                s *= 2
            return x
        own = (lax.broadcasted_iota(jnp.int32, (c2, LANES), 0) // c
               == lax.broadcasted_iota(jnp.int32, (c2, LANES), 1) // RWKV_N)

        def stack(xv):
            return jnp.where(own, jnp.concatenate([xv, xv], axis=0), 0.0).astype(BF16)

        chunk_order = list(range(WKV_CHUNKS_PER_STEP))[::sgn]
        probs = [(j, p) for j in chunk_order for p in range(pps)]
        at = lambda ref, j, p: ref[j * c:(j + 1) * c, p * LANES:(p + 1) * LANES].astype(F32)

        lwv = [at(lw_ref, j, p) for j, p in probs]
        cs = [running_sum(x) for x in lwv]
        g = [jnp.exp(x) for x in cs]
        ginv = [jnp.exp(-x) for x in cs]
        gprev = [jnp.exp(x - w) for x, w in zip(cs, lwv)]
        gend = [jnp.exp(x[end_row:end_row + 1] - x) for x in cs]
        gall = [jnp.exp(x[end_row:end_row + 1]) for x in cs]
        kkv = [at(kk_ref, j, p) for j, p in probs]
        bv = [at(b_ref, j, p) for j, p in probs]
        ktv = [at(kt_ref, j, p) for j, p in probs]
        n = len(probs)
        xa = [stack(-kkv[i] * gprev[i]) for i in range(n)]
        xr = [stack(at(r_ref, *probs[i]) * g[i]) for i in range(n)]
        xb = [stack(bv[i] * ginv[i]) for i in range(n)]
        xk = [stack(ktv[i] * ginv[i]) for i in range(n)]
        xb_end = [stack(bv[i] * gend[i]) for i in range(n)]
        xk_end = [stack(ktv[i] * gend[i]) for i in range(n)]
        xv = [stack(at(v_ref, *probs[i])) for i in range(n)]
        mm = [_dot_nt(jnp.concatenate([xa[i], xr[i]], axis=0), jnp.concatenate([xb[i], xk[i]], axis=0))
              for i in range(n)]
        m_ab = [jnp.where(strict, x[:c2, :c2], 0.0) for x in mm]
        m_ak = [jnp.where(strict, x[:c2, c2:], 0.0).astype(BF16) for x in mm]
        m_rb = [jnp.where(incl, x[c2:, :c2], 0.0).astype(BF16) for x in mm]
        m_rk = [jnp.where(incl, x[c2:, c2:], 0.0).astype(BF16) for x in mm]
        akv = [_dot(m_ak[i], xv[i]).astype(BF16) for i in range(n)]
        pj = [x.astype(BF16) for x in m_ab]
        tinv = [eye + x for x in m_ab]
        pj = [_dot(x, x).astype(BF16) for x in pj]
        for _ in range(int(math.log2(c)) - 2):
            both = [_dot(jnp.concatenate([pj[i], tinv[i].astype(BF16)], axis=0), pj[i]) for i in range(n)]
            pj = [x[:c2].astype(BF16) for x in both]
            tinv = [tinv[i] + both[i][c2:] for i in range(n)]
        tinv = [(tinv[i] + _dot(tinv[i].astype(BF16), pj[i])).astype(BF16) for i in range(n)]
        pq = [_dot(tinv[i], jnp.concatenate([xa[i], akv[i]], axis=1)).astype(BF16) for i in range(n)]
        ab = [_dot_tn(pq[i], xb_end[i]) for i in range(n)]
        a_c = [(eye * gall[i] + ab[i][:LANES]).astype(BF16) for i in range(n)]
        b_c = [ab[i][LANES:] + _dot_tn(xv[i], xk_end[i]) for i in range(n)]
        cd = [_dot(m_rb[i], pq[i]) for i in range(n)]
        c_c = [(xr[i].astype(F32) + cd[i][:, :LANES]).astype(BF16) for i in range(n)]
        d_c = [cd[i][:, LANES:] + _dot(m_rk[i], xv[i]) for i in range(n)]
        for p in range(pps):
            s = s_scr[p]
            for i, (j, pp) in enumerate(probs):
                if pp != p:
                    continue
                s_hi, s_lo = _split_bf16(s, 2)
                yy = _dot_nt(c_c[i], s_hi) + _dot_nt(c_c[i], s_lo) + d_c[i]
                y_ref[j * c:(j + 1) * c, p * LANES:(p + 1) * LANES] = yy[:c] + yy[c:]
                s = _dot(s_hi, a_c[i]) + _dot(s_lo, a_c[i]) + b_c[i]
            s_scr[p] = s

        @pl.when(last)
        def _():
            so_ref[0] = s_scr[...]

    tok = pl.BlockSpec((rows, pw), lambda lb, s: (block_of(s), lb))
    st = (1, pps, LANES, LANES)
    return pl.pallas_call(
        body, grid=(n_pairs // pps, n_steps),
        in_specs=[tok] * 6 + [pl.BlockSpec(st, lambda lb, s: (
            jnp.clip(seq_of(block_of(s))[1] - lay.n_ctx_seq, 0, lay.n_lat_seq - 1), lb, 0, 0))],
        out_specs=[tok, pl.BlockSpec(st, lambda lb, s: (seq_of(block_of(s))[1], lb, 0, 0))],
        out_shape=[jax.ShapeDtypeStruct((m, d), F32), jax.ShapeDtypeStruct((n_seq, n_pairs, LANES, LANES), F32)],
        scratch_shapes=[pltpu.VMEM((pps, LANES, LANES), F32)],
        compiler_params=_cp(2, arbitrary_last=True), name="wkv_scan_bwd" if reverse else "wkv_scan_fwd",
    )(r, kk, v, lw, b, kt, s_init)


def _wkv_post(y_fwd, y_bwd, bonus, g, ln_g, ln_b):
    m, d = bonus.shape
    tm = _tile(128, m)

    def body(yf_ref, yb_ref, bon_ref, g_ref, lg_ref, lb_ref, o_ref):
        yv = yf_ref[...] + yb_ref[...]
        mu = _head_sum(yv) * (1.0 / RWKV_N)
        yc = yv - mu
        var = _head_sum(yc * yc) * (1.0 / RWKV_N)
        o = yc * lax.rsqrt(var + GN_EPS) * lg_ref[...] + lb_ref[...] + bon_ref[...]
        o_ref[...] = (o * g_ref[...]).astype(o_ref.dtype)

    row = pl.BlockSpec((tm, d), lambda i: (i, 0))
    vec = pl.BlockSpec((1, d), lambda i: (0, 0))
    return pl.pallas_call(
        body, grid=(m // tm,),
        in_specs=[row, row, row, row, vec, vec],
        out_specs=row, out_shape=jax.ShapeDtypeStruct((m, d), BF16), compiler_params=_cp(1), name="wkv_post",
    )(y_fwd, y_bwd, bonus, g, ln_g.reshape(1, d), ln_b.reshape(1, d))


def _pad_cols(w, n):
    return jnp.pad(w, ((0, 0), (0, n - w.shape[1])))


def _pad_rows(w, n):
    return jnp.pad(w, ((0, n - w.shape[0]), (0, 0)))


def _rwkv_layer(x, hf, mod3, lay, norm_ffn, router, s_fwd, s_bwd, mu, w_r, w_k, w_v, w_o, w0, w1, w2, a0, a1, a2,
                g1, g2, k_k, k_a, r_k, ln_g, ln_b):
    m, d = x.shape
    tm = lay.tile(1024)
    xr, xw, xk, xv, xa, xg = _shift_mix(hf, mu, lay)
    sq = dict(tm=lay.tile(512), tn=_tile(1024, d), n_outer=True)
    r = _mm_act(xr, w_r, None, None, F32, name="rwkv_r", **sq)
    k = _mm_act(xk, w_k, None, None, F32, name="rwkv_k", **sq)
    v = _mm_act(xv, w_v, None, None, F32, name="rwkv_v", **sq)
    gh = _mm_act(xg, g1, None, jax.nn.sigmoid, BF16, tm=tm, tn=g1.shape[1], name="rwkv_g1")
    g = _mm_act(gh, g2, None, None, F32, name="rwkv_g2", **sq)
    both = lambda w: jnp.concatenate([_pad_cols(w[0], LANES), _pad_cols(w[1], LANES)], axis=1)
    th = _mm_act(xw, both(w1), None, jnp.tanh, BF16, tm=tm, tn=2 * LANES, name="rwkv_w1")
    ah = _mm_act(xa, both(a1), None, None, BF16, tm=tm, tn=2 * LANES, name="rwkv_a1")
    wl, al = [], []
    for dr in range(2):
        cols = slice(dr * LANES, (dr + 1) * LANES)
        wl.append(_mm_act(th[:, cols], _pad_rows(w2[dr], LANES), w0[dr], None, F32, name="rwkv_w2", **sq))
        al.append(_mm_act(ah[:, cols], _pad_rows(a2[dr], LANES), a0[dr], None, F32, name="rwkv_a2", **sq))
    kk, lw, b, kt, bonus = _wkv_prep(r, k, v, wl, al, k_k, k_a, r_k.reshape(-1))
    n_pairs = d // LANES
    ys, finals = [], []
    for dr, s0 in enumerate((s_fwd, s_bwd)):
        st = s0.astype(F32).reshape(-1, n_pairs, 2, RWKV_N, RWKV_N)
        s_init = jnp.zeros((st.shape[0], n_pairs, LANES, LANES), F32)
        s_init = s_init.at[..., :RWKV_N, :RWKV_N].set(st[:, :, 0]).at[..., RWKV_N:, RWKV_N:].set(st[:, :, 1])
        y, s_fin = _wkv_scan_dir(r, kk, v, lw[dr], b[dr], kt[dr], s_init, lay, reverse=dr == 1)
        sf = s_fin[:lay.n_ctx_seq]
        s_pair = jnp.stack([sf[..., :RWKV_N, :RWKV_N], sf[..., RWKV_N:, RWKV_N:]], axis=2)
        ys.append(y)
        finals.append(s_pair.reshape(lay.n_ctx_seq, 2 * n_pairs, RWKV_N, RWKV_N))
    og = _wkv_post(ys[0], ys[1], bonus, g, ln_g, ln_b)
    x, h_ffn, top_e, top_p = _out_norm(og, w_o, None, x, mod3, lay, norm_ffn, BF16, router=router, name="rwkv_out")
    return x, h_ffn, top_e, top_p, finals[0], finals[1]


def kernel(x_prompt, x_sample, cache_ckv_l0, cache_krope_l0, state_wkv_fwd_l3, state_wkv_bwd_l3, c, c_ctx, l0_w_mod, l0_b_mod, l0_norm_mix, l0_norm_ffn, l0_mla_w_down, l0_mla_q_norm, l0_mla_kv_norm, l0_mla_w_uq, l0_mla_w_ukv, l0_mla_w_o, l0_ffn_w_gate, l0_ffn_w_up, l0_ffn_w_down, l1_w_mod, l1_b_mod, l1_norm_mix, l1_norm_ffn, l1_conv_w_in, l1_conv_b_in, l1_conv_w_dw, l1_conv_b_dw, l1_conv_ln_g, l1_conv_ln_b, l1_conv_w_out, l1_conv_b_out, l1_moe_w_router, l1_moe_b_router, l1_moe_w_gate, l1_moe_w_up, l1_moe_w_down, l2_w_mod, l2_b_mod, l2_norm_mix, l2_norm_ffn, l2_sgu_w_in, l2_sgu_b_in, l2_sgu_ln_g, l2_sgu_ln_b, l2_sgu_w_s, l2_sgu_b_s, l2_sgu_w_out, l2_sgu_b_out, l2_ffn_w_gate, l2_ffn_w_up, l2_ffn_w_down, l3_w_mod, l3_b_mod, l3_norm_mix, l3_norm_ffn, l3_rwkv_mu, l3_rwkv_w_r, l3_rwkv_w_k, l3_rwkv_w_v, l3_rwkv_w_o, l3_rwkv_w0, l3_rwkv_w1, l3_rwkv_w2, l3_rwkv_a0, l3_rwkv_a1, l3_rwkv_a2, l3_rwkv_g1, l3_rwkv_g2, l3_rwkv_k_k, l3_rwkv_k_a, l3_rwkv_r_k, l3_rwkv_ln_g, l3_rwkv_ln_b, l3_moe_w_router, l3_moe_b_router, l3_moe_w_gate, l3_moe_w_up, l3_moe_w_down, norm_out):
    n_ctx_seq, ctx_len, d = x_prompt.shape
    n_lat_seq, lat_len, _ = x_sample.shape
    lay = _Lay(n_ctx_seq, ctx_len, n_lat_seq, lat_len)
    assert n_lat_seq + 1 <= SUBLANES
    x = jnp.concatenate([x_prompt.reshape(-1, d), x_sample.reshape(-1, d)], axis=0)
    cond8 = jnp.zeros((SUBLANES, d), F32).at[0].set(c_ctx).at[1:1 + n_lat_seq].set(c)

    mods = [_ada_mod(cond8, w, b) for w, b in ((l0_w_mod, l0_b_mod), (l1_w_mod, l1_b_mod), (l2_w_mod, l2_b_mod),
                                               (l3_w_mod, l3_b_mod))]

    h = _norm_mod(x, l0_norm_mix, mods[0], (0, 1), lay, BF16)
    x, h, ckv_l0, krope_l0 = _mla_layer(x, h, mods[0], lay, l0_norm_ffn, cache_ckv_l0, cache_krope_l0, l0_mla_w_down,
                                        l0_mla_q_norm, l0_mla_kv_norm, l0_mla_w_uq, l0_mla_w_ukv, l0_mla_w_o)
    x = _dense_ffn(x, h, l0_ffn_w_gate, l0_ffn_w_up, l0_ffn_w_down, mods[0], lay)

    h = _norm_mod(x, l1_norm_mix, mods[1], (0, 1), lay, BF16)
    router = _router_ops(l1_moe_w_router, l1_moe_b_router)
    x, h, top_e, top_p = _conformer_layer(x, h, mods[1], lay, l1_norm_ffn, router, l1_conv_w_in, l1_conv_b_in,
                                          l1_conv_w_dw, l1_conv_b_dw, l1_conv_ln_g, l1_conv_ln_b, l1_conv_w_out,
                                          l1_conv_b_out)
    x, h = _moe_ffn(x, h, top_e, top_p, mods[1], lay, router[3], l1_moe_w_gate, l1_moe_w_up, l1_moe_w_down,
                    (l2_norm_mix, mods[2], (0, 1), BF16, True))

    x, h = _sgu_layer(x, h, mods[2], lay, l2_norm_ffn, l2_sgu_w_in, l2_sgu_b_in, l2_sgu_ln_g, l2_sgu_ln_b,
                      l2_sgu_w_s, l2_sgu_b_s, l2_sgu_w_out, l2_sgu_b_out)
    x = _dense_ffn(x, h, l2_ffn_w_gate, l2_ffn_w_up, l2_ffn_w_down, mods[2], lay)

    hf = _norm_mod(x, l3_norm_mix, mods[3], (0, 1), lay, F32)
    router = _router_ops(l3_moe_w_router, l3_moe_b_router)
    x, h, top_e, top_p, wkv_fwd, wkv_bwd = _rwkv_layer(
        x, hf, mods[3], lay, l3_norm_ffn, router, state_wkv_fwd_l3, state_wkv_bwd_l3, l3_rwkv_mu, l3_rwkv_w_r,
        l3_rwkv_w_k, l3_rwkv_w_v, l3_rwkv_w_o, l3_rwkv_w0, l3_rwkv_w1, l3_rwkv_w2, l3_rwkv_a0, l3_rwkv_a1,
        l3_rwkv_a2, l3_rwkv_g1, l3_rwkv_g2, l3_rwkv_k_k, l3_rwkv_k_a, l3_rwkv_r_k, l3_rwkv_ln_g, l3_rwkv_ln_b)
    y = _moe_ffn(x, h, top_e, top_p, mods[3], lay, router[3], l3_moe_w_gate, l3_moe_w_up, l3_moe_w_down,
                 (norm_out, None, None, F32, False))
    y_prompt = y[:lay.n_ctx].reshape(n_ctx_seq, ctx_len, d)
    y_sample = y[lay.n_ctx:].reshape(n_lat_seq, lat_len, d)
    return (y_prompt, y_sample, ckv_l0, krope_l0, wkv_fwd, wkv_bwd)
```

```python
import functools
import math

import jax
import jax.numpy as jnp
import numpy as np
from jax import lax
from jax.experimental import pallas as pl
from jax.experimental.pallas import tpu as pltpu

F32 = jnp.float32
BF16 = jnp.bfloat16

NORM_EPS = 1e-6
LN_EPS = 1e-5
GN_EPS = 64e-5
QK_NOPE = 128
QK_ROPE = 64
V_DIM = 128
ROPE_THETA = 10000.0
GRID_W = 64
RWKV_N = 64
N_MOD = 6
TOP_K = 2

LANES = 128
SUBLANES = 8
VMEM_LIMIT_BYTES = 56 * 1024 * 1024

WKV_CHUNK = 64
WKV_PAIRS_PER_STEP = 4
WKV_CHUNKS_PER_STEP = 4
MOE_ROWS = 512


def _cp(n_axes, arbitrary_last=False):
    sem = ["parallel"] * n_axes
    if arbitrary_last:
        sem[-1] = "arbitrary"
    return pltpu.CompilerParams(dimension_semantics=tuple(sem), vmem_limit_bytes=VMEM_LIMIT_BYTES)


def _tile(pref, *sizes):
    t = pref
    while any(s % t for s in sizes):
        t //= 2
    assert t >= SUBLANES
    return t


class _Lay:
    def __init__(self, n_ctx_seq, ctx_len, n_lat_seq, lat_len):
        self.n_ctx_seq, self.ctx_len, self.n_lat_seq, self.lat_len = n_ctx_seq, ctx_len, n_lat_seq, lat_len
        self.n_ctx = n_ctx_seq * ctx_len
        self.n_tok = self.n_ctx + n_lat_seq * lat_len

    def tile(self, pref):
        return _tile(pref, self.n_ctx, self.lat_len)

    def group(self, start):
        return jnp.where(start < self.n_ctx, 0, 1 + (start - self.n_ctx) // self.lat_len)

    def seq_pos(self, start):
        is_ctx = start < self.n_ctx
        pos = jnp.where(is_ctx, start % self.ctx_len, (start - self.n_ctx) % self.lat_len)
        return pos, jnp.where(is_ctx, self.ctx_len, self.lat_len)


def _silu(x):
    return x * jax.nn.sigmoid(x)


def _gelu_tanh(x):
    return 0.5 * x * (1.0 + jnp.tanh(math.sqrt(2.0 / math.pi) * (x + 0.044715 * (x * x * x))))


def _softplus(x):
    return jnp.maximum(x, 0.0) + jnp.log(1.0 + jnp.exp(-jnp.abs(x)))


def _split_bf16(x, n):
    parts = []
    r = x
    for _ in range(n):
        h = r.astype(BF16)
        parts.append(h)
        r = r - h.astype(F32)
    return parts


def _dot(a, b):
    return jnp.dot(a, b, preferred_element_type=F32)


def _dot_nt(a, b):
    return lax.dot_general(a, b, (((1,), (1,)), ((), ())), preferred_element_type=F32)


def _dot_tn(a, b):
    return lax.dot_general(a, b, (((0,), (0,)), ((), ())), preferred_element_type=F32)


def _pair_ones():
    r = lax.broadcasted_iota(jnp.int32, (LANES, LANES), 0) // RWKV_N
    c = lax.broadcasted_iota(jnp.int32, (LANES, LANES), 1) // RWKV_N
    return jnp.where(r == c, 1.0, 0.0).astype(BF16)


def _head_sum(x):
    ones = _pair_ones()
    cols = []
    for t in range(x.shape[1] // LANES):
        xt = x[:, t * LANES:(t + 1) * LANES]
        cols.append(sum(_dot(p, ones) for p in _split_bf16(xt, 3)))
    return jnp.concatenate(cols, axis=1)


def _mm(a, ws, *, tm, tn, gn, epi, outs, vecs=(), wcols=None, n_outer=False, name="mm"):
    m, k = a.shape
    gm = m // tm
    nw, nv = len(ws), len(vecs)
    wcols = wcols or [lambda j: j] * nw
    if n_outer:
        grid = (gn, gm)
        wrap = lambda fn: (lambda g0, g1: fn(g1, g0))
    else:
        grid = (gm, gn)
        wrap = lambda fn: (lambda g0, g1: fn(g0, g1))
    in_specs = [pl.BlockSpec((tm, k), wrap(lambda i, j: (i, 0)))]
    for wc in wcols:
        in_specs.append(pl.BlockSpec((k, tn), wrap(lambda i, j, wc=wc: (0, wc(j)))))
    for _, bshape, fn in vecs:
        in_specs.append(pl.BlockSpec(bshape, wrap(fn)))
    out_shape = [jax.ShapeDtypeStruct(s, d) for s, d, _, _ in outs]
    out_specs = [pl.BlockSpec(b, wrap(fn)) for _, _, b, fn in outs]

    n_out = len(outs)

    def body(*refs):
        a_ref = refs[0]
        w_refs = refs[1:1 + nw]
        v_refs = refs[1 + nw:1 + nw + nv]
        o_refs = refs[1 + nw + nv:1 + nw + nv + n_out]
        wb_refs = refs[1 + nw + nv + n_out:]
        av = a_ref[...]
        if av.dtype != BF16:
            av = av.astype(BF16)
        if n_outer:
            @pl.when(pl.program_id(1) == 0)
            def _():
                for w, wb in zip(w_refs, wb_refs):
                    wb[...] = w[...].astype(BF16)

            accs = [_dot(av, wb[...]) for wb in wb_refs]
        else:
            accs = [_dot(av, w[...].astype(BF16)) for w in w_refs]
        for o, r in zip(o_refs, epi(accs, v_refs)):
            o[...] = r.reshape(o.shape).astype(o.dtype)

    res = pl.pallas_call(body, grid=grid, in_specs=in_specs, out_specs=out_specs, out_shape=out_shape,
                         scratch_shapes=[pltpu.VMEM((k, tn), BF16)] * nw if n_outer else [],
                         compiler_params=_cp(2, arbitrary_last=n_outer), name=name)(a, *ws, *[v[0] for v in vecs])
    return res


def _bias_vec(b, tn):
    return (b.reshape(1, -1), (1, tn), lambda i, j: (0, j))


def _mm_act(a, w, bias, act, out_dtype, *, tm, tn, name, n_outer=False):
    m, n = a.shape[0], w.shape[1]
    vecs = [] if bias is None else [_bias_vec(bias, tn)]

    def epi(accs, v):
        x = accs[0]
        if bias is not None:
            x = x + v[0][...]
        return [act(x) if act is not None else x]

    return _mm(a, [w], tm=tm, tn=tn, gn=pl.cdiv(n, tn), epi=epi, vecs=vecs, n_outer=n_outer, name=name,
               outs=[((m, n), out_dtype, (tm, tn), lambda i, j: (i, j))])[0]


def _mm_dual(a, w1, w2, b1, b2, fn, out_dtype, *, n, col2, tm, tn, name, n_outer=True):
    m = a.shape[0]
    gn = pl.cdiv(n, tn)
    vecs = []
    if b1 is not None:
        vecs = [(b1.reshape(1, -1), (1, tn), lambda i, j: (0, j)),
                (b2.reshape(1, -1), (1, tn), lambda i, j: (0, j + col2))]

    def epi(accs, v):
        x, y = accs
        if b1 is not None:
            x, y = x + v[0][...], y + v[1][...]
        return [fn(x, y)]

    return _mm(a, [w1, w2], tm=tm, tn=tn, gn=gn, epi=epi, vecs=vecs, n_outer=n_outer, name=name,
               wcols=[lambda j: j, lambda j: j + col2],
               outs=[((m, n), out_dtype, (tm, tn), lambda i, j: (i, j))])[0]


def _mm_resid(a, w, bias, resid, mod3, gate_row, lay, *, tm, tn, name, n_outer=False):
    m, n = resid.shape
    vecs = [(mod3, (1, N_MOD, tn), lambda i, j: (lay.group(i * tm), 0, j)),
            (resid, (tm, tn), lambda i, j: (i, j))]
    if bias is not None:
        vecs.append(_bias_vec(bias, tn))

    def epi(accs, v):
        x = accs[0]
        if bias is not None:
            x = x + v[2][...]
        return [v[1][...] + v[0][0, gate_row:gate_row + 1, :] * x]

    return _mm(a, [w], tm=tm, tn=tn, gn=n // tn, epi=epi, vecs=vecs, n_outer=n_outer, name=name,
               outs=[((m, n), F32, (tm, tn), lambda i, j: (i, j))])[0]


def _ada_mod(cond8, w_mod, b_mod):
    d, n = w_mod.shape
    tn = _tile(1536, n)

    def body(c_ref, w_ref, b_ref, o_ref):
        c = c_ref[...]
        o_ref[...] = _dot(_silu(c).astype(BF16), w_ref[...].astype(BF16)) + b_ref[...]

    m = pl.pallas_call(
        body, grid=(n // tn,),
        in_specs=[pl.BlockSpec((SUBLANES, d), lambda j: (0, 0)), pl.BlockSpec((d, tn), lambda j: (0, j)),
                  pl.BlockSpec((1, tn), lambda j: (0, j))],
        out_specs=pl.BlockSpec((SUBLANES, tn), lambda j: (0, j)),
        out_shape=jax.ShapeDtypeStruct((SUBLANES, n), F32), compiler_params=_cp(1), name="ada_mod",
    )(cond8, w_mod, b_mod.reshape(1, n))
    return m.reshape(SUBLANES, N_MOD, d)


def _norm_rows(pre_fn, pre_ops, gain, mod3, rows, lay, out_dtype, *, m, d, emit_x=False, router=None,
               name="norm_mod"):
    tm = lay.tile(256)
    n_e = None if router is None else router[3]
    n_pre = len(pre_ops)

    def body(*refs):
        g_ref = refs[n_pre]
        pos = n_pre + 1
        xf = pre_fn(*refs[:n_pre])
        y = xf * lax.rsqrt(jnp.mean(xf * xf, axis=-1, keepdims=True) + NORM_EPS) * g_ref[...]
        if rows is not None:
            mod_ref = refs[pos]
            pos += 1
            y = y * (1.0 + mod_ref[0, rows[1]:rows[1] + 1, :]) + mod_ref[0, rows[0]:rows[0] + 1, :]
        if router is not None:
            whi_ref, wlo_ref, rb_ref = refs[pos:pos + 3]
            pos += 3
        if emit_x:
            refs[pos][...] = xf
            pos += 1
        o_ref = refs[pos]
        o_ref[...] = y.astype(o_ref.dtype)
        if router is not None:
            e_ref, p_ref = refs[pos + 1], refs[pos + 2]
            y_hi, y_lo = _split_bf16(y, 2)
            logits = _dot(y_hi, whi_ref[...]) + _dot(y_lo, whi_ref[...]) + _dot(y_hi, wlo_ref[...]) + rb_ref[...]
            lane = lax.broadcasted_iota(jnp.int32, logits.shape, 1)
            logits = jnp.where(lane < n_e, logits, -jnp.inf)
            m1 = jnp.max(logits, axis=-1, keepdims=True)
            i1 = jnp.min(jnp.where(logits == m1, lane, LANES), axis=-1, keepdims=True)
            rest = jnp.where(lane == i1, -jnp.inf, logits)
            m2 = jnp.max(rest, axis=-1, keepdims=True)
            i2 = jnp.min(jnp.where(rest == m2, lane, LANES), axis=-1, keepdims=True)
            e2 = jnp.exp(m2 - m1)
            p1 = 1.0 / (1.0 + e2)
            e_ref[...] = jnp.where(lane == 0, i1, jnp.where(lane == 1, i2, 0))
            p_ref[...] = jnp.where(lane == 0, p1, jnp.where(lane == 1, e2 * p1, 0.0))

    in_specs = [pl.BlockSpec(bshape, fn) for _, bshape, fn in pre_ops] + [pl.BlockSpec((1, d), lambda i: (0, 0))]
    args = [op[0] for op in pre_ops] + [gain.reshape(1, d)]
    if rows is not None:
        in_specs.append(pl.BlockSpec((1, N_MOD, d), lambda i: (lay.group(i * tm), 0, 0)))
        args.append(mod3)
    row = pl.BlockSpec((tm, d), lambda i: (i, 0))
    out_shape = ([jax.ShapeDtypeStruct((m, d), F32)] if emit_x else []) + [jax.ShapeDtypeStruct((m, d), out_dtype)]
    out_specs = [row] * len(out_shape)
    if router is not None:
        in_specs += [pl.BlockSpec((d, LANES), lambda i: (0, 0)), pl.BlockSpec((d, LANES), lambda i: (0, 0)),
                     pl.BlockSpec((1, LANES), lambda i: (0, 0))]
        args += list(router[:3])
        out_shape += [jax.ShapeDtypeStruct((m, LANES), jnp.int32), jax.ShapeDtypeStruct((m, LANES), F32)]
        out_specs += [pl.BlockSpec((tm, LANES), lambda i: (i, 0))] * 2
    res = pl.pallas_call(body, grid=(m // tm,), in_specs=in_specs, out_specs=out_specs, out_shape=out_shape,
                         compiler_params=_cp(1), name=name)(*args)
    return res if len(res) > 1 else res[0]


def _norm_mod(x, gain, mod3, rows, lay, out_dtype, *, name="norm_mod"):
    m, d = x.shape
    tm = lay.tile(256)
    return _norm_rows(lambda x_ref: x_ref[...], [(x, (tm, d), lambda i: (i, 0))], gain, mod3, rows, lay, out_dtype,
                      m=m, d=d, name=name)


def _out_norm(a, w, bias, resid, mod3, lay, gain, out_dtype, *, router=None, name):
    m, d = resid.shape
    k = a.shape[1]
    tm = lay.tile(256)
    has_bias = bias is not None

    def pre(a_ref, w_ref, mod_ref, r_ref, *b_ref):
        y = _dot(a_ref[...], w_ref[...])
        if has_bias:
            y = y + b_ref[0][...]
        return r_ref[...] + mod_ref[0, 2:3, :] * y

    ops = [(a, (tm, k), lambda i: (i, 0)), (w.astype(BF16), (k, d), lambda i: (0, 0)),
           (mod3, (1, N_MOD, d), lambda i: (lay.group(i * tm), 0, 0)), (resid, (tm, d), lambda i: (i, 0))]
    if has_bias:
        ops.append((bias.reshape(1, d), (1, d), lambda i: (0, 0)))
    return _norm_rows(pre, ops, gain, mod3, (3, 4), lay, out_dtype, m=m, d=d, emit_x=True, router=router, name=name)


def _dense_ffn(x, h, w_gate, w_up, w_down, mod3, lay):
    d_ff = w_gate.shape[1]
    hid = _mm_dual(h, w_gate, w_up, None, None, lambda g, u: _silu(g) * u, BF16, n=d_ff, col2=0,
                   tm=lay.tile(1024), tn=512, name="ffn_up")
    return _mm_resid(hid, w_down, None, x, mod3, 5, lay, tm=lay.tile(512), tn=512, n_outer=True,
                     name="ffn_down")


def _moe_up(xs, w_gate, w_up, block_e, block_src, n_used, *, tm):
    n_slots, d = xs.shape
    n_e, _, d_ff = w_gate.shape
    tn = _tile(1024, d_ff)
    n_blocks = n_slots // tm

    def body(be_ref, bs_ref, nu_ref, x_ref, wg_ref, wu_ref, o_ref, wgb_ref, wub_ref):
        i = pl.program_id(1)

        @pl.when((i == 0) | (be_ref[i] != be_ref[jnp.maximum(i - 1, 0)]))
        def _():
            wgb_ref[...] = wg_ref[...].astype(BF16)
            wub_ref[...] = wu_ref[...].astype(BF16)

        @pl.when(i < nu_ref[0])
        def _():
            xv = x_ref[...]
            g = _dot(xv, wgb_ref[...])
            u = _dot(xv, wub_ref[...])
            o_ref[...] = (_silu(g) * u).astype(o_ref.dtype)

    grid_spec = pltpu.PrefetchScalarGridSpec(
        num_scalar_prefetch=3, grid=(d_ff // tn, n_blocks),
        in_specs=[pl.BlockSpec((tm, d), lambda j, i, be, bs, nu: (bs[i], 0)),
                  pl.BlockSpec((None, d, tn), lambda j, i, be, bs, nu: (be[i], 0, j)),
                  pl.BlockSpec((None, d, tn), lambda j, i, be, bs, nu: (be[i], 0, j))],
        out_specs=pl.BlockSpec((tm, tn), lambda j, i, be, bs, nu: (bs[i], j)),
        scratch_shapes=[pltpu.VMEM((d, tn), BF16)] * 2)
    return pl.pallas_call(body, grid_spec=grid_spec, out_shape=jax.ShapeDtypeStruct((n_slots, d_ff), BF16),
                          compiler_params=_cp(2, arbitrary_last=True), name="moe_up"
                          )(block_e, block_src, n_used, xs, w_gate, w_up)


def _moe_down(hid, w_down, block_e, block_src, n_used, *, tm):
    n_slots, d_ff = hid.shape
    d = w_down.shape[2]
    tn = _tile(512, d)
    n_blocks = n_slots // tm

    def body(be_ref, bs_ref, nu_ref, h_ref, w_ref, o_ref, wb_ref):
        i = pl.program_id(1)

        @pl.when((i == 0) | (be_ref[i] != be_ref[jnp.maximum(i - 1, 0)]))
        def _():
            wb_ref[...] = w_ref[...].astype(BF16)

        @pl.when(i < nu_ref[0])
        def _():
            o_ref[...] = _dot(h_ref[...], wb_ref[...]).astype(o_ref.dtype)

    grid_spec = pltpu.PrefetchScalarGridSpec(
        num_scalar_prefetch=3, grid=(d // tn, n_blocks),
        in_specs=[pl.BlockSpec((tm, d_ff), lambda j, i, be, bs, nu: (bs[i], 0)),
                  pl.BlockSpec((None, d_ff, tn), lambda j, i, be, bs, nu: (be[i], 0, j))],
        out_specs=pl.BlockSpec((tm, tn), lambda j, i, be, bs, nu: (bs[i], j)),
        scratch_shapes=[pltpu.VMEM((d_ff, tn), BF16)])
    return pl.pallas_call(body, grid_spec=grid_spec, out_shape=jax.ShapeDtypeStruct((n_slots, d), BF16),
                          compiler_params=_cp(2, arbitrary_last=True), name="moe_down"
                          )(block_e, block_src, n_used, hid, w_down)


def _moe_combine(x, y0, y1, top_p, mod3, lay, nxt):
    m, d = x.shape
    tm = lay.tile(256)
    gain, next_mod3, rows, out_dtype, emit_x = nxt

    def pre(x_ref, a_ref, b_ref, p_ref, mod_ref):
        mix = p_ref[:, 0:1] * a_ref[...].astype(F32) + p_ref[:, 1:2] * b_ref[...].astype(F32)
        return x_ref[...] + mod_ref[0, 5:6, :] * mix

    row = lambda arr: (arr, (tm, d), lambda i: (i, 0))
    ops = [row(x), row(y0), row(y1), (top_p, (tm, LANES), lambda i: (i, 0)),
           (mod3, (1, N_MOD, d), lambda i: (lay.group(i * tm), 0, 0))]
    return _norm_rows(pre, ops, gain, next_mod3, rows, lay, out_dtype, m=m, d=d, emit_x=emit_x,
                      name="moe_combine")


def _router_ops(w_router, b_router):
    d, n_e = w_router.shape
    wr = jnp.zeros((d, LANES), F32).at[:, :n_e].set(w_router)
    wr_hi = wr.astype(BF16)
    wr_lo = (wr - wr_hi.astype(F32)).astype(BF16)
    rb = jnp.zeros((1, LANES), F32).at[0, :n_e].set(b_router.astype(F32))
    return wr_hi, wr_lo, rb, n_e


def _moe_ffn(x, h, top_e, top_p, mod3, lay, n_e, w_gate, w_up, w_down, nxt):
    n_tok, d = x.shape
    tm = MOE_ROWS
    nk = n_tok * TOP_K
    flat_e = top_e[:, :TOP_K].reshape(nk)
    onehot = (flat_e[:, None] == jnp.arange(n_e, dtype=jnp.int32)[None, :]).astype(jnp.int32)
    rank = jnp.take_along_axis(jnp.cumsum(onehot, axis=0) - onehot, flat_e[:, None], axis=1)[:, 0]
    counts = jnp.sum(onehot, axis=0)
    padded = (counts + tm - 1) // tm * tm
    pad_end = jnp.cumsum(padded)
    dest = (pad_end - padded)[flat_e] + rank
    n_blocks = -(-nk // tm) + n_e
    n_slots = n_blocks * tm
    slot_tok = jnp.zeros((n_slots,), jnp.int32).at[dest].set(jnp.arange(nk, dtype=jnp.int32) // TOP_K)
    n_used = (pad_end[-1] // tm).astype(jnp.int32)
    blk = jnp.arange(n_blocks, dtype=jnp.int32)
    block_src = jnp.minimum(blk, n_used - 1)
    block_e = jnp.minimum(jnp.searchsorted(pad_end, block_src * tm, side="right"), n_e - 1).astype(jnp.int32)
    xs = jnp.take(h, slot_tok, axis=0, mode="clip")
    hid = _moe_up(xs, w_gate, w_up, block_e, block_src, n_used.reshape(1), tm=tm)
    ys = _moe_down(hid, w_down, block_e, block_src, n_used.reshape(1), tm=tm)
    pos = dest.reshape(n_tok, TOP_K)
    return _moe_combine(x, jnp.take(ys, pos[:, 0], axis=0, mode="clip"),
                        jnp.take(ys, pos[:, 1], axis=0, mode="clip"), top_p, mod3, lay, nxt)


def _mla_mid(dq, q_norm, kv_norm, cos64, sin64, q_lora, kv_lora, lay):
    m, n = dq.shape
    tm = lay.tile(512)
    o_kd, o_ko, o_ks = q_lora + kv_lora, q_lora + kv_lora + QK_ROPE, q_lora + kv_lora + 2 * QK_ROPE

    def body(x_ref, qg_ref, kg_ref, cos_ref, sin_ref, cq_ref, ckv_ref, kro_ref, krr_ref):
        cq = x_ref[:, 0:q_lora]
        cq_ref[...] = (cq * lax.rsqrt(jnp.mean(cq * cq, axis=-1, keepdims=True) + NORM_EPS)
                       * qg_ref[...]).astype(cq_ref.dtype)
        ckv = x_ref[:, q_lora:q_lora + kv_lora]
        ckv_ref[...] = ckv * lax.rsqrt(jnp.mean(ckv * ckv, axis=-1, keepdims=True) + NORM_EPS) * kg_ref[...]
        kro_ref[...] = x_ref[:, o_ko:o_ko + QK_ROPE]
        krr_ref[...] = (x_ref[:, o_kd:o_kd + QK_ROPE] * cos_ref[...]
                        + x_ref[:, o_ks:o_ks + QK_ROPE] * sin_ref[...]).astype(krr_ref.dtype)

    rope = pl.BlockSpec((tm, QK_ROPE), lambda i: (i, 0))
    return pl.pallas_call(
        body, grid=(m // tm,),
        in_specs=[pl.BlockSpec((tm, n), lambda i: (i, 0)), pl.BlockSpec((1, q_lora), lambda i: (0, 0)),
                  pl.BlockSpec((1, kv_lora), lambda i: (0, 0)), rope, rope],
        out_specs=[pl.BlockSpec((tm, q_lora), lambda i: (i, 0)), pl.BlockSpec((tm, kv_lora), lambda i: (i, 0)),
                   rope, rope],
        out_shape=[jax.ShapeDtypeStruct((m, q_lora), BF16), jax.ShapeDtypeStruct((m, kv_lora), F32),
                   jax.ShapeDtypeStruct((m, QK_ROPE), F32), jax.ShapeDtypeStruct((m, QK_ROPE), BF16)],
        compiler_params=_cp(1), name="mla_mid",
    )(dq, q_norm.reshape(1, -1), kv_norm.reshape(1, -1), cos64, sin64)


def _mla_q(cq, w_uq_ext, cos64, sin64, n_heads, tm):
    m = cq.shape[0]
    hw = QK_NOPE + 2 * QK_ROPE
    dqk = QK_NOPE + QK_ROPE
    scale = dqk ** -0.5

    def epi(accs, v):
        x = accs[0]
        rot = x[:, QK_NOPE:dqk] * v[0][...] + x[:, dqk:hw] * v[1][...]
        return [jnp.concatenate([x[:, :QK_NOPE], rot], axis=1) * scale]

    rope = lambda arr: (arr, (tm, QK_ROPE), lambda i, j: (i, 0))
    return _mm(cq, [w_uq_ext], tm=tm, tn=hw, gn=n_heads, epi=epi, vecs=[rope(cos64), rope(sin64)], name="mla_q",
               outs=[((n_heads, m, dqk), BF16, (1, tm, dqk), lambda i, j: (j, i, 0))])[0]


def _mla_kv(ckv, w_ukv, krope, n_heads, tm):
    m = ckv.shape[0]
    dqk = QK_NOPE + QK_ROPE

    def epi(accs, v):
        x = accs[0]
        return [jnp.concatenate([x[:, :QK_NOPE], v[0][...].astype(F32)], axis=1), x[:, QK_NOPE:]]

    return _mm(ckv, [w_ukv], tm=tm, tn=QK_NOPE + V_DIM, gn=n_heads, epi=epi, name="mla_kv",
               vecs=[(krope, (tm, QK_ROPE), lambda i, j: (i, 0))],
               outs=[((n_heads, m, dqk), BF16, (1, tm, dqk), lambda i, j: (j, i, 0)),
                     ((n_heads, m, V_DIM), BF16, (1, tm, V_DIM), lambda i, j: (j, i, 0))])


def _attn_context(q, k, v, lay):
    n_heads = q.shape[0]
    t = lay.ctx_len
    dqk = q.shape[2]

    def body(q_ref, k_ref, v_ref, o_ref):
        for h in range(n_heads):
            s = _dot_nt(q_ref[h], k_ref[h])
            p = jnp.exp(s - jnp.max(s, axis=-1, keepdims=True))
            l = jnp.sum(p, axis=-1, keepdims=True)
            o = _dot(p.astype(BF16), v_ref[h]) / l
            o_ref[:, h * V_DIM:(h + 1) * V_DIM] = o.astype(o_ref.dtype)

    return pl.pallas_call(
        body, grid=(lay.n_ctx_seq,),
        in_specs=[pl.BlockSpec((n_heads, t, dqk), lambda b: (0, b, 0)),
                  pl.BlockSpec((n_heads, t, dqk), lambda b: (0, b, 0)),
                  pl.BlockSpec((n_heads, t, V_DIM), lambda b: (0, b, 0))],
        out_specs=pl.BlockSpec((t, n_heads * V_DIM), lambda b: (b, 0)),
        out_shape=jax.ShapeDtypeStruct((lay.n_ctx, n_heads * V_DIM), BF16),
        compiler_params=_cp(1), name="attn_context",
    )(q, k, v)


def _attn_latent(q, k, v, kc, vc, lay, past_len):
    n_heads = q.shape[0]
    t = lay.lat_len
    dqk = q.shape[2]
    hp = 2 if n_heads % 2 == 0 else 1
    tq = _tile(256, t)
    lat0 = lay.n_ctx // t
    q0 = lay.n_ctx // tq
    kch = _tile(1024, t)
    n_ch = t // kch

    def body(q_ref, k_ref, v_ref, kc_ref, vc_ref, o_ref):
        def scores(h):
            qv = q_ref[h]
            return ([_dot_nt(qv, k_ref[h, j * kch:(j + 1) * kch, :]) for j in range(n_ch)]
                    + [_dot_nt(qv, kc_ref[h])])

        def row_max(s):
            return functools.reduce(jnp.maximum, [jnp.max(x, axis=-1, keepdims=True) for x in s])

        def probs(s, mx):
            p = [jnp.exp(x - mx) for x in s]
            l = functools.reduce(jnp.add, [jnp.sum(x, axis=-1, keepdims=True) for x in p])
            return [x.astype(BF16) for x in p], l

        def weighted(h, p):
            vs = [v_ref[h, j * kch:(j + 1) * kch, :] for j in range(n_ch)] + [vc_ref[h]]
            return functools.reduce(jnp.add, [_dot(x, vv) for x, vv in zip(p, vs)])

        s = [scores(h) for h in range(hp)]
        outs = []
        p_prev = None
        for h in range(hp):
            p, l = probs(s[h], row_max(s[h]))
            if p_prev is not None:
                outs.append(weighted(h - 1, p_prev[0]) / p_prev[1])
            p_prev = (p, l)
        outs.append(weighted(hp - 1, p_prev[0]) / p_prev[1])
        o_ref[...] = jnp.concatenate(outs, axis=1).astype(o_ref.dtype)

    return pl.pallas_call(
        body, grid=(lay.n_lat_seq, n_heads // hp, t // tq),
        in_specs=[pl.BlockSpec((hp, tq, dqk), lambda b, h, i: (h, q0 + b * (t // tq) + i, 0)),
                  pl.BlockSpec((hp, t, dqk), lambda b, h, i: (h, lat0 + b, 0)),
                  pl.BlockSpec((hp, t, V_DIM), lambda b, h, i: (h, lat0 + b, 0)),
                  pl.BlockSpec((hp, past_len, dqk), lambda b, h, i: (h, b, 0)),
                  pl.BlockSpec((hp, past_len, V_DIM), lambda b, h, i: (h, b, 0))],
        out_specs=pl.BlockSpec((tq, hp * V_DIM), lambda b, h, i: (b * (t // tq) + i, h)),
        out_shape=jax.ShapeDtypeStruct((lay.n_lat_seq * t, n_heads * V_DIM), BF16),
        compiler_params=_cp(3), name="attn_latent",
    )(q, k, v, kc, vc)


def _rope_tables(lay):
    n = lay.lat_len
    pairs = QK_ROPE // 4
    row_pos = (jnp.arange(n) // GRID_W).astype(F32)
    col_pos = (jnp.arange(n) % GRID_W).astype(F32)
    inv_freq = ROPE_THETA ** (-jnp.arange(pairs, dtype=F32) / pairs)
    ang = jnp.concatenate([row_pos[:, None] * inv_freq, col_pos[:, None] * inv_freq], axis=-1)
    cos, sin = jnp.cos(ang), jnp.sin(ang)
    cos64 = jnp.concatenate([cos, cos], axis=-1)
    sin64 = jnp.concatenate([-sin, sin], axis=-1)
    one = jnp.ones((lay.n_ctx, QK_ROPE), F32)
    cos_all = jnp.concatenate([one] + [cos64] * lay.n_lat_seq, axis=0)
    sin_all = jnp.concatenate([0.0 * one] + [sin64] * lay.n_lat_seq, axis=0)
    return cos_all, sin_all


def _mla_layer(x, h, mod3, lay, norm_ffn, cache_ckv, cache_krope, w_down, q_norm, kv_norm, w_uq, w_ukv, w_o):
    d = x.shape[1]
    q_lora, kv_lora = q_norm.shape[0], kv_norm.shape[0]
    n_heads = w_o.shape[0] // V_DIM
    dqk = QK_NOPE + QK_ROPE
    perm_d = np.concatenate([np.arange(0, QK_ROPE, 2), np.arange(1, QK_ROPE, 2)])
    perm_s = np.concatenate([np.arange(1, QK_ROPE, 2), np.arange(0, QK_ROPE, 2)])
    kr0 = q_lora + kv_lora
    w_down_ext = jnp.concatenate(
        [w_down[:, :kr0], w_down[:, kr0 + perm_d], w_down[:, kr0:], w_down[:, kr0 + perm_s],
         jnp.zeros((d, QK_ROPE), F32)], axis=1)
    w_uq_h = w_uq.reshape(q_lora, n_heads, dqk)
    w_uq_ext = jnp.concatenate([w_uq_h[:, :, :QK_NOPE], w_uq_h[:, :, QK_NOPE + perm_d],
                                w_uq_h[:, :, QK_NOPE + perm_s]], axis=2).reshape(q_lora, -1)
    cos64, sin64 = _rope_tables(lay)
    tm = lay.tile(1024)
    dq = _mm_act(h, w_down_ext, None, None, F32, tm=tm, tn=512, name="mla_down")
    cq, ckv, krope_raw, krope_rot = _mla_mid(dq, q_norm, kv_norm, cos64, sin64, q_lora, kv_lora, lay)
    q = _mla_q(cq, w_uq_ext, cos64, sin64, n_heads, lay.tile(2048))
    k, v = _mla_kv(ckv, w_ukv, krope_rot, n_heads, lay.tile(2048))
    n_past = cache_ckv.shape[0] * cache_ckv.shape[1]
    kc, vc = _mla_kv(cache_ckv.reshape(n_past, kv_lora), w_ukv,
                     cache_krope.reshape(n_past, QK_ROPE)[:, perm_d].astype(BF16), n_heads,
                     _tile(512, cache_ckv.shape[1]))
    o_ctx = _attn_context(q, k, v, lay)
    o_lat = _attn_latent(q, k, v, kc, vc, lay, cache_ckv.shape[1])
    attn = jnp.concatenate([o_ctx, o_lat], axis=0)
    x, h_ffn = _out_norm(attn, w_o, None, x, mod3, lay, norm_ffn, BF16, name="mla_out")
    ckv_ctx = ckv[:lay.n_ctx].reshape(lay.n_ctx_seq, lay.ctx_len, kv_lora)
    krope_ctx = krope_raw[:lay.n_ctx].reshape(lay.n_ctx_seq, lay.ctx_len, QK_ROPE)
    return x, h_ffn, ckv_ctx, krope_ctx


def _conv_dw(u, w_dw, b_dw, ln_g, ln_b, lay):
    m, d = u.shape
    width = w_dw.shape[0]
    pad = width // 2
    t = lay.tile(256)
    halo = 2 * SUBLANES
    assert pad <= halo
    rb = _tile(64, t)

    sh_rows = t + halo + SUBLANES

    def body(prev_ref, cur_ref, nxt_ref, w_ref, bdw_ref, g_ref, b_ref, o_ref, ext_ref, dw_ref, sh_ref):
        pos, ln = lay.seq_pos(pl.program_id(0) * t)
        ext_ref[0:halo, :] = jnp.where(pos > 0, prev_ref[...], 0.0)
        ext_ref[halo:halo + t, :] = cur_ref[...]
        ext_ref[halo + t:halo + t + halo, :] = jnp.where(pos + t < ln, nxt_ref[...], 0.0)

        def chunk(c, carry):
            ls = pl.ds(pl.multiple_of(c * LANES, LANES), LANES)
            for s in range(SUBLANES):
                sh_ref[s] = ext_ref[pl.ds(s, sh_rows), ls]
            for r in range(t // rb):
                acc = jnp.zeros((rb, LANES), F32)
                for j in range(width):
                    off = halo - pad + j
                    acc = acc + (sh_ref[off % SUBLANES, pl.ds(off // SUBLANES * SUBLANES + r * rb, rb), :]
                                 * w_ref[pl.ds(j, 1), ls])
                dw_ref[pl.ds(r * rb, rb), ls] = acc + bdw_ref[:, ls]
            return carry

        lax.fori_loop(0, d // LANES, chunk, 0)
        xv = dw_ref[...]
        mu = jnp.mean(xv, axis=-1, keepdims=True)
        xc = xv - mu
        var = jnp.mean(xc * xc, axis=-1, keepdims=True)
        o_ref[...] = _silu(xc * lax.rsqrt(var + LN_EPS) * g_ref[...] + b_ref[...]).astype(o_ref.dtype)

    hb = t // halo
    vec = pl.BlockSpec((1, d), lambda i: (0, 0))
    return pl.pallas_call(
        body, grid=(m // t,),
        in_specs=[pl.BlockSpec((halo, d), lambda i: (jnp.maximum(i * hb - 1, 0), 0)),
                  pl.BlockSpec((t, d), lambda i: (i, 0)),
                  pl.BlockSpec((halo, d), lambda i: (jnp.minimum((i + 1) * hb, m // halo - 1), 0)),
                  pl.BlockSpec((width, d), lambda i: (0, 0)), vec, vec, vec],
        out_specs=pl.BlockSpec((t, d), lambda i: (i, 0)),
        out_shape=jax.ShapeDtypeStruct((m, d), BF16),
        scratch_shapes=[pltpu.VMEM((t + 2 * halo, d), F32), pltpu.VMEM((t, d), F32),
                        pltpu.VMEM((SUBLANES, sh_rows, LANES), F32)],
        compiler_params=_cp(1), name="conv_dw",
    )(u, u, u, w_dw, b_dw.reshape(1, d), ln_g.reshape(1, d), ln_b.reshape(1, d))


def _conformer_layer(x, h, mod3, lay, norm_ffn, router, w_in, b_in, w_dw, b_dw, ln_g, ln_b, w_out, b_out):
    d = x.shape[1]
    tm = lay.tile(1024)
    u = _mm_dual(h, w_in, w_in, b_in, b_in, lambda a, b: a * jax.nn.sigmoid(b), F32, n=d, col2=d // 512,
                 tm=tm, tn=512, name="conv_in")
    cv = _conv_dw(u, w_dw, b_dw, ln_g, ln_b, lay)
    return _out_norm(cv, w_out, b_out, x, mod3, lay, norm_ffn, BF16, router=router, name="conv_out")


def _sgu_mix(z, ln_g, ln_b, w_s, b_s, lay):
    m, d2 = z.shape
    d = d2 // 2
    groups, chunk, _ = w_s.shape
    gd = d // groups
    tm = lay.tile(2 * chunk)
    bs_t = jnp.transpose(b_s)

    def body(u_ref, v_ref, g_ref, b_ref, ws_ref, bs_ref, o_ref):
        xv = v_ref[...]
        mu = jnp.mean(xv, axis=-1, keepdims=True)
        xc = xv - mu
        var = jnp.mean(xc * xc, axis=-1, keepdims=True)
        vn = (xc * lax.rsqrt(var + LN_EPS) * g_ref[...] + b_ref[...]).astype(BF16)
        for c in range(tm // chunk):
            rs = slice(c * chunk, (c + 1) * chunk)
            for g in range(groups):
                ls = slice(g * gd, (g + 1) * gd)
                mixed = _dot(ws_ref[g].astype(BF16), vn[rs, ls]) + bs_ref[:, g:g + 1]
                o_ref[rs, ls] = (u_ref[rs, ls] * mixed).astype(o_ref.dtype)

    vec = pl.BlockSpec((1, d), lambda i: (0, 0))
    return pl.pallas_call(
        body, grid=(m // tm,),
        in_specs=[pl.BlockSpec((tm, d), lambda i: (i, 0)), pl.BlockSpec((tm, d), lambda i: (i, 1)), vec, vec,
                  pl.BlockSpec((groups, chunk, chunk), lambda i: (0, 0, 0)),
                  pl.BlockSpec((chunk, groups), lambda i: (0, 0))],
        out_specs=pl.BlockSpec((tm, d), lambda i: (i, 0)),
        out_shape=jax.ShapeDtypeStruct((m, d), BF16), compiler_params=_cp(1), name="sgu_mix",
    )(z, z, ln_g.reshape(1, d), ln_b.reshape(1, d), w_s, bs_t)


def _sgu_layer(x, h, mod3, lay, norm_ffn, w_in, b_in, ln_g, ln_b, w_s, b_s, w_out, b_out):
    tm = lay.tile(1024)
    d = x.shape[1]
    z = _mm_act(h, w_in, b_in, _gelu_tanh, F32, tm=lay.tile(512), tn=_tile(1024, d), name="sgu_in", n_outer=True)
    sm = _sgu_mix(z, ln_g, ln_b, w_s, b_s, lay)
    return _out_norm(sm, w_out, b_out, x, mod3, lay, norm_ffn, BF16, name="sgu_out")


def _shift_mix(h, mu, lay):
    m, d = h.shape
    t = lay.tile(256)
    halo = SUBLANES

    def body(prev_ref, cur_ref, nxt_ref, mu_ref, *rest):
        o_refs, ext_ref = rest[:6], rest[6]
        pos, ln = lay.seq_pos(pl.program_id(0) * t)
        ext_ref[0:halo, :] = jnp.where(pos > 0, prev_ref[...], 0.0)
        ext_ref[halo:halo + t, :] = cur_ref[...]
        ext_ref[halo + t:halo + t + halo, :] = jnp.where(pos + t < ln, nxt_ref[...], 0.0)
        xv = cur_ref[...]
        xx = 0.5 * (ext_ref[pl.ds(halo - 1, t), :] + ext_ref[pl.ds(halo + 1, t), :]) - xv
        for j in range(6):
            o_refs[j][...] = (xv + xx * mu_ref[j:j + 1, :]).astype(BF16)

    hb = t // halo
    row = pl.BlockSpec((t, d), lambda i: (i, 0))
    return pl.pallas_call(
        body, grid=(m // t,),
        in_specs=[pl.BlockSpec((halo, d), lambda i: (jnp.maximum(i * hb - 1, 0), 0)), row,
                  pl.BlockSpec((halo, d), lambda i: (jnp.minimum((i + 1) * hb, m // halo - 1), 0)),
                  pl.BlockSpec((6, d), lambda i: (0, 0))],
        out_specs=[row] * 6, out_shape=[jax.ShapeDtypeStruct((m, d), BF16)] * 6,
        scratch_shapes=[pltpu.VMEM((t + 2 * halo, d), F32)],
        compiler_params=_cp(1), name="rwkv_shift_mix",
    )(h, h, h, mu)


def _wkv_prep(r, k, v, wl, al, k_k, k_a, r_k):
    m, d = r.shape
    tm = _tile(128, m)

    def body(r_ref, k_ref, v_ref, wl0_ref, wl1_ref, al0_ref, al1_ref, kk_ref, ka_ref, rk_ref,
             okk_ref, olw0_ref, olw1_ref, ob0_ref, ob1_ref, okt0_ref, okt1_ref, obon_ref):
        rv, kv, vv = r_ref[...], k_ref[...], v_ref[...]
        kk = kv * kk_ref[...]
        kk = kk / jnp.maximum(jnp.sqrt(_head_sum(kk * kk)), 1e-12)
        okk_ref[...] = kk.astype(okk_ref.dtype)
        bonus = jnp.zeros_like(rv)
        for wl_ref, al_ref, olw_ref, ob_ref, okt_ref in ((wl0_ref, al0_ref, olw0_ref, ob0_ref, okt0_ref),
                                                          (wl1_ref, al1_ref, olw1_ref, ob1_ref, okt1_ref)):
            w_log = -_softplus(-wl_ref[...]) - 0.5
            olw_ref[...] = -jnp.exp(w_log)
            a = jax.nn.sigmoid(al_ref[...])
            kt = kv * (1.0 + (a - 1.0) * ka_ref[...])
            ob_ref[...] = (kk * a).astype(ob_ref.dtype)
            okt_ref[...] = kt.astype(okt_ref.dtype)
            bonus = bonus + _head_sum(rv * kt * rk_ref[...]) * vv
        obon_ref[...] = bonus

    row = pl.BlockSpec((tm, d), lambda i: (i, 0))
    vec = pl.BlockSpec((1, d), lambda i: (0, 0))
    f32 = jax.ShapeDtypeStruct((m, d), F32)
    bf16 = jax.ShapeDtypeStruct((m, d), BF16)
    kk, lw0, lw1, b0, b1, kt0, kt1, bonus = pl.pallas_call(
        body, grid=(m // tm,),
        in_specs=[row] * 7 + [vec] * 3,
        out_specs=[row] * 8,
        out_shape=[bf16, f32, f32, bf16, bf16, bf16, bf16, f32],
        compiler_params=_cp(1), name="wkv_prep",
    )(r, k, v, wl[0], wl[1], al[0], al[1], k_k.reshape(1, d), k_a.reshape(1, d), r_k.reshape(1, d))
    return kk, (lw0, lw1), (b0, b1), (kt0, kt1), bonus


def _wkv_scan_dir(r, kk, v, lw, b, kt, s_init, lay, reverse):
    m, d = r.shape
    c = WKV_CHUNK
    rows = c * WKV_CHUNKS_PER_STEP
    n_pairs = d // LANES
    pps = math.gcd(WKV_PAIRS_PER_STEP, n_pairs)
    pw = pps * LANES
    n_steps = m // rows
    ctx_steps = lay.n_ctx // rows
    sps_ctx, sps_lat = lay.ctx_len // rows, lay.lat_len // rows
    n_seq = lay.n_ctx_seq + lay.n_lat_seq
    c2 = 2 * c
    sgn = -1 if reverse else 1

    def block_of(s):
        return n_steps - 1 - s if reverse else s

    def seq_of(tb):
        is_ctx = tb < ctx_steps
        seq = jnp.where(is_ctx, tb // sps_ctx, lay.n_ctx_seq + (tb - ctx_steps) // sps_lat)
        pos = jnp.where(is_ctx, tb % sps_ctx, (tb - ctx_steps) % sps_lat)
        return is_ctx, seq, pos, jnp.where(is_ctx, sps_ctx, sps_lat)

    def body(r_ref, kk_ref, v_ref, lw_ref, b_ref, kt_ref, si_ref, y_ref, so_ref, s_scr):
        is_ctx, _, pos, sps = seq_of(block_of(pl.program_id(1)))
        first = pos == (sps - 1 if reverse else 0)
        last = pos == (0 if reverse else sps - 1)

        @pl.when(first)
        def _():
            s_scr[...] = jnp.where(is_ctx, 0.0, si_ref[0])

        ri = lax.broadcasted_iota(jnp.int32, (c2, c2), 0)
        ci = lax.broadcasted_iota(jnp.int32, (c2, c2), 1)
        same = (ri // c) == (ci // c)
        strict = same & ((ri - ci) * sgn > 0)
        incl = same & ((ri - ci) * sgn >= 0)
        eye = jnp.where(ri == ci, 1.0, 0.0)
        end_row = 0 if reverse else c - 1
        step_row = lax.broadcasted_iota(jnp.int32, (c, LANES), 0)

        def running_sum(x):
            s = 1
            while s < c:
                if reverse:
                    x = x + jnp.where(step_row < c - s, pltpu.roll(x, c - s, axis=0), 0.0)
                else:
                    x = x + jnp.where(step_row >= s, pltpu.roll(x, s, axis=0), 0.0)
                s *= 2
            return x
        own = (lax.broadcasted_iota(jnp.int32, (c2, LANES), 0) // c
               == lax.broadcasted_iota(jnp.int32, (c2, LANES), 1) // RWKV_N)

        def stack(xv):
            return jnp.where(own, jnp.concatenate([xv, xv], axis=0), 0.0).astype(BF16)

        chunk_order = list(range(WKV_CHUNKS_PER_STEP))[::sgn]
        probs = [(j, p) for j in chunk_order for p in range(pps)]
        at = lambda ref, j, p: ref[j * c:(j + 1) * c, p * LANES:(p + 1) * LANES].astype(F32)

        lwv = [at(lw_ref, j, p) for j, p in probs]
        cs = [running_sum(x) for x in lwv]
        g = [jnp.exp(x) for x in cs]
        ginv = [jnp.exp(-x) for x in cs]
        gprev = [jnp.exp(x - w) for x, w in zip(cs, lwv)]
        gend = [jnp.exp(x[end_row:end_row + 1] - x) for x in cs]
        gall = [jnp.exp(x[end_row:end_row + 1]) for x in cs]
        kkv = [at(kk_ref, j, p) for j, p in probs]
        bv = [at(b_ref, j, p) for j, p in probs]
        ktv = [at(kt_ref, j, p) for j, p in probs]
        n = len(probs)
        xa = [stack(-kkv[i] * gprev[i]) for i in range(n)]
        xr = [stack(at(r_ref, *probs[i]) * g[i]) for i in range(n)]
        xb = [stack(bv[i] * ginv[i]) for i in range(n)]
        xk = [stack(ktv[i] * ginv[i]) for i in range(n)]
        xb_end = [stack(bv[i] * gend[i]) for i in range(n)]
        xk_end = [stack(ktv[i] * gend[i]) for i in range(n)]
        xv = [stack(at(v_ref, *probs[i])) for i in range(n)]
        mm = [_dot_nt(jnp.concatenate([xa[i], xr[i]], axis=0), jnp.concatenate([xb[i], xk[i]], axis=0))
              for i in range(n)]
        m_ab = [jnp.where(strict, x[:c2, :c2], 0.0) for x in mm]
        m_ak = [jnp.where(strict, x[:c2, c2:], 0.0).astype(BF16) for x in mm]
        m_rb = [jnp.where(incl, x[c2:, :c2], 0.0).astype(BF16) for x in mm]
        m_rk = [jnp.where(incl, x[c2:, c2:], 0.0).astype(BF16) for x in mm]
        akv = [_dot(m_ak[i], xv[i]).astype(BF16) for i in range(n)]
        pj = [x.astype(BF16) for x in m_ab]
        tinv = [eye + x for x in m_ab]
        pj = [_dot(x, x).astype(BF16) for x in pj]
        for _ in range(int(math.log2(c)) - 2):
            both = [_dot(jnp.concatenate([pj[i], tinv[i].astype(BF16)], axis=0), pj[i]) for i in range(n)]
            pj = [x[:c2].astype(BF16) for x in both]
            tinv = [tinv[i] + both[i][c2:] for i in range(n)]
        tinv = [(tinv[i] + _dot(tinv[i].astype(BF16), pj[i])).astype(BF16) for i in range(n)]
        pq = [_dot(tinv[i], jnp.concatenate([xa[i], akv[i]], axis=1)).astype(BF16) for i in range(n)]
        ab = [_dot_tn(pq[i], xb_end[i]) for i in range(n)]
        a_c = [(eye * gall[i] + ab[i][:LANES]).astype(BF16) for i in range(n)]
        b_c = [ab[i][LANES:] + _dot_tn(xv[i], xk_end[i]) for i in range(n)]
        cd = [_dot(m_rb[i], pq[i]) for i in range(n)]
        c_c = [(xr[i].astype(F32) + cd[i][:, :LANES]).astype(BF16) for i in range(n)]
        d_c = [cd[i][:, LANES:] + _dot(m_rk[i], xv[i]) for i in range(n)]
        for p in range(pps):
            s = s_scr[p]
            for i, (j, pp) in enumerate(probs):
                if pp != p:
                    continue
                s_hi, s_lo = _split_bf16(s, 2)
                yy = _dot_nt(c_c[i], s_hi) + _dot_nt(c_c[i], s_lo) + d_c[i]
                y_ref[j * c:(j + 1) * c, p * LANES:(p + 1) * LANES] = yy[:c] + yy[c:]
                s = _dot(s_hi, a_c[i]) + _dot(s_lo, a_c[i]) + b_c[i]
            s_scr[p] = s

        @pl.when(last)
        def _():
            so_ref[0] = s_scr[...]

    tok = pl.BlockSpec((rows, pw), lambda lb, s: (block_of(s), lb))
    st = (1, pps, LANES, LANES)
    return pl.pallas_call(
        body, grid=(n_pairs // pps, n_steps),
        in_specs=[tok] * 6 + [pl.BlockSpec(st, lambda lb, s: (
            jnp.clip(seq_of(block_of(s))[1] - lay.n_ctx_seq, 0, lay.n_lat_seq - 1), lb, 0, 0))],
        out_specs=[tok, pl.BlockSpec(st, lambda lb, s: (seq_of(block_of(s))[1], lb, 0, 0))],
        out_shape=[jax.ShapeDtypeStruct((m, d), F32), jax.ShapeDtypeStruct((n_seq, n_pairs, LANES, LANES), F32)],
        scratch_shapes=[pltpu.VMEM((pps, LANES, LANES), F32)],
        compiler_params=_cp(2, arbitrary_last=True), name="wkv_scan_bwd" if reverse else "wkv_scan_fwd",
    )(r, kk, v, lw, b, kt, s_init)


def _wkv_post(y_fwd, y_bwd, bonus, g, ln_g, ln_b):
    m, d = bonus.shape
    tm = _tile(128, m)

    def body(yf_ref, yb_ref, bon_ref, g_ref, lg_ref, lb_ref, o_ref):
        yv = yf_ref[...] + yb_ref[...]
        mu = _head_sum(yv) * (1.0 / RWKV_N)
        yc = yv - mu
        var = _head_sum(yc * yc) * (1.0 / RWKV_N)
        o = yc * lax.rsqrt(var + GN_EPS) * lg_ref[...] + lb_ref[...] + bon_ref[...]
        o_ref[...] = (o * g_ref[...]).astype(o_ref.dtype)

    row = pl.BlockSpec((tm, d), lambda i: (i, 0))
    vec = pl.BlockSpec((1, d), lambda i: (0, 0))
    return pl.pallas_call(
        body, grid=(m // tm,),
        in_specs=[row, row, row, row, vec, vec],
        out_specs=row, out_shape=jax.ShapeDtypeStruct((m, d), BF16), compiler_params=_cp(1), name="wkv_post",
    )(y_fwd, y_bwd, bonus, g, ln_g.reshape(1, d), ln_b.reshape(1, d))


def _pad_cols(w, n):
    return jnp.pad(w, ((0, 0), (0, n - w.shape[1])))


def _pad_rows(w, n):
    return jnp.pad(w, ((0, n - w.shape[0]), (0, 0)))


def _rwkv_layer(x, hf, mod3, lay, norm_ffn, router, s_fwd, s_bwd, mu, w_r, w_k, w_v, w_o, w0, w1, w2, a0, a1, a2,
                g1, g2, k_k, k_a, r_k, ln_g, ln_b):
    m, d = x.shape
    tm = lay.tile(1024)
    xr, xw, xk, xv, xa, xg = _shift_mix(hf, mu, lay)
    sq = dict(tm=lay.tile(512), tn=_tile(1024, d), n_outer=True)
    r = _mm_act(xr, w_r, None, None, F32, name="rwkv_r", **sq)
    k = _mm_act(xk, w_k, None, None, F32, name="rwkv_k", **sq)
    v = _mm_act(xv, w_v, None, None, F32, name="rwkv_v", **sq)
    gh = _mm_act(xg, g1, None, jax.nn.sigmoid, BF16, tm=tm, tn=g1.shape[1], name="rwkv_g1")
    g = _mm_act(gh, g2, None, None, F32, name="rwkv_g2", **sq)
    both = lambda w: jnp.concatenate([_pad_cols(w[0], LANES), _pad_cols(w[1], LANES)], axis=1)
    th = _mm_act(xw, both(w1), None, jnp.tanh, BF16, tm=tm, tn=2 * LANES, name="rwkv_w1")
    ah = _mm_act(xa, both(a1), None, None, BF16, tm=tm, tn=2 * LANES, name="rwkv_a1")
    wl, al = [], []
    for dr in range(2):
        cols = slice(dr * LANES, (dr + 1) * LANES)
        wl.append(_mm_act(th[:, cols], _pad_rows(w2[dr], LANES), w0[dr], None, F32, name="rwkv_w2", **sq))
        al.append(_mm_act(ah[:, cols], _pad_rows(a2[dr], LANES), a0[dr], None, F32, name="rwkv_a2", **sq))
    kk, lw, b, kt, bonus = _wkv_prep(r, k, v, wl, al, k_k, k_a, r_k.reshape(-1))
    n_pairs = d // LANES
    ys, finals = [], []
    for dr, s0 in enumerate((s_fwd, s_bwd)):
        st = s0.astype(F32).reshape(-1, n_pairs, 2, RWKV_N, RWKV_N)
        s_init = jnp.zeros((st.shape[0], n_pairs, LANES, LANES), F32)
        s_init = s_init.at[..., :RWKV_N, :RWKV_N].set(st[:, :, 0]).at[..., RWKV_N:, RWKV_N:].set(st[:, :, 1])
        y, s_fin = _wkv_scan_dir(r, kk, v, lw[dr], b[dr], kt[dr], s_init, lay, reverse=dr == 1)
        sf = s_fin[:lay.n_ctx_seq]
        s_pair = jnp.stack([sf[..., :RWKV_N, :RWKV_N], sf[..., RWKV_N:, RWKV_N:]], axis=2)
        ys.append(y)
        finals.append(s_pair.reshape(lay.n_ctx_seq, 2 * n_pairs, RWKV_N, RWKV_N))
    og = _wkv_post(ys[0], ys[1], bonus, g, ln_g, ln_b)
    x, h_ffn, top_e, top_p = _out_norm(og, w_o, None, x, mod3, lay, norm_ffn, BF16, router=router, name="rwkv_out")
    return x, h_ffn, top_e, top_p, finals[0], finals[1]


def kernel(x_prompt, x_sample, cache_ckv_l0, cache_krope_l0, state_wkv_fwd_l3, state_wkv_bwd_l3, c, c_ctx, l0_w_mod, l0_b_mod, l0_norm_mix, l0_norm_ffn, l0_mla_w_down, l0_mla_q_norm, l0_mla_kv_norm, l0_mla_w_uq, l0_mla_w_ukv, l0_mla_w_o, l0_ffn_w_gate, l0_ffn_w_up, l0_ffn_w_down, l1_w_mod, l1_b_mod, l1_norm_mix, l1_norm_ffn, l1_conv_w_in, l1_conv_b_in, l1_conv_w_dw, l1_conv_b_dw, l1_conv_ln_g, l1_conv_ln_b, l1_conv_w_out, l1_conv_b_out, l1_moe_w_router, l1_moe_b_router, l1_moe_w_gate, l1_moe_w_up, l1_moe_w_down, l2_w_mod, l2_b_mod, l2_norm_mix, l2_norm_ffn, l2_sgu_w_in, l2_sgu_b_in, l2_sgu_ln_g, l2_sgu_ln_b, l2_sgu_w_s, l2_sgu_b_s, l2_sgu_w_out, l2_sgu_b_out, l2_ffn_w_gate, l2_ffn_w_up, l2_ffn_w_down, l3_w_mod, l3_b_mod, l3_norm_mix, l3_norm_ffn, l3_rwkv_mu, l3_rwkv_w_r, l3_rwkv_w_k, l3_rwkv_w_v, l3_rwkv_w_o, l3_rwkv_w0, l3_rwkv_w1, l3_rwkv_w2, l3_rwkv_a0, l3_rwkv_a1, l3_rwkv_a2, l3_rwkv_g1, l3_rwkv_g2, l3_rwkv_k_k, l3_rwkv_k_a, l3_rwkv_r_k, l3_rwkv_ln_g, l3_rwkv_ln_b, l3_moe_w_router, l3_moe_b_router, l3_moe_w_gate, l3_moe_w_up, l3_moe_w_down, norm_out):
    n_ctx_seq, ctx_len, d = x_prompt.shape
    n_lat_seq, lat_len, _ = x_sample.shape
    lay = _Lay(n_ctx_seq, ctx_len, n_lat_seq, lat_len)
    assert n_lat_seq + 1 <= SUBLANES
    x = jnp.concatenate([x_prompt.reshape(-1, d), x_sample.reshape(-1, d)], axis=0)
    cond8 = jnp.zeros((SUBLANES, d), F32).at[0].set(c_ctx).at[1:1 + n_lat_seq].set(c)

    mods = [_ada_mod(cond8, w, b) for w, b in ((l0_w_mod, l0_b_mod), (l1_w_mod, l1_b_mod), (l2_w_mod, l2_b_mod),
                                               (l3_w_mod, l3_b_mod))]

    h = _norm_mod(x, l0_norm_mix, mods[0], (0, 1), lay, BF16)
    x, h, ckv_l0, krope_l0 = _mla_layer(x, h, mods[0], lay, l0_norm_ffn, cache_ckv_l0, cache_krope_l0, l0_mla_w_down,
                                        l0_mla_q_norm, l0_mla_kv_norm, l0_mla_w_uq, l0_mla_w_ukv, l0_mla_w_o)
    x = _dense_ffn(x, h, l0_ffn_w_gate, l0_ffn_w_up, l0_ffn_w_down, mods[0], lay)

    h = _norm_mod(x, l1_norm_mix, mods[1], (0, 1), lay, BF16)
    router = _router_ops(l1_moe_w_router, l1_moe_b_router)
    x, h, top_e, top_p = _conformer_layer(x, h, mods[1], lay, l1_norm_ffn, router, l1_conv_w_in, l1_conv_b_in,
                                          l1_conv_w_dw, l1_conv_b_dw, l1_conv_ln_g, l1_conv_ln_b, l1_conv_w_out,
                                          l1_conv_b_out)
    x, h = _moe_ffn(x, h, top_e, top_p, mods[1], lay, router[3], l1_moe_w_gate, l1_moe_w_up, l1_moe_w_down,
                    (l2_norm_mix, mods[2], (0, 1), BF16, True))

    x, h = _sgu_layer(x, h, mods[2], lay, l2_norm_ffn, l2_sgu_w_in, l2_sgu_b_in, l2_sgu_ln_g, l2_sgu_ln_b,
                      l2_sgu_w_s, l2_sgu_b_s, l2_sgu_w_out, l2_sgu_b_out)
    x = _dense_ffn(x, h, l2_ffn_w_gate, l2_ffn_w_up, l2_ffn_w_down, mods[2], lay)

    hf = _norm_mod(x, l3_norm_mix, mods[3], (0, 1), lay, F32)
    router = _router_ops(l3_moe_w_router, l3_moe_b_router)
    x, h, top_e, top_p, wkv_fwd, wkv_bwd = _rwkv_layer(
        x, hf, mods[3], lay, l3_norm_ffn, router, state_wkv_fwd_l3, state_wkv_bwd_l3, l3_rwkv_mu, l3_rwkv_w_r,
        l3_rwkv_w_k, l3_rwkv_w_v, l3_rwkv_w_o, l3_rwkv_w0, l3_rwkv_w1, l3_rwkv_w2, l3_rwkv_a0, l3_rwkv_a1,
        l3_rwkv_a2, l3_rwkv_g1, l3_rwkv_g2, l3_rwkv_k_k, l3_rwkv_k_a, l3_rwkv_r_k, l3_rwkv_ln_g, l3_rwkv_ln_b)
    y = _moe_ffn(x, h, top_e, top_p, mods[3], lay, router[3], l3_moe_w_gate, l3_moe_w_up, l3_moe_w_down,
                 (norm_out, None, None, F32, False))
    y_prompt = y[:lay.n_ctx].reshape(n_ctx_seq, ctx_len, d)
    y_sample = y[lay.n_ctx:].reshape(n_lat_seq, lat_len, d)
    return (y_prompt, y_sample, ckv_l0, krope_l0, wkv_fwd, wkv_bwd)
```

```python
import functools
import math

import jax
import jax.numpy as jnp
import numpy as np
from jax import lax
from jax.experimental import pallas as pl
from jax.experimental.pallas import tpu as pltpu

F32 = jnp.float32
BF16 = jnp.bfloat16

NORM_EPS = 1e-6
LN_EPS = 1e-5
GN_EPS = 64e-5
QK_NOPE = 128
QK_ROPE = 64
V_DIM = 128
ROPE_THETA = 10000.0
GRID_W = 64
RWKV_N = 64
N_MOD = 6
TOP_K = 2

LANES = 128
SUBLANES = 8
VMEM_LIMIT_BYTES = 56 * 1024 * 1024

WKV_CHUNK = 64
WKV_PAIRS_PER_STEP = 4
WKV_CHUNKS_PER_STEP = 4
MOE_ROWS = 512


def _cp(n_axes, arbitrary_last=False):
    sem = ["parallel"] * n_axes
    if arbitrary_last:
        sem[-1] = "arbitrary"
    return pltpu.CompilerParams(dimension_semantics=tuple(sem), vmem_limit_bytes=VMEM_LIMIT_BYTES)


def _tile(pref, *sizes):
    t = pref
    while any(s % t for s in sizes):
        t //= 2
    assert t >= SUBLANES
    return t


class _Lay:
    def __init__(self, n_ctx_seq, ctx_len, n_lat_seq, lat_len):
        self.n_ctx_seq, self.ctx_len, self.n_lat_seq, self.lat_len = n_ctx_seq, ctx_len, n_lat_seq, lat_len
        self.n_ctx = n_ctx_seq * ctx_len
        self.n_tok = self.n_ctx + n_lat_seq * lat_len

    def tile(self, pref):
        return _tile(pref, self.n_ctx, self.lat_len)

    def group(self, start):
        return jnp.where(start < self.n_ctx, 0, 1 + (start - self.n_ctx) // self.lat_len)

    def seq_pos(self, start):
        is_ctx = start < self.n_ctx
        pos = jnp.where(is_ctx, start % self.ctx_len, (start - self.n_ctx) % self.lat_len)
        return pos, jnp.where(is_ctx, self.ctx_len, self.lat_len)


def _silu(x):
    return x * jax.nn.sigmoid(x)


def _gelu_tanh(x):
    return 0.5 * x * (1.0 + jnp.tanh(math.sqrt(2.0 / math.pi) * (x + 0.044715 * (x * x * x))))


def _softplus(x):
    return jnp.maximum(x, 0.0) + jnp.log(1.0 + jnp.exp(-jnp.abs(x)))


def _split_bf16(x, n):
    parts = []
    r = x
    for _ in range(n):
        h = r.astype(BF16)
        parts.append(h)
        r = r - h.astype(F32)
    return parts


def _dot(a, b):
    return jnp.dot(a, b, preferred_element_type=F32)


def _dot_nt(a, b):
    return lax.dot_general(a, b, (((1,), (1,)), ((), ())), preferred_element_type=F32)


def _dot_tn(a, b):
    return lax.dot_general(a, b, (((0,), (0,)), ((), ())), preferred_element_type=F32)


def _pair_ones():
    r = lax.broadcasted_iota(jnp.int32, (LANES, LANES), 0) // RWKV_N
    c = lax.broadcasted_iota(jnp.int32, (LANES, LANES), 1) // RWKV_N
    return jnp.where(r == c, 1.0, 0.0).astype(BF16)


def _head_sum(x):
    ones = _pair_ones()
    cols = []
    for t in range(x.shape[1] // LANES):
        xt = x[:, t * LANES:(t + 1) * LANES]
        cols.append(sum(_dot(p, ones) for p in _split_bf16(xt, 3)))
    return jnp.concatenate(cols, axis=1)


def _mm(a, ws, *, tm, tn, gn, epi, outs, vecs=(), wcols=None, n_outer=False, name="mm"):
    m, k = a.shape
    gm = m // tm
    nw, nv = len(ws), len(vecs)
    wcols = wcols or [lambda j: j] * nw
    if n_outer:
        grid = (gn, gm)
        wrap = lambda fn: (lambda g0, g1: fn(g1, g0))
    else:
        grid = (gm, gn)
        wrap = lambda fn: (lambda g0, g1: fn(g0, g1))
    in_specs = [pl.BlockSpec((tm, k), wrap(lambda i, j: (i, 0)))]
    for wc in wcols:
        in_specs.append(pl.BlockSpec((k, tn), wrap(lambda i, j, wc=wc: (0, wc(j)))))
    for _, bshape, fn in vecs:
        in_specs.append(pl.BlockSpec(bshape, wrap(fn)))
    out_shape = [jax.ShapeDtypeStruct(s, d) for s, d, _, _ in outs]
    out_specs = [pl.BlockSpec(b, wrap(fn)) for _, _, b, fn in outs]

    def body(*refs):
        a_ref = refs[0]
        w_refs = refs[1:1 + nw]
        v_refs = refs[1 + nw:1 + nw + nv]
        o_refs = refs[1 + nw + nv:]
        av = a_ref[...]
        if av.dtype != BF16:
            av = av.astype(BF16)
        accs = [_dot(av, w[...].astype(BF16)) for w in w_refs]
        for o, r in zip(o_refs, epi(accs, v_refs)):
            o[...] = r.reshape(o.shape).astype(o.dtype)

    res = pl.pallas_call(body, grid=grid, in_specs=in_specs, out_specs=out_specs, out_shape=out_shape,
                         compiler_params=_cp(2), name=name)(a, *ws, *[v[0] for v in vecs])
    return res


def _bias_vec(b, tn):
    return (b.reshape(1, -1), (1, tn), lambda i, j: (0, j))


def _mm_act(a, w, bias, act, out_dtype, *, tm, tn, name, n_outer=False):
    m, n = a.shape[0], w.shape[1]
    vecs = [] if bias is None else [_bias_vec(bias, tn)]

    def epi(accs, v):
        x = accs[0]
        if bias is not None:
            x = x + v[0][...]
        return [act(x) if act is not None else x]

    return _mm(a, [w], tm=tm, tn=tn, gn=pl.cdiv(n, tn), epi=epi, vecs=vecs, n_outer=n_outer, name=name,
               outs=[((m, n), out_dtype, (tm, tn), lambda i, j: (i, j))])[0]


def _mm_dual(a, w1, w2, b1, b2, fn, out_dtype, *, n, col2, tm, tn, name, n_outer=True):
    m = a.shape[0]
    gn = pl.cdiv(n, tn)
    vecs = []
    if b1 is not None:
        vecs = [(b1.reshape(1, -1), (1, tn), lambda i, j: (0, j)),
                (b2.reshape(1, -1), (1, tn), lambda i, j: (0, j + col2))]

    def epi(accs, v):
        x, y = accs
        if b1 is not None:
            x, y = x + v[0][...], y + v[1][...]
        return [fn(x, y)]

    return _mm(a, [w1, w2], tm=tm, tn=tn, gn=gn, epi=epi, vecs=vecs, n_outer=n_outer, name=name,
               wcols=[lambda j: j, lambda j: j + col2],
               outs=[((m, n), out_dtype, (tm, tn), lambda i, j: (i, j))])[0]


def _mm_resid(a, w, bias, resid, mod3, gate_row, lay, *, tm, tn, name, n_outer=False):
    m, n = resid.shape
    vecs = [(mod3, (1, N_MOD, tn), lambda i, j: (lay.group(i * tm), 0, j)),
            (resid, (tm, tn), lambda i, j: (i, j))]
    if bias is not None:
        vecs.append(_bias_vec(bias, tn))

    def epi(accs, v):
        x = accs[0]
        if bias is not None:
            x = x + v[2][...]
        return [v[1][...] + v[0][0, gate_row:gate_row + 1, :] * x]

    return _mm(a, [w], tm=tm, tn=tn, gn=n // tn, epi=epi, vecs=vecs, n_outer=n_outer, name=name,
               outs=[((m, n), F32, (tm, tn), lambda i, j: (i, j))])[0]


def _ada_mod(cond8, w_mod, b_mod):
    d, n = w_mod.shape
    tn = _tile(1536, n)

    def body(c_ref, w_ref, b_ref, o_ref):
        c = c_ref[...]
        o_ref[...] = _dot(_silu(c).astype(BF16), w_ref[...].astype(BF16)) + b_ref[...]

    m = pl.pallas_call(
        body, grid=(n // tn,),
        in_specs=[pl.BlockSpec((SUBLANES, d), lambda j: (0, 0)), pl.BlockSpec((d, tn), lambda j: (0, j)),
                  pl.BlockSpec((1, tn), lambda j: (0, j))],
        out_specs=pl.BlockSpec((SUBLANES, tn), lambda j: (0, j)),
        out_shape=jax.ShapeDtypeStruct((SUBLANES, n), F32), compiler_params=_cp(1), name="ada_mod",
    )(cond8, w_mod, b_mod.reshape(1, n))
    return m.reshape(SUBLANES, N_MOD, d)


def _norm_rows(pre_fn, pre_ops, gain, mod3, rows, lay, out_dtype, *, m, d, emit_x=False, router=None,
               name="norm_mod"):
    tm = lay.tile(256)
    n_e = None if router is None else router[3]
    n_pre = len(pre_ops)

    def body(*refs):
        g_ref = refs[n_pre]
        pos = n_pre + 1
        xf = pre_fn(*refs[:n_pre])
        y = xf * lax.rsqrt(jnp.mean(xf * xf, axis=-1, keepdims=True) + NORM_EPS) * g_ref[...]
        if rows is not None:
            mod_ref = refs[pos]
            pos += 1
            y = y * (1.0 + mod_ref[0, rows[1]:rows[1] + 1, :]) + mod_ref[0, rows[0]:rows[0] + 1, :]
        if router is not None:
            whi_ref, wlo_ref, rb_ref = refs[pos:pos + 3]
            pos += 3
        if emit_x:
            refs[pos][...] = xf
            pos += 1
        o_ref = refs[pos]
        o_ref[...] = y.astype(o_ref.dtype)
        if router is not None:
            e_ref, p_ref = refs[pos + 1], refs[pos + 2]
            y_hi, y_lo = _split_bf16(y, 2)
            logits = _dot(y_hi, whi_ref[...]) + _dot(y_lo, whi_ref[...]) + _dot(y_hi, wlo_ref[...]) + rb_ref[...]
            lane = lax.broadcasted_iota(jnp.int32, logits.shape, 1)
            logits = jnp.where(lane < n_e, logits, -jnp.inf)
            m1 = jnp.max(logits, axis=-1, keepdims=True)
            i1 = jnp.min(jnp.where(logits == m1, lane, LANES), axis=-1, keepdims=True)
            rest = jnp.where(lane == i1, -jnp.inf, logits)
            m2 = jnp.max(rest, axis=-1, keepdims=True)
            i2 = jnp.min(jnp.where(rest == m2, lane, LANES), axis=-1, keepdims=True)
            e2 = jnp.exp(m2 - m1)
            p1 = 1.0 / (1.0 + e2)
            e_ref[...] = jnp.where(lane == 0, i1, jnp.where(lane == 1, i2, 0))
            p_ref[...] = jnp.where(lane == 0, p1, jnp.where(lane == 1, e2 * p1, 0.0))

    in_specs = [pl.BlockSpec(bshape, fn) for _, bshape, fn in pre_ops] + [pl.BlockSpec((1, d), lambda i: (0, 0))]
    args = [op[0] for op in pre_ops] + [gain.reshape(1, d)]
    if rows is not None:
        in_specs.append(pl.BlockSpec((1, N_MOD, d), lambda i: (lay.group(i * tm), 0, 0)))
        args.append(mod3)
    row = pl.BlockSpec((tm, d), lambda i: (i, 0))
    out_shape = ([jax.ShapeDtypeStruct((m, d), F32)] if emit_x else []) + [jax.ShapeDtypeStruct((m, d), out_dtype)]
    out_specs = [row] * len(out_shape)
    if router is not None:
        in_specs += [pl.BlockSpec((d, LANES), lambda i: (0, 0)), pl.BlockSpec((d, LANES), lambda i: (0, 0)),
                     pl.BlockSpec((1, LANES), lambda i: (0, 0))]
        args += list(router[:3])
        out_shape += [jax.ShapeDtypeStruct((m, LANES), jnp.int32), jax.ShapeDtypeStruct((m, LANES), F32)]
        out_specs += [pl.BlockSpec((tm, LANES), lambda i: (i, 0))] * 2
    res = pl.pallas_call(body, grid=(m // tm,), in_specs=in_specs, out_specs=out_specs, out_shape=out_shape,
                         compiler_params=_cp(1), name=name)(*args)
    return res if len(res) > 1 else res[0]


def _norm_mod(x, gain, mod3, rows, lay, out_dtype, *, name="norm_mod"):
    m, d = x.shape
    tm = lay.tile(256)
    return _norm_rows(lambda x_ref: x_ref[...], [(x, (tm, d), lambda i: (i, 0))], gain, mod3, rows, lay, out_dtype,
                      m=m, d=d, name=name)


def _out_norm(a, w, bias, resid, mod3, lay, gain, out_dtype, *, router=None, name, a_fn=None):
    m, d = resid.shape
    k = w.shape[0]
    tm = lay.tile(256)
    has_bias = bias is not None
    if a_fn is None:
        a, a_fn = [(a, tm)], lambda a_ref: a_ref[...]
    n_a = len(a)

    def pre(*refs):
        w_ref, mod_ref, r_ref = refs[n_a:n_a + 3]
        y = _dot(a_fn(*refs[:n_a]), w_ref[...])
        if has_bias:
            y = y + refs[n_a + 3][...]
        return r_ref[...] + mod_ref[0, 2:3, :] * y

    ops = [(arr, (rows, arr.shape[1]), (lambda i: (i, 0)) if rows == tm else (lambda i: (0, 0))) for arr, rows in a]
    ops += [(w.astype(BF16), (k, d), lambda i: (0, 0)),
            (mod3, (1, N_MOD, d), lambda i: (lay.group(i * tm), 0, 0)), (resid, (tm, d), lambda i: (i, 0))]
    if has_bias:
        ops.append((bias.reshape(1, d), (1, d), lambda i: (0, 0)))
    return _norm_rows(pre, ops, gain, mod3, (3, 4), lay, out_dtype, m=m, d=d, emit_x=True, router=router, name=name)


def _dense_ffn(x, h, w_gate, w_up, w_down, mod3, lay):
    d_ff = w_gate.shape[1]
    hid = _mm_dual(h, w_gate, w_up, None, None, lambda g, u: _silu(g) * u, BF16, n=d_ff, col2=0,
                   tm=lay.tile(1024), tn=512, name="ffn_up")
    return _mm_resid(hid, w_down, None, x, mod3, 5, lay, tm=lay.tile(512), tn=512, n_outer=True,
                     name="ffn_down")


def _moe_up(xs, w_gate, w_up, block_e, block_src, n_used, *, tm):
    n_slots, d = xs.shape
    n_e, _, d_ff = w_gate.shape
    tn = _tile(1024, d_ff)
    n_blocks = n_slots // tm

    def body(be_ref, bs_ref, nu_ref, x_ref, wg_ref, wu_ref, o_ref):
        @pl.when(pl.program_id(1) < nu_ref[0])
        def _():
            xv = x_ref[...]
            g = _dot(xv, wg_ref[...].astype(BF16))
            u = _dot(xv, wu_ref[...].astype(BF16))
            o_ref[...] = (_silu(g) * u).astype(o_ref.dtype)

    grid_spec = pltpu.PrefetchScalarGridSpec(
        num_scalar_prefetch=3, grid=(d_ff // tn, n_blocks),
        in_specs=[pl.BlockSpec((tm, d), lambda j, i, be, bs, nu: (bs[i], 0)),
                  pl.BlockSpec((None, d, tn), lambda j, i, be, bs, nu: (be[i], 0, j)),
                  pl.BlockSpec((None, d, tn), lambda j, i, be, bs, nu: (be[i], 0, j))],
        out_specs=pl.BlockSpec((tm, tn), lambda j, i, be, bs, nu: (bs[i], j)))
    return pl.pallas_call(body, grid_spec=grid_spec, out_shape=jax.ShapeDtypeStruct((n_slots, d_ff), BF16),
                          compiler_params=_cp(2, arbitrary_last=True), name="moe_up"
                          )(block_e, block_src, n_used, xs, w_gate, w_up)


def _moe_down(hid, w_down, block_e, block_src, n_used, *, tm):
    n_slots, d_ff = hid.shape
    d = w_down.shape[2]
    tn = _tile(512, d)
    n_blocks = n_slots // tm

    def body(be_ref, bs_ref, nu_ref, h_ref, w_ref, o_ref):
        @pl.when(pl.program_id(1) < nu_ref[0])
        def _():
            o_ref[...] = _dot(h_ref[...], w_ref[...].astype(BF16)).astype(o_ref.dtype)

    grid_spec = pltpu.PrefetchScalarGridSpec(
        num_scalar_prefetch=3, grid=(d // tn, n_blocks),
        in_specs=[pl.BlockSpec((tm, d_ff), lambda j, i, be, bs, nu: (bs[i], 0)),
                  pl.BlockSpec((None, d_ff, tn), lambda j, i, be, bs, nu: (be[i], 0, j))],
        out_specs=pl.BlockSpec((tm, tn), lambda j, i, be, bs, nu: (bs[i], j)))
    return pl.pallas_call(body, grid_spec=grid_spec, out_shape=jax.ShapeDtypeStruct((n_slots, d), BF16),
                          compiler_params=_cp(2, arbitrary_last=True), name="moe_down"
                          )(block_e, block_src, n_used, hid, w_down)


def _moe_combine(x, y0, y1, top_p, mod3, lay, nxt):
    m, d = x.shape
    tm = lay.tile(256)
    gain, next_mod3, rows, out_dtype, emit_x = nxt

    def pre(x_ref, a_ref, b_ref, p_ref, mod_ref):
        mix = p_ref[:, 0:1] * a_ref[...].astype(F32) + p_ref[:, 1:2] * b_ref[...].astype(F32)
        return x_ref[...] + mod_ref[0, 5:6, :] * mix

    row = lambda arr: (arr, (tm, d), lambda i: (i, 0))
    ops = [row(x), row(y0), row(y1), (top_p, (tm, LANES), lambda i: (i, 0)),
           (mod3, (1, N_MOD, d), lambda i: (lay.group(i * tm), 0, 0))]
    return _norm_rows(pre, ops, gain, next_mod3, rows, lay, out_dtype, m=m, d=d, emit_x=emit_x,
                      name="moe_combine")


def _router_ops(w_router, b_router):
    d, n_e = w_router.shape
    wr = jnp.zeros((d, LANES), F32).at[:, :n_e].set(w_router)
    wr_hi = wr.astype(BF16)
    wr_lo = (wr - wr_hi.astype(F32)).astype(BF16)
    rb = jnp.zeros((1, LANES), F32).at[0, :n_e].set(b_router.astype(F32))
    return wr_hi, wr_lo, rb, n_e


def _moe_ffn(x, h, top_e, top_p, mod3, lay, n_e, w_gate, w_up, w_down, nxt):
    n_tok, d = x.shape
    tm = MOE_ROWS
    nk = n_tok * TOP_K
    flat_e = top_e[:, :TOP_K].reshape(nk)
    onehot = (flat_e[:, None] == jnp.arange(n_e, dtype=jnp.int32)[None, :]).astype(jnp.int32)
    rank = jnp.take_along_axis(jnp.cumsum(onehot, axis=0) - onehot, flat_e[:, None], axis=1)[:, 0]
    counts = jnp.sum(onehot, axis=0)
    padded = (counts + tm - 1) // tm * tm
    pad_end = jnp.cumsum(padded)
    dest = (pad_end - padded)[flat_e] + rank
    n_blocks = -(-nk // tm) + n_e
    n_slots = n_blocks * tm
    slot_tok = jnp.zeros((n_slots,), jnp.int32).at[dest].set(jnp.arange(nk, dtype=jnp.int32) // TOP_K)
    n_used = (pad_end[-1] // tm).astype(jnp.int32)
    blk = jnp.arange(n_blocks, dtype=jnp.int32)
    block_src = jnp.minimum(blk, n_used - 1)
    block_e = jnp.minimum(jnp.searchsorted(pad_end, block_src * tm, side="right"), n_e - 1).astype(jnp.int32)
    xs = jnp.take(h, slot_tok, axis=0, mode="clip")
    hid = _moe_up(xs, w_gate, w_up, block_e, block_src, n_used.reshape(1), tm=tm)
    ys = _moe_down(hid, w_down, block_e, block_src, n_used.reshape(1), tm=tm)
    pos = dest.reshape(n_tok, TOP_K)
    return _moe_combine(x, jnp.take(ys, pos[:, 0], axis=0, mode="clip"),
                        jnp.take(ys, pos[:, 1], axis=0, mode="clip"), top_p, mod3, lay, nxt)


def _mla_mid(dq, q_norm, kv_norm, cos64, sin64, q_lora, kv_lora, lay):
    m, n = dq.shape
    tm = lay.tile(512)
    o_kd, o_ko, o_ks = q_lora + kv_lora, q_lora + kv_lora + QK_ROPE, q_lora + kv_lora + 2 * QK_ROPE

    def body(x_ref, qg_ref, kg_ref, cos_ref, sin_ref, cq_ref, ckv_ref, kro_ref, krr_ref):
        cq = x_ref[:, 0:q_lora]
        cq_ref[...] = (cq * lax.rsqrt(jnp.mean(cq * cq, axis=-1, keepdims=True) + NORM_EPS)
                       * qg_ref[...]).astype(cq_ref.dtype)
        ckv = x_ref[:, q_lora:q_lora + kv_lora]
        ckv_ref[...] = ckv * lax.rsqrt(jnp.mean(ckv * ckv, axis=-1, keepdims=True) + NORM_EPS) * kg_ref[...]
        kro_ref[...] = x_ref[:, o_ko:o_ko + QK_ROPE]
        krr_ref[...] = (x_ref[:, o_kd:o_kd + QK_ROPE] * cos_ref[...]
                        + x_ref[:, o_ks:o_ks + QK_ROPE] * sin_ref[...]).astype(krr_ref.dtype)

    rope = pl.BlockSpec((tm, QK_ROPE), lambda i: (i, 0))
    return pl.pallas_call(
        body, grid=(m // tm,),
        in_specs=[pl.BlockSpec((tm, n), lambda i: (i, 0)), pl.BlockSpec((1, q_lora), lambda i: (0, 0)),
                  pl.BlockSpec((1, kv_lora), lambda i: (0, 0)), rope, rope],
        out_specs=[pl.BlockSpec((tm, q_lora), lambda i: (i, 0)), pl.BlockSpec((tm, kv_lora), lambda i: (i, 0)),
                   rope, rope],
        out_shape=[jax.ShapeDtypeStruct((m, q_lora), BF16), jax.ShapeDtypeStruct((m, kv_lora), F32),
                   jax.ShapeDtypeStruct((m, QK_ROPE), F32), jax.ShapeDtypeStruct((m, QK_ROPE), BF16)],
        compiler_params=_cp(1), name="mla_mid",
    )(dq, q_norm.reshape(1, -1), kv_norm.reshape(1, -1), cos64, sin64)


def _mla_q(cq, w_uq_ext, cos64, sin64, n_heads, tm):
    m = cq.shape[0]
    hw = QK_NOPE + 2 * QK_ROPE
    dqk = QK_NOPE + QK_ROPE
    scale = dqk ** -0.5

    def epi(accs, v):
        x = accs[0]
        rot = x[:, QK_NOPE:dqk] * v[0][...] + x[:, dqk:hw] * v[1][...]
        return [jnp.concatenate([x[:, :QK_NOPE], rot], axis=1) * scale]

    rope = lambda arr: (arr, (tm, QK_ROPE), lambda i, j: (i, 0))
    return _mm(cq, [w_uq_ext], tm=tm, tn=hw, gn=n_heads, epi=epi, vecs=[rope(cos64), rope(sin64)], name="mla_q",
               outs=[((n_heads, m, dqk), BF16, (1, tm, dqk), lambda i, j: (j, i, 0))])[0]


def _mla_kv(ckv, w_ukv, krope, n_heads, tm):
    m = ckv.shape[0]
    dqk = QK_NOPE + QK_ROPE

    def epi(accs, v):
        x = accs[0]
        return [jnp.concatenate([x[:, :QK_NOPE], v[0][...].astype(F32)], axis=1), x[:, QK_NOPE:]]

    return _mm(ckv, [w_ukv], tm=tm, tn=QK_NOPE + V_DIM, gn=n_heads, epi=epi, name="mla_kv",
               vecs=[(krope, (tm, QK_ROPE), lambda i, j: (i, 0))],
               outs=[((n_heads, m, dqk), BF16, (1, tm, dqk), lambda i, j: (j, i, 0)),
                     ((n_heads, m, V_DIM), BF16, (1, tm, V_DIM), lambda i, j: (j, i, 0))])


def _attn_context(q, k, v, lay):
    n_heads = q.shape[0]
    t = lay.ctx_len
    dqk = q.shape[2]

    def body(q_ref, k_ref, v_ref, o_ref):
        for h in range(n_heads):
            s = _dot_nt(q_ref[h], k_ref[h])
            p = jnp.exp(s - jnp.max(s, axis=-1, keepdims=True))
            l = jnp.sum(p, axis=-1, keepdims=True)
            o = _dot(p.astype(BF16), v_ref[h]) / l
            o_ref[:, h * V_DIM:(h + 1) * V_DIM] = o.astype(o_ref.dtype)

    return pl.pallas_call(
        body, grid=(lay.n_ctx_seq,),
        in_specs=[pl.BlockSpec((n_heads, t, dqk), lambda b: (0, b, 0)),
                  pl.BlockSpec((n_heads, t, dqk), lambda b: (0, b, 0)),
                  pl.BlockSpec((n_heads, t, V_DIM), lambda b: (0, b, 0))],
        out_specs=pl.BlockSpec((t, n_heads * V_DIM), lambda b: (b, 0)),
        out_shape=jax.ShapeDtypeStruct((lay.n_ctx, n_heads * V_DIM), BF16),
        compiler_params=_cp(1), name="attn_context",
    )(q, k, v)


def _attn_latent(q, k, v, kc, vc, lay, past_len):
    n_heads = q.shape[0]
    t = lay.lat_len
    dqk = q.shape[2]
    hp = 2 if n_heads % 2 == 0 else 1
    tq = _tile(256, t)
    lat0 = lay.n_ctx // t
    q0 = lay.n_ctx // tq
    kch = _tile(1024, t)
    n_ch = t // kch

    def body(q_ref, k_ref, v_ref, kc_ref, vc_ref, o_ref):
        def scores(h):
            qv = q_ref[h]
            return ([_dot_nt(qv, k_ref[h, j * kch:(j + 1) * kch, :]) for j in range(n_ch)]
                    + [_dot_nt(qv, kc_ref[h])])

        def row_max(s):
            return functools.reduce(jnp.maximum, [jnp.max(x, axis=-1, keepdims=True) for x in s])

        def probs(s, mx):
            p = [jnp.exp(x - mx) for x in s]
            l = functools.reduce(jnp.add, [jnp.sum(x, axis=-1, keepdims=True) for x in p])
            return [x.astype(BF16) for x in p], l

        def weighted(h, p):
            vs = [v_ref[h, j * kch:(j + 1) * kch, :] for j in range(n_ch)] + [vc_ref[h]]
            return functools.reduce(jnp.add, [_dot(x, vv) for x, vv in zip(p, vs)])

        s = [scores(h) for h in range(hp)]
        outs = []
        p_prev = None
        for h in range(hp):
            p, l = probs(s[h], row_max(s[h]))
            if p_prev is not None:
                outs.append(weighted(h - 1, p_prev[0]) / p_prev[1])
            p_prev = (p, l)
        outs.append(weighted(hp - 1, p_prev[0]) / p_prev[1])
        o_ref[...] = jnp.concatenate(outs, axis=1).astype(o_ref.dtype)

    return pl.pallas_call(
        body, grid=(lay.n_lat_seq, n_heads // hp, t // tq),
        in_specs=[pl.BlockSpec((hp, tq, dqk), lambda b, h, i: (h, q0 + b * (t // tq) + i, 0)),
                  pl.BlockSpec((hp, t, dqk), lambda b, h, i: (h, lat0 + b, 0)),
                  pl.BlockSpec((hp, t, V_DIM), lambda b, h, i: (h, lat0 + b, 0)),
                  pl.BlockSpec((hp, past_len, dqk), lambda b, h, i: (h, b, 0)),
                  pl.BlockSpec((hp, past_len, V_DIM), lambda b, h, i: (h, b, 0))],
        out_specs=pl.BlockSpec((tq, hp * V_DIM), lambda b, h, i: (b * (t // tq) + i, h)),
        out_shape=jax.ShapeDtypeStruct((lay.n_lat_seq * t, n_heads * V_DIM), BF16),
        compiler_params=_cp(3), name="attn_latent",
    )(q, k, v, kc, vc)


def _rope_tables(lay):
    n = lay.lat_len
    pairs = QK_ROPE // 4
    row_pos = (jnp.arange(n) // GRID_W).astype(F32)
    col_pos = (jnp.arange(n) % GRID_W).astype(F32)
    inv_freq = ROPE_THETA ** (-jnp.arange(pairs, dtype=F32) / pairs)
    ang = jnp.concatenate([row_pos[:, None] * inv_freq, col_pos[:, None] * inv_freq], axis=-1)
    cos, sin = jnp.cos(ang), jnp.sin(ang)
    cos64 = jnp.concatenate([cos, cos], axis=-1)
    sin64 = jnp.concatenate([-sin, sin], axis=-1)
    one = jnp.ones((lay.n_ctx, QK_ROPE), F32)
    cos_all = jnp.concatenate([one] + [cos64] * lay.n_lat_seq, axis=0)
    sin_all = jnp.concatenate([0.0 * one] + [sin64] * lay.n_lat_seq, axis=0)
    return cos_all, sin_all


def _mla_layer(x, h, mod3, lay, norm_ffn, cache_ckv, cache_krope, w_down, q_norm, kv_norm, w_uq, w_ukv, w_o):
    d = x.shape[1]
    q_lora, kv_lora = q_norm.shape[0], kv_norm.shape[0]
    n_heads = w_o.shape[0] // V_DIM
    dqk = QK_NOPE + QK_ROPE
    perm_d = np.concatenate([np.arange(0, QK_ROPE, 2), np.arange(1, QK_ROPE, 2)])
    perm_s = np.concatenate([np.arange(1, QK_ROPE, 2), np.arange(0, QK_ROPE, 2)])
    kr0 = q_lora + kv_lora
    w_down_ext = jnp.concatenate(
        [w_down[:, :kr0], w_down[:, kr0 + perm_d], w_down[:, kr0:], w_down[:, kr0 + perm_s],
         jnp.zeros((d, QK_ROPE), F32)], axis=1)
    w_uq_h = w_uq.reshape(q_lora, n_heads, dqk)
    w_uq_ext = jnp.concatenate([w_uq_h[:, :, :QK_NOPE], w_uq_h[:, :, QK_NOPE + perm_d],
                                w_uq_h[:, :, QK_NOPE + perm_s]], axis=2).reshape(q_lora, -1)
    cos64, sin64 = _rope_tables(lay)
    tm = lay.tile(1024)
    dq = _mm_act(h, w_down_ext, None, None, F32, tm=tm, tn=512, name="mla_down")
    cq, ckv, krope_raw, krope_rot = _mla_mid(dq, q_norm, kv_norm, cos64, sin64, q_lora, kv_lora, lay)
    q = _mla_q(cq, w_uq_ext, cos64, sin64, n_heads, lay.tile(2048))
    k, v = _mla_kv(ckv, w_ukv, krope_rot, n_heads, lay.tile(2048))
    n_past = cache_ckv.shape[0] * cache_ckv.shape[1]
    kc, vc = _mla_kv(cache_ckv.reshape(n_past, kv_lora), w_ukv,
                     cache_krope.reshape(n_past, QK_ROPE)[:, perm_d].astype(BF16), n_heads,
                     _tile(512, cache_ckv.shape[1]))
    o_ctx = _attn_context(q, k, v, lay)
    o_lat = _attn_latent(q, k, v, kc, vc, lay, cache_ckv.shape[1])
    attn = jnp.concatenate([o_ctx, o_lat], axis=0)
    x, h_ffn = _out_norm(attn, w_o, None, x, mod3, lay, norm_ffn, BF16, name="mla_out")
    ckv_ctx = ckv[:lay.n_ctx].reshape(lay.n_ctx_seq, lay.ctx_len, kv_lora)
    krope_ctx = krope_raw[:lay.n_ctx].reshape(lay.n_ctx_seq, lay.ctx_len, QK_ROPE)
    return x, h_ffn, ckv_ctx, krope_ctx


def _conv_dw(u, w_dw, b_dw, ln_g, ln_b, lay):
    m, d = u.shape
    width = w_dw.shape[0]
    pad = width // 2
    t = lay.tile(256)
    halo = 2 * SUBLANES
    assert pad <= halo
    rb = _tile(64, t)

    sh_rows = t + halo + SUBLANES

    def body(prev_ref, cur_ref, nxt_ref, w_ref, bdw_ref, g_ref, b_ref, o_ref, ext_ref, dw_ref, sh_ref):
        pos, ln = lay.seq_pos(pl.program_id(0) * t)
        ext_ref[0:halo, :] = jnp.where(pos > 0, prev_ref[...], 0.0)
        ext_ref[halo:halo + t, :] = cur_ref[...]
        ext_ref[halo + t:halo + t + halo, :] = jnp.where(pos + t < ln, nxt_ref[...], 0.0)

        def chunk(c, carry):
            ls = pl.ds(pl.multiple_of(c * LANES, LANES), LANES)
            for s in range(SUBLANES):
                sh_ref[s] = ext_ref[pl.ds(s, sh_rows), ls]
            for r in range(t // rb):
                acc = jnp.zeros((rb, LANES), F32)
                for j in range(width):
                    off = halo - pad + j
                    acc = acc + (sh_ref[off % SUBLANES, pl.ds(off // SUBLANES * SUBLANES + r * rb, rb), :]
                                 * w_ref[pl.ds(j, 1), ls])
                dw_ref[pl.ds(r * rb, rb), ls] = acc + bdw_ref[:, ls]
            return carry

        lax.fori_loop(0, d // LANES, chunk, 0)
        xv = dw_ref[...]
        mu = jnp.mean(xv, axis=-1, keepdims=True)
        xc = xv - mu
        var = jnp.mean(xc * xc, axis=-1, keepdims=True)
        o_ref[...] = _silu(xc * lax.rsqrt(var + LN_EPS) * g_ref[...] + b_ref[...]).astype(o_ref.dtype)

    hb = t // halo
    vec = pl.BlockSpec((1, d), lambda i: (0, 0))
    return pl.pallas_call(
        body, grid=(m // t,),
        in_specs=[pl.BlockSpec((halo, d), lambda i: (jnp.maximum(i * hb - 1, 0), 0)),
                  pl.BlockSpec((t, d), lambda i: (i, 0)),
                  pl.BlockSpec((halo, d), lambda i: (jnp.minimum((i + 1) * hb, m // halo - 1), 0)),
                  pl.BlockSpec((width, d), lambda i: (0, 0)), vec, vec, vec],
        out_specs=pl.BlockSpec((t, d), lambda i: (i, 0)),
        out_shape=jax.ShapeDtypeStruct((m, d), BF16),
        scratch_shapes=[pltpu.VMEM((t + 2 * halo, d), F32), pltpu.VMEM((t, d), F32),
                        pltpu.VMEM((SUBLANES, sh_rows, LANES), F32)],
        compiler_params=_cp(1), name="conv_dw",
    )(u, u, u, w_dw, b_dw.reshape(1, d), ln_g.reshape(1, d), ln_b.reshape(1, d))


def _conformer_layer(x, h, mod3, lay, norm_ffn, router, w_in, b_in, w_dw, b_dw, ln_g, ln_b, w_out, b_out):
    d = x.shape[1]
    tm = lay.tile(1024)
    u = _mm_dual(h, w_in, w_in, b_in, b_in, lambda a, b: a * jax.nn.sigmoid(b), F32, n=d, col2=d // 512,
                 tm=tm, tn=512, name="conv_in")
    cv = _conv_dw(u, w_dw, b_dw, ln_g, ln_b, lay)
    return _out_norm(cv, w_out, b_out, x, mod3, lay, norm_ffn, BF16, router=router, name="conv_out")


def _sgu_mix(z, ln_g, ln_b, w_s, b_s, lay):
    m, d2 = z.shape
    d = d2 // 2
    groups, chunk, _ = w_s.shape
    gd = d // groups
    tm = lay.tile(2 * chunk)
    bs_t = jnp.transpose(b_s)

    def body(u_ref, v_ref, g_ref, b_ref, ws_ref, bs_ref, o_ref):
        xv = v_ref[...].astype(F32)
        mu = jnp.mean(xv, axis=-1, keepdims=True)
        xc = xv - mu
        var = jnp.mean(xc * xc, axis=-1, keepdims=True)
        vn =(xc * lax.rsqrt(var + LN_EPS) * g_ref[...] + b_ref[...]).astype(BF16)
        for c in range(tm // chunk):
            rs = slice(c * chunk, (c + 1) * chunk)
            for g in range(groups):
                ls = slice(g * gd, (g + 1) * gd)
                mixed = _dot(ws_ref[g].astype(BF16), vn[rs, ls]) + bs_ref[:, g:g + 1]
                o_ref[rs, ls] = (u_ref[rs, ls].astype(F32) * mixed).astype(o_ref.dtype)

    vec = pl.BlockSpec((1, d), lambda i: (0, 0))
    return pl.pallas_call(
        body, grid=(m // tm,),
        in_specs=[pl.BlockSpec((tm, d), lambda i: (i, 0)), pl.BlockSpec((tm, d), lambda i: (i, 1)), vec, vec,
                  pl.BlockSpec((groups, chunk, chunk), lambda i: (0, 0, 0)),
                  pl.BlockSpec((chunk, groups), lambda i: (0, 0))],
        out_specs=pl.BlockSpec((tm, d), lambda i: (i, 0)),
        out_shape=jax.ShapeDtypeStruct((m, d), BF16), compiler_params=_cp(1), name="sgu_mix",
    )(z, z, ln_g.reshape(1, d), ln_b.reshape(1, d), w_s, bs_t)


def _sgu_layer(x, h, mod3, lay, norm_ffn, w_in, b_in, ln_g, ln_b, w_s, b_s, w_out, b_out):
    tm = lay.tile(1024)
    d = x.shape[1]
    z = _mm_act(h, w_in, b_in, _gelu_tanh, BF16, tm=lay.tile(512), tn=_tile(1024, d), name="sgu_in", n_outer=True)
    sm = _sgu_mix(z, ln_g, ln_b, w_s, b_s, lay)
    return _out_norm(sm, w_out, b_out, x, mod3, lay, norm_ffn, BF16, name="sgu_out")


def _shift_mix(h, mu, lay):
    m, d = h.shape
    t = lay.tile(256)
    halo = SUBLANES

    def body(prev_ref, cur_ref, nxt_ref, mu_ref, *rest):
        o_refs, ext_ref = rest[:6], rest[6]
        pos, ln = lay.seq_pos(pl.program_id(0) * t)
        ext_ref[0:halo, :] = jnp.where(pos > 0, prev_ref[...], 0.0)
        ext_ref[halo:halo + t, :] = cur_ref[...]
        ext_ref[halo + t:halo + t + halo, :] = jnp.where(pos + t < ln, nxt_ref[...], 0.0)
        xv = cur_ref[...]
        xx = 0.5 * (ext_ref[pl.ds(halo - 1, t), :] + ext_ref[pl.ds(halo + 1, t), :]) - xv
        for j in range(6):
            o_refs[j][...] = (xv + xx * mu_ref[j:j + 1, :]).astype(BF16)

    hb = t // halo
    row = pl.BlockSpec((t, d), lambda i: (i, 0))
    return pl.pallas_call(
        body, grid=(m // t,),
        in_specs=[pl.BlockSpec((halo, d), lambda i: (jnp.maximum(i * hb - 1, 0), 0)), row,
                  pl.BlockSpec((halo, d), lambda i: (jnp.minimum((i + 1) * hb, m // halo - 1), 0)),
                  pl.BlockSpec((6, d), lambda i: (0, 0))],
        out_specs=[row] * 6, out_shape=[jax.ShapeDtypeStruct((m, d), BF16)] * 6,
        scratch_shapes=[pltpu.VMEM((t + 2 * halo, d), F32)],
        compiler_params=_cp(1), name="rwkv_shift_mix",
    )(h, h, h, mu)


def _wkv_prep(r, k, v, wl, al, k_k, k_a, r_k):
    m, d = r.shape
    tm = _tile(128, m)

    def body(r_ref, k_ref, v_ref, wl0_ref, wl1_ref, al0_ref, al1_ref, kk_ref, ka_ref, rk_ref,
             okk_ref, olw0_ref, olw1_ref, ob0_ref, ob1_ref, okt0_ref, okt1_ref, obon_ref):
        rv, kv, vv = r_ref[...], k_ref[...], v_ref[...]
        kk = kv * kk_ref[...]
        kk = kk / jnp.maximum(jnp.sqrt(_head_sum(kk * kk)), 1e-12)
        okk_ref[...] = kk.astype(okk_ref.dtype)
        bonus = jnp.zeros_like(rv)
        for wl_ref, al_ref, olw_ref, ob_ref, okt_ref in ((wl0_ref, al0_ref, olw0_ref, ob0_ref, okt0_ref),
                                                          (wl1_ref, al1_ref, olw1_ref, ob1_ref, okt1_ref)):
            w_log = -_softplus(-wl_ref[...]) - 0.5
            olw_ref[...] = -jnp.exp(w_log)
            a = jax.nn.sigmoid(al_ref[...])
            kt = kv * (1.0 + (a - 1.0) * ka_ref[...])
            ob_ref[...] = (kk * a).astype(ob_ref.dtype)
            okt_ref[...] = kt.astype(okt_ref.dtype)
            bonus = bonus + _head_sum(rv * kt * rk_ref[...]) * vv
        obon_ref[...] = bonus

    row = pl.BlockSpec((tm, d), lambda i: (i, 0))
    vec = pl.BlockSpec((1, d), lambda i: (0, 0))
    f32 = jax.ShapeDtypeStruct((m, d), F32)
    bf16 = jax.ShapeDtypeStruct((m, d), BF16)
    kk, lw0, lw1, b0, b1, kt0, kt1, bonus = pl.pallas_call(
        body, grid=(m // tm,),
        in_specs=[row] * 7 + [vec] * 3,
        out_specs=[row] * 8,
        out_shape=[bf16, f32, f32, bf16, bf16, bf16, bf16, f32],
        compiler_params=_cp(1), name="wkv_prep",
    )(r, k, v, wl[0], wl[1], al[0], al[1], k_k.reshape(1, d), k_a.reshape(1, d), r_k.reshape(1, d))
    return kk, (lw0, lw1), (b0, b1), (kt0, kt1), bonus


def _wkv_scan_dir(r, kk, v, lw, b, kt, s_init, lay, reverse):
    m, d = r.shape
    c = WKV_CHUNK
    rows = c * WKV_CHUNKS_PER_STEP
    n_pairs = d // LANES
    pps = math.gcd(WKV_PAIRS_PER_STEP, n_pairs)
    pw = pps * LANES
    n_steps = m // rows
    ctx_steps = lay.n_ctx // rows
    sps_ctx, sps_lat = lay.ctx_len // rows, lay.lat_len // rows
    n_seq = lay.n_ctx_seq + lay.n_lat_seq
    c2 = 2 * c
    sgn = -1 if reverse else 1

    def block_of(s):
        return n_steps - 1 - s if reverse else s

    def seq_of(tb):
        is_ctx = tb < ctx_steps
        seq = jnp.where(is_ctx, tb // sps_ctx, lay.n_ctx_seq + (tb - ctx_steps) // sps_lat)
        pos = jnp.where(is_ctx, tb % sps_ctx, (tb - ctx_steps) % sps_lat)
        return is_ctx, seq, pos, jnp.where(is_ctx, sps_ctx, sps_lat)

    def body(r_ref, kk_ref, v_ref, lw_ref, b_ref, kt_ref, si_ref, y_ref, so_ref, s_scr):
        is_ctx, _, pos, sps = seq_of(block_of(pl.program_id(1)))
        first = pos == (sps - 1 if reverse else 0)
        last = pos == (0 if reverse else sps - 1)

        @pl.when(first)
        def _():
            s_scr[...] = jnp.where(is_ctx, 0.0, si_ref[0])

        ri = lax.broadcasted_iota(jnp.int32, (c2, c2), 0)
        ci = lax.broadcasted_iota(jnp.int32, (c2, c2), 1)
        same = (ri // c) == (ci // c)
        strict = same & ((ri - ci) * sgn > 0)
        incl = same & ((ri - ci) * sgn >= 0)
        eye = jnp.where(ri == ci, 1.0, 0.0)
        end_row = 0 if reverse else c - 1
        step_row = lax.broadcasted_iota(jnp.int32, (c, LANES), 0)

        def running_sum(x):
            s = 1
            while s < c:
                if reverse:
                    x = x + jnp.where(step_row < c - s, pltpu.roll(x, c - s, axis=0), 0.0)
                else:
                    x = x + jnp.where(step_row >= s, pltpu.roll(x, s, axis=0), 0.0)
                s *= 2
            return x
        own = (lax.broadcasted_iota(jnp.int32, (c2, LANES), 0) // c
               == lax.broadcasted_iota(jnp.int32, (c2, LANES), 1) // RWKV_N)

        def stack(xv):
            return jnp.where(own, jnp.concatenate([xv, xv], axis=0), 0.0).astype(BF16)

        chunk_order = list(range(WKV_CHUNKS_PER_STEP))[::sgn]
        probs = [(j, p) for j in chunk_order for p in range(pps)]
        at = lambda ref, j, p: ref[j * c:(j + 1) * c, p * LANES:(p + 1) * LANES].astype(F32)

        lwv = [at(lw_ref, j, p) for j, p in probs]
        cs = [running_sum(x) for x in lwv]
        g = [jnp.exp(x) for x in cs]
        ginv = [jnp.exp(-x) for x in cs]
        gprev = [jnp.exp(x - w) for x, w in zip(cs, lwv)]
        gend = [jnp.exp(x[end_row:end_row + 1] - x) for x in cs]
        gall = [jnp.exp(x[end_row:end_row + 1]) for x in cs]
        kkv = [at(kk_ref, j, p) for j, p in probs]
        bv = [at(b_ref, j, p) for j, p in probs]
        ktv = [at(kt_ref, j, p) for j, p in probs]
        n = len(probs)
        xa = [stack(-kkv[i] * gprev[i]) for i in range(n)]
        xr = [stack(at(r_ref, *probs[i]) * g[i]) for i in range(n)]
        xb = [stack(bv[i] * ginv[i]) for i in range(n)]
        xk = [stack(ktv[i] * ginv[i]) for i in range(n)]
        xb_end = [stack(bv[i] * gend[i]) for i in range(n)]
        xk_end = [stack(ktv[i] * gend[i]) for i in range(n)]
        xv = [stack(at(v_ref, *probs[i])) for i in range(n)]
        mm = [_dot_nt(jnp.concatenate([xa[i], xr[i]], axis=0), jnp.concatenate([xb[i], xk[i]], axis=0))
              for i in range(n)]
        m_ab = [jnp.where(strict, x[:c2, :c2], 0.0) for x in mm]
        m_ak = [jnp.where(strict, x[:c2, c2:], 0.0).astype(BF16) for x in mm]
        m_rb = [jnp.where(incl, x[c2:, :c2], 0.0).astype(BF16) for x in mm]
        m_rk = [jnp.where(incl, x[c2:, c2:], 0.0).astype(BF16) for x in mm]
        akv = [_dot(m_ak[i], xv[i]).astype(BF16) for i in range(n)]
        pj = [x.astype(BF16) for x in m_ab]
        tinv = [eye + x for x in m_ab]
        pj = [_dot(x, x).astype(BF16) for x in pj]
        for _ in range(int(math.log2(c)) - 2):
            both = [_dot(jnp.concatenate([pj[i], tinv[i].astype(BF16)], axis=0), pj[i]) for i in range(n)]
            pj = [x[:c2].astype(BF16) for x in both]
            tinv = [tinv[i] + both[i][c2:] for i in range(n)]
        tinv = [(tinv[i] + _dot(tinv[i].astype(BF16), pj[i])).astype(BF16) for i in range(n)]
        pq = [_dot(tinv[i], jnp.concatenate([xa[i], akv[i]], axis=1)).astype(BF16) for i in range(n)]
        ab = [_dot_tn(pq[i], xb_end[i]) for i in range(n)]
        a_c = [(eye * gall[i] + ab[i][:LANES]).astype(BF16) for i in range(n)]
        b_c = [ab[i][LANES:] + _dot_tn(xv[i], xk_end[i]) for i in range(n)]
        cd = [_dot(m_rb[i], pq[i]) for i in range(n)]
        c_c = [(xr[i].astype(F32) + cd[i][:, :LANES]).astype(BF16) for i in range(n)]
        d_c = [cd[i][:, LANES:] + _dot(m_rk[i], xv[i]) for i in range(n)]
        for p in range(pps):
            s = s_scr[p]
            for i, (j, pp) in enumerate(probs):
                if pp != p:
                    continue
                s_hi, s_lo = _split_bf16(s, 2)
                yy = _dot_nt(c_c[i], s_hi) + _dot_nt(c_c[i], s_lo) + d_c[i]
                y_ref[j * c:(j + 1) * c, p * LANES:(p + 1) * LANES] = yy[:c] + yy[c:]
                s = _dot(s_hi, a_c[i]) + _dot(s_lo, a_c[i]) + b_c[i]
            s_scr[p] = s

        @pl.when(last)
        def _():
            so_ref[0] = s_scr[...]

    tok = pl.BlockSpec((rows, pw), lambda lb, s: (block_of(s), lb))
    st = (1, pps, LANES, LANES)
    return pl.pallas_call(
        body, grid=(n_pairs // pps, n_steps),
        in_specs=[tok] * 6 + [pl.BlockSpec(st, lambda lb, s: (
            jnp.clip(seq_of(block_of(s))[1] - lay.n_ctx_seq, 0, lay.n_lat_seq - 1), lb, 0, 0))],
        out_specs=[tok, pl.BlockSpec(st, lambda lb, s: (seq_of(block_of(s))[1], lb, 0, 0))],
        out_shape=[jax.ShapeDtypeStruct((m, d), F32), jax.ShapeDtypeStruct((n_seq, n_pairs, LANES, LANES), F32)],
        scratch_shapes=[pltpu.VMEM((pps, LANES, LANES), F32)],
        compiler_params=_cp(2, arbitrary_last=True), name="wkv_scan_bwd" if reverse else "wkv_scan_fwd",
    )(r, kk, v, lw, b, kt, s_init)


def _wkv_post(y_fwd, y_bwd, bonus, g, ln_g, ln_b, tm):
    d = bonus.shape[1]

    def tile_fn(yf_ref, yb_ref, bon_ref, g_ref, lg_ref, lb_ref):
        yv = yf_ref[...] + yb_ref[...]
        mu = _head_sum(yv) * (1.0 / RWKV_N)
        yc = yv - mu
        var = _head_sum(yc * yc) * (1.0 / RWKV_N)
        o = yc * lax.rsqrt(var + GN_EPS) * lg_ref[...] + lb_ref[...] + bon_ref[...]
        return (o * g_ref[...]).astype(BF16)

    ops = [(y_fwd, tm), (y_bwd, tm), (bonus, tm), (g, tm), (ln_g.reshape(1, d), 1), (ln_b.reshape(1, d), 1)]
    return ops, tile_fn


def _pad_cols(w, n):
    return jnp.pad(w, ((0, 0), (0, n - w.shape[1])))


def _pad_rows(w, n):
    return jnp.pad(w, ((0, n - w.shape[0]), (0, 0)))


def _rwkv_layer(x, hf, mod3, lay, norm_ffn, router, s_fwd, s_bwd, mu, w_r, w_k, w_v, w_o, w0, w1, w2, a0, a1, a2,
                g1, g2, k_k, k_a, r_k, ln_g, ln_b):
    m, d = x.shape
    tm = lay.tile(1024)
    xr, xw, xk, xv, xa, xg = _shift_mix(hf, mu, lay)
    sq = dict(tm=lay.tile(512), tn=_tile(1024, d), n_outer=True)
    r = _mm_act(xr, w_r, None, None, F32, name="rwkv_r", **sq)
    k = _mm_act(xk, w_k, None, None, F32, name="rwkv_k", **sq)
    v = _mm_act(xv, w_v, None, None, F32, name="rwkv_v", **sq)
    gh = _mm_act(xg, g1, None, jax.nn.sigmoid, BF16, tm=tm, tn=g1.shape[1], name="rwkv_g1")
    g = _mm_act(gh, g2, None, None, F32, name="rwkv_g2", **sq)
    both = lambda w: jnp.concatenate([_pad_cols(w[0], LANES), _pad_cols(w[1], LANES)], axis=1)
    th = _mm_act(xw, both(w1), None, jnp.tanh, BF16, tm=tm, tn=2 * LANES, name="rwkv_w1")
    ah = _mm_act(xa, both(a1), None, None, BF16, tm=tm, tn=2 * LANES, name="rwkv_a1")
    wl, al = [], []
    for dr in range(2):
        cols = slice(dr * LANES, (dr + 1) * LANES)
        wl.append(_mm_act(th[:, cols], _pad_rows(w2[dr], LANES), w0[dr], None, F32, name="rwkv_w2", **sq))
        al.append(_mm_act(ah[:, cols], _pad_rows(a2[dr], LANES), a0[dr], None, F32, name="rwkv_a2", **sq))
    kk, lw, b, kt, bonus = _wkv_prep(r, k, v, wl, al, k_k, k_a, r_k.reshape(-1))
    n_pairs = d // LANES
    ys, finals = [], []
    for dr, s0 in enumerate((s_fwd, s_bwd)):
        st = s0.astype(F32).reshape(-1, n_pairs, 2, RWKV_N, RWKV_N)
        s_init = jnp.zeros((st.shape[0], n_pairs, LANES, LANES), F32)
        s_init = s_init.at[..., :RWKV_N, :RWKV_N].set(st[:, :, 0]).at[..., RWKV_N:, RWKV_N:].set(st[:, :, 1])
        y, s_fin = _wkv_scan_dir(r, kk, v, lw[dr], b[dr], kt[dr], s_init, lay, reverse=dr == 1)
        sf = s_fin[:lay.n_ctx_seq]
        s_pair = jnp.stack([sf[..., :RWKV_N, :RWKV_N], sf[..., RWKV_N:, RWKV_N:]], axis=2)
        ys.append(y)
        finals.append(s_pair.reshape(lay.n_ctx_seq, 2 * n_pairs, RWKV_N, RWKV_N))
    post_ops, post_fn = _wkv_post(ys[0], ys[1], bonus, g, ln_g, ln_b, lay.tile(256))
    x, h_ffn, top_e, top_p = _out_norm(post_ops, w_o, None, x, mod3, lay, norm_ffn, BF16, router=router,
                                       name="rwkv_out", a_fn=post_fn)
    return x, h_ffn, top_e, top_p, finals[0], finals[1]


def kernel(x_prompt, x_sample, cache_ckv_l0, cache_krope_l0, state_wkv_fwd_l3, state_wkv_bwd_l3, c, c_ctx, l0_w_mod, l0_b_mod, l0_norm_mix, l0_norm_ffn, l0_mla_w_down, l0_mla_q_norm, l0_mla_kv_norm, l0_mla_w_uq, l0_mla_w_ukv, l0_mla_w_o, l0_ffn_w_gate, l0_ffn_w_up, l0_ffn_w_down, l1_w_mod, l1_b_mod, l1_norm_mix, l1_norm_ffn, l1_conv_w_in, l1_conv_b_in, l1_conv_w_dw, l1_conv_b_dw, l1_conv_ln_g, l1_conv_ln_b, l1_conv_w_out, l1_conv_b_out, l1_moe_w_router, l1_moe_b_router, l1_moe_w_gate, l1_moe_w_up, l1_moe_w_down, l2_w_mod, l2_b_mod, l2_norm_mix, l2_norm_ffn, l2_sgu_w_in, l2_sgu_b_in, l2_sgu_ln_g, l2_sgu_ln_b, l2_sgu_w_s, l2_sgu_b_s, l2_sgu_w_out, l2_sgu_b_out, l2_ffn_w_gate, l2_ffn_w_up, l2_ffn_w_down, l3_w_mod, l3_b_mod, l3_norm_mix, l3_norm_ffn, l3_rwkv_mu, l3_rwkv_w_r, l3_rwkv_w_k, l3_rwkv_w_v, l3_rwkv_w_o, l3_rwkv_w0, l3_rwkv_w1, l3_rwkv_w2, l3_rwkv_a0, l3_rwkv_a1, l3_rwkv_a2, l3_rwkv_g1, l3_rwkv_g2, l3_rwkv_k_k, l3_rwkv_k_a, l3_rwkv_r_k, l3_rwkv_ln_g, l3_rwkv_ln_b, l3_moe_w_router, l3_moe_b_router, l3_moe_w_gate, l3_moe_w_up, l3_moe_w_down, norm_out):
    n_ctx_seq, ctx_len, d = x_prompt.shape
    n_lat_seq, lat_len, _ = x_sample.shape
    lay = _Lay(n_ctx_seq, ctx_len, n_lat_seq, lat_len)
    assert n_lat_seq + 1 <= SUBLANES
    x = jnp.concatenate([x_prompt.reshape(-1, d), x_sample.reshape(-1, d)], axis=0)
    cond8 = jnp.zeros((SUBLANES, d), F32).at[0].set(c_ctx).at[1:1 + n_lat_seq].set(c)

    mods = [_ada_mod(cond8, w, b) for w, b in ((l0_w_mod, l0_b_mod), (l1_w_mod, l1_b_mod), (l2_w_mod, l2_b_mod),
                                               (l3_w_mod, l3_b_mod))]

    h = _norm_mod(x, l0_norm_mix, mods[0], (0, 1), lay, BF16)
    x, h, ckv_l0, krope_l0 = _mla_layer(x, h, mods[0], lay, l0_norm_ffn, cache_ckv_l0, cache_krope_l0, l0_mla_w_down,
                                        l0_mla_q_norm, l0_mla_kv_norm, l0_mla_w_uq, l0_mla_w_ukv, l0_mla_w_o)
    x = _dense_ffn(x, h, l0_ffn_w_gate, l0_ffn_w_up, l0_ffn_w_down, mods[0], lay)

    h = _norm_mod(x, l1_norm_mix, mods[1], (0, 1), lay, BF16)
    router = _router_ops(l1_moe_w_router, l1_moe_b_router)
    x, h, top_e, top_p = _conformer_layer(x, h, mods[1], lay, l1_norm_ffn, router, l1_conv_w_in, l1_conv_b_in,
                                          l1_conv_w_dw, l1_conv_b_dw, l1_conv_ln_g, l1_conv_ln_b, l1_conv_w_out,
                                          l1_conv_b_out)
    x, h = _moe_ffn(x, h, top_e, top_p, mods[1], lay, router[3], l1_moe_w_gate, l1_moe_w_up, l1_moe_w_down,
                    (l2_norm_mix, mods[2], (0, 1), BF16, True))

    x, h = _sgu_layer(x, h, mods[2], lay, l2_norm_ffn, l2_sgu_w_in, l2_sgu_b_in, l2_sgu_ln_g, l2_sgu_ln_b,
                      l2_sgu_w_s, l2_sgu_b_s, l2_sgu_w_out, l2_sgu_b_out)
    x = _dense_ffn(x, h, l2_ffn_w_gate, l2_ffn_w_up, l2_ffn_w_down, mods[2], lay)

    hf = _norm_mod(x, l3_norm_mix, mods[3], (0, 1), lay, F32)
    router = _router_ops(l3_moe_w_router, l3_moe_b_router)
    x, h, top_e, top_p, wkv_fwd, wkv_bwd = _rwkv_layer(
        x, hf, mods[3], lay, l3_norm_ffn, router, state_wkv_fwd_l3, state_wkv_bwd_l3, l3_rwkv_mu, l3_rwkv_w_r,
        l3_rwkv_w_k, l3_rwkv_w_v, l3_rwkv_w_o, l3_rwkv_w0, l3_rwkv_w1, l3_rwkv_w2, l3_rwkv_a0, l3_rwkv_a1,
        l3_rwkv_a2, l3_rwkv_g1, l3_rwkv_g2, l3_rwkv_k_k, l3_rwkv_k_a, l3_rwkv_r_k, l3_rwkv_ln_g, l3_rwkv_ln_b)
    y = _moe_ffn(x, h, top_e, top_p, mods[3], lay, router[3], l3_moe_w_gate, l3_moe_w_up, l3_moe_w_down,
                 (norm_out, None, None, F32, False))
    y_prompt = y[:lay.n_ctx].reshape(n_ctx_seq, ctx_len, d)
    y_sample = y[lay.n_ctx:].reshape(n_lat_seq, lat_len, d)
    return (y_prompt, y_sample, ckv_l0, krope_l0, wkv_fwd, wkv_bwd)
```

```python
import functools
import math

import jax
import jax.numpy as jnp
import numpy as np
from jax import lax
from jax.experimental import pallas as pl
from jax.experimental.pallas import tpu as pltpu

F32 = jnp.float32
BF16 = jnp.bfloat16

NORM_EPS = 1e-6
LN_EPS = 1e-5
GN_EPS = 64e-5
QK_NOPE = 128
QK_ROPE = 64
V_DIM = 128
ROPE_THETA = 10000.0
GRID_W = 64
RWKV_N = 64
N_MOD = 6
TOP_K = 2

LANES = 128
SUBLANES = 8
VMEM_LIMIT_BYTES = 56 * 1024 * 1024

WKV_CHUNK = 64
WKV_PAIRS_PER_STEP = 4
WKV_CHUNKS_PER_STEP = 4
MOE_ROWS = 512


def _cp(n_axes, arbitrary_last=False):
    sem = ["parallel"] * n_axes
    if arbitrary_last:
        sem[-1] = "arbitrary"
    return pltpu.CompilerParams(dimension_semantics=tuple(sem), vmem_limit_bytes=VMEM_LIMIT_BYTES)


def _tile(pref, *sizes):
    t = pref
    while any(s % t for s in sizes):
        t //= 2
    assert t >= SUBLANES
    return t


class _Lay:
    def __init__(self, n_ctx_seq, ctx_len, n_lat_seq, lat_len):
        self.n_ctx_seq, self.ctx_len, self.n_lat_seq, self.lat_len = n_ctx_seq, ctx_len, n_lat_seq, lat_len
        self.n_ctx = n_ctx_seq * ctx_len
        self.n_tok = self.n_ctx + n_lat_seq * lat_len

    def tile(self, pref):
        return _tile(pref, self.n_ctx, self.lat_len)

    def group(self, start):
        return jnp.where(start < self.n_ctx, 0, 1 + (start - self.n_ctx) // self.lat_len)

    def seq_pos(self, start):
        is_ctx = start < self.n_ctx
        pos = jnp.where(is_ctx, start % self.ctx_len, (start - self.n_ctx) % self.lat_len)
        return pos, jnp.where(is_ctx, self.ctx_len, self.lat_len)


def _silu(x):
    return x * jax.nn.sigmoid(x)


def _gelu_tanh(x):
    return 0.5 * x * (1.0 + jnp.tanh(math.sqrt(2.0 / math.pi) * (x + 0.044715 * (x * x * x))))


def _softplus(x):
    return jnp.maximum(x, 0.0) + jnp.log(1.0 + jnp.exp(-jnp.abs(x)))


def _split_bf16(x, n):
    parts = []
    r = x
    for _ in range(n):
        h = r.astype(BF16)
        parts.append(h)
        r = r - h.astype(F32)
    return parts


def _dot(a, b):
    return jnp.dot(a, b, preferred_element_type=F32)


def _dot_nt(a, b):
    return lax.dot_general(a, b, (((1,), (1,)), ((), ())), preferred_element_type=F32)


def _dot_tn(a, b):
    return lax.dot_general(a, b, (((0,), (0,)), ((), ())), preferred_element_type=F32)


def _pair_ones():
    r = lax.broadcasted_iota(jnp.int32, (LANES, LANES), 0) // RWKV_N
    c = lax.broadcasted_iota(jnp.int32, (LANES, LANES), 1) // RWKV_N
    return jnp.where(r == c, 1.0, 0.0).astype(BF16)


def _head_sum(x):
    ones = _pair_ones()
    cols = []
    for t in range(x.shape[1] // LANES):
        xt = x[:, t * LANES:(t + 1) * LANES]
        cols.append(sum(_dot(p, ones) for p in _split_bf16(xt, 3)))
    return jnp.concatenate(cols, axis=1)


def _mm(a, ws, *, tm, tn, gn, epi, outs, vecs=(), wcols=None, n_outer=False, name="mm"):
    m, k = a.shape
    gm = m // tm
    nw, nv = len(ws), len(vecs)
    wcols = wcols or [lambda j: j] * nw
    if n_outer:
        grid = (gn, gm)
        wrap = lambda fn: (lambda g0, g1: fn(g1, g0))
    else:
        grid = (gm, gn)
        wrap = lambda fn: (lambda g0, g1: fn(g0, g1))
    in_specs = [pl.BlockSpec((tm, k), wrap(lambda i, j: (i, 0)))]
    for wc in wcols:
        in_specs.append(pl.BlockSpec((k, tn), wrap(lambda i, j, wc=wc: (0, wc(j)))))
    for _, bshape, fn in vecs:
        in_specs.append(pl.BlockSpec(bshape, wrap(fn)))
    out_shape = [jax.ShapeDtypeStruct(s, d) for s, d, _, _ in outs]
    out_specs = [pl.BlockSpec(b, wrap(fn)) for _, _, b, fn in outs]

    def body(*refs):
        a_ref = refs[0]
        w_refs = refs[1:1 + nw]
        v_refs = refs[1 + nw:1 + nw + nv]
        o_refs = refs[1 + nw + nv:]
        av = a_ref[...]
        if av.dtype != BF16:
            av = av.astype(BF16)
        accs = [_dot(av, w[...].astype(BF16)) for w in w_refs]
        for o, r in zip(o_refs, epi(accs, v_refs)):
            o[...] = r.reshape(o.shape).astype(o.dtype)

    res = pl.pallas_call(body, grid=grid, in_specs=in_specs, out_specs=out_specs, out_shape=out_shape,
                         compiler_params=_cp(2), name=name)(a, *ws, *[v[0] for v in vecs])
    return res


def _bias_vec(b, tn):
    return (b.reshape(1, -1), (1, tn), lambda i, j: (0, j))


def _mm_act(a, w, bias, act, out_dtype, *, tm, tn, name, n_outer=False):
    m, n = a.shape[0], w.shape[1]
    vecs = [] if bias is None else [_bias_vec(bias, tn)]

    def epi(accs, v):
        x = accs[0]
        if bias is not None:
            x = x + v[0][...]
        return [act(x) if act is not None else x]

    return _mm(a, [w], tm=tm, tn=tn, gn=pl.cdiv(n, tn), epi=epi, vecs=vecs, n_outer=n_outer, name=name,
               outs=[((m, n), out_dtype, (tm, tn), lambda i, j: (i, j))])[0]


def _mm_dual(a, w1, w2, b1, b2, fn, out_dtype, *, n, col2, tm, tn, name, n_outer=True):
    m = a.shape[0]
    gn = pl.cdiv(n, tn)
    vecs = []
    if b1 is not None:
        vecs = [(b1.reshape(1, -1), (1, tn), lambda i, j: (0, j)),
                (b2.reshape(1, -1), (1, tn), lambda i, j: (0, j + col2))]

    def epi(accs, v):
        x, y = accs
        if b1 is not None:
            x, y = x + v[0][...], y + v[1][...]
        return [fn(x, y)]

    return _mm(a, [w1, w2], tm=tm, tn=tn, gn=gn, epi=epi, vecs=vecs, n_outer=n_outer, name=name,
               wcols=[lambda j: j, lambda j: j + col2],
               outs=[((m, n), out_dtype, (tm, tn), lambda i, j: (i, j))])[0]


def _mm_resid(a, w, bias, resid, mod3, gate_row, lay, *, tm, tn, name, n_outer=False):
    m, n = resid.shape
    vecs = [(mod3, (1, N_MOD, tn), lambda i, j: (lay.group(i * tm), 0, j)),
            (resid, (tm, tn), lambda i, j: (i, j))]
    if bias is not None:
        vecs.append(_bias_vec(bias, tn))

    def epi(accs, v):
        x = accs[0]
        if bias is not None:
            x = x + v[2][...]
        return [v[1][...] + v[0][0, gate_row:gate_row + 1, :] * x]

    return _mm(a, [w], tm=tm, tn=tn, gn=n // tn, epi=epi, vecs=vecs, n_outer=n_outer, name=name,
               outs=[((m, n), F32, (tm, tn), lambda i, j: (i, j))])[0]


def _ada_mod(cond8, w_mod, b_mod):
    d, n = w_mod.shape
    tn = _tile(1536, n)

    def body(c_ref, w_ref, b_ref, o_ref):
        c = c_ref[...]
        o_ref[...] = _dot(_silu(c).astype(BF16), w_ref[...].astype(BF16)) + b_ref[...]

    m = pl.pallas_call(
        body, grid=(n // tn,),
        in_specs=[pl.BlockSpec((SUBLANES, d), lambda j: (0, 0)), pl.BlockSpec((d, tn), lambda j: (0, j)),
                  pl.BlockSpec((1, tn), lambda j: (0, j))],
        out_specs=pl.BlockSpec((SUBLANES, tn), lambda j: (0, j)),
        out_shape=jax.ShapeDtypeStruct((SUBLANES, n), F32), compiler_params=_cp(1), name="ada_mod",
    )(cond8, w_mod, b_mod.reshape(1, n))
    return m.reshape(SUBLANES, N_MOD, d)


def _norm_rows(pre_fn, pre_ops, gain, mod3, rows, lay, out_dtype, *, m, d, emit_x=False, router=None,
               name="norm_mod"):
    tm = lay.tile(256)
    n_e = None if router is None else router[3]
    n_pre = len(pre_ops)

    def body(*refs):
        g_ref = refs[n_pre]
        pos = n_pre + 1
        xf = pre_fn(*refs[:n_pre])
        y = xf * lax.rsqrt(jnp.mean(xf * xf, axis=-1, keepdims=True) + NORM_EPS) * g_ref[...]
        if rows is not None:
            mod_ref = refs[pos]
            pos += 1
            y = y * (1.0 + mod_ref[0, rows[1]:rows[1] + 1, :]) + mod_ref[0, rows[0]:rows[0] + 1, :]
        if router is not None:
            whi_ref, wlo_ref, rb_ref = refs[pos:pos + 3]
            pos += 3
        if emit_x:
            refs[pos][...] = xf
            pos += 1
        o_ref = refs[pos]
        o_ref[...] = y.astype(o_ref.dtype)
        if router is not None:
            e_ref, p_ref = refs[pos + 1], refs[pos + 2]
            y_hi, y_lo = _split_bf16(y, 2)
            logits = _dot(y_hi, whi_ref[...]) + _dot(y_lo, whi_ref[...]) + _dot(y_hi, wlo_ref[...]) + rb_ref[...]
            lane = lax.broadcasted_iota(jnp.int32, logits.shape, 1)
            logits = jnp.where(lane < n_e, logits, -jnp.inf)
            m1 = jnp.max(logits, axis=-1, keepdims=True)
            i1 = jnp.min(jnp.where(logits == m1, lane, LANES), axis=-1, keepdims=True)
            rest = jnp.where(lane == i1, -jnp.inf, logits)
            m2 = jnp.max(rest, axis=-1, keepdims=True)
            i2 = jnp.min(jnp.where(rest == m2, lane, LANES), axis=-1, keepdims=True)
            e2 = jnp.exp(m2 - m1)
            p1 = 1.0 / (1.0 + e2)
            e_ref[...] = jnp.where(lane == 0, i1, jnp.where(lane == 1, i2, 0))
            p_ref[...] = jnp.where(lane == 0, p1, jnp.where(lane == 1, e2 * p1, 0.0))

    in_specs = [pl.BlockSpec(bshape, fn) for _, bshape, fn in pre_ops] + [pl.BlockSpec((1, d), lambda i: (0, 0))]
    args = [op[0] for op in pre_ops] + [gain.reshape(1, d)]
    if rows is not None:
        in_specs.append(pl.BlockSpec((1, N_MOD, d), lambda i: (lay.group(i * tm), 0, 0)))
        args.append(mod3)
    row = pl.BlockSpec((tm, d), lambda i: (i, 0))
    out_shape = ([jax.ShapeDtypeStruct((m, d), F32)] if emit_x else []) + [jax.ShapeDtypeStruct((m, d), out_dtype)]
    out_specs = [row] * len(out_shape)
    if router is not None:
        in_specs += [pl.BlockSpec((d, LANES), lambda i: (0, 0)), pl.BlockSpec((d, LANES), lambda i: (0, 0)),
                     pl.BlockSpec((1, LANES), lambda i: (0, 0))]
        args += list(router[:3])
        out_shape += [jax.ShapeDtypeStruct((m, LANES), jnp.int32), jax.ShapeDtypeStruct((m, LANES), F32)]
        out_specs += [pl.BlockSpec((tm, LANES), lambda i: (i, 0))] * 2
    res = pl.pallas_call(body, grid=(m // tm,), in_specs=in_specs, out_specs=out_specs, out_shape=out_shape,
                         compiler_params=_cp(1), name=name)(*args)
    return res if len(res) > 1 else res[0]


def _norm_mod(x, gain, mod3, rows, lay, out_dtype, *, name="norm_mod"):
    m, d = x.shape
    tm = lay.tile(256)
    return _norm_rows(lambda x_ref: x_ref[...], [(x, (tm, d), lambda i: (i, 0))], gain, mod3, rows, lay, out_dtype,
                      m=m, d=d, name=name)


def _out_norm(a, w, bias, resid, mod3, lay, gain, out_dtype, *, router=None, name, a_fn=None):
    m, d = resid.shape
    k = w.shape[0]
    tm = lay.tile(256)
    has_bias = bias is not None
    if a_fn is None:
        a, a_fn = [(a, tm)], lambda a_ref: a_ref[...]
    n_a = len(a)

    def pre(*refs):
        w_ref, mod_ref, r_ref = refs[n_a:n_a + 3]
        y = _dot(a_fn(*refs[:n_a]), w_ref[...])
        if has_bias:
            y = y + refs[n_a + 3][...]
        return r_ref[...] + mod_ref[0, 2:3, :] * y

    ops = [(arr, (rows, arr.shape[1]), (lambda i: (i, 0)) if rows == tm else (lambda i: (0, 0))) for arr, rows in a]
    ops += [(w.astype(BF16), (k, d), lambda i: (0, 0)),
            (mod3, (1, N_MOD, d), lambda i: (lay.group(i * tm), 0, 0)), (resid, (tm, d), lambda i: (i, 0))]
    if has_bias:
        ops.append((bias.reshape(1, d), (1, d), lambda i: (0, 0)))
    return _norm_rows(pre, ops, gain, mod3, (3, 4), lay, out_dtype, m=m, d=d, emit_x=True, router=router, name=name)


def _dense_ffn(x, h, w_gate, w_up, w_down, mod3, lay):
    d_ff = w_gate.shape[1]
    hid = _mm_dual(h, w_gate, w_up, None, None, lambda g, u: _silu(g) * u, BF16, n=d_ff, col2=0,
                   tm=lay.tile(1024), tn=512, name="ffn_up")
    return _mm_resid(hid, w_down, None, x, mod3, 5, lay, tm=lay.tile(512), tn=512, n_outer=True,
                     name="ffn_down")


def _moe_up(xs, w_gate, w_up, block_e, block_src, n_used, *, tm):
    n_slots, d = xs.shape
    n_e, _, d_ff = w_gate.shape
    tn = _tile(1024, d_ff)
    n_blocks = n_slots // tm

    def body(be_ref, bs_ref, nu_ref, x_ref, wg_ref, wu_ref, o_ref):
        @pl.when(pl.program_id(1) < nu_ref[0])
        def _():
            xv = x_ref[...]
            g = _dot(xv, wg_ref[...].astype(BF16))
            u = _dot(xv, wu_ref[...].astype(BF16))
            o_ref[...] = (_silu(g) * u).astype(o_ref.dtype)

        @pl.when(pl.program_id(1) >= nu_ref[0])
        def _():
            o_ref[...] = jnp.zeros(o_ref.shape, o_ref.dtype)

    grid_spec = pltpu.PrefetchScalarGridSpec(
        num_scalar_prefetch=3, grid=(d_ff // tn, n_blocks),
        in_specs=[pl.BlockSpec((tm, d), lambda j, i, be, bs, nu: (bs[i], 0)),
                  pl.BlockSpec((None, d, tn), lambda j, i, be, bs, nu: (be[i], 0, j)),
                  pl.BlockSpec((None, d, tn), lambda j, i, be, bs, nu: (be[i], 0, j))],
        out_specs=pl.BlockSpec((tm, tn), lambda j, i, be, bs, nu: (i, j)))
    return pl.pallas_call(body, grid_spec=grid_spec, out_shape=jax.ShapeDtypeStruct((n_slots, d_ff), BF16),
                          compiler_params=_cp(2, arbitrary_last=True), name="moe_up"
                          )(block_e, block_src, n_used, xs, w_gate, w_up)


def _moe_down(hid, w_down, block_e, block_src, n_used, *, tm):
    n_slots, d_ff = hid.shape
    d = w_down.shape[2]
    tn = _tile(512, d)
    n_blocks = n_slots // tm

    def body(be_ref, bs_ref, nu_ref, h_ref, w_ref, o_ref):
        @pl.when(pl.program_id(1) < nu_ref[0])
        def _():
            o_ref[...] = _dot(h_ref[...], w_ref[...].astype(BF16)).astype(o_ref.dtype)

        @pl.when(pl.program_id(1) >= nu_ref[0])
        def _():
            o_ref[...] = jnp.zeros(o_ref.shape, o_ref.dtype)

    grid_spec = pltpu.PrefetchScalarGridSpec(
        num_scalar_prefetch=3, grid=(d // tn, n_blocks),
        in_specs=[pl.BlockSpec((tm, d_ff), lambda j, i, be, bs, nu: (bs[i], 0)),
                  pl.BlockSpec((None, d_ff, tn), lambda j, i, be, bs, nu: (be[i], 0, j))],
        out_specs=pl.BlockSpec((tm, tn), lambda j, i, be, bs, nu: (i, j)))
    return pl.pallas_call(body, grid_spec=grid_spec, out_shape=jax.ShapeDtypeStruct((n_slots, d), BF16),
                          compiler_params=_cp(2, arbitrary_last=True), name="moe_down"
                          )(block_e, block_src, n_used, hid, w_down)


def _moe_combine(x, y0, y1, top_p, mod3, lay, nxt):
    m, d = x.shape
    tm = lay.tile(256)
    gain, next_mod3, rows, out_dtype, emit_x = nxt

    def pre(x_ref, a_ref, b_ref, p_ref, mod_ref):
        mix = p_ref[:, 0:1] * a_ref[...].astype(F32) + p_ref[:, 1:2] * b_ref[...].astype(F32)
        return x_ref[...] + mod_ref[0, 5:6, :] * mix

    row = lambda arr: (arr, (tm, d), lambda i: (i, 0))
    ops = [row(x), row(y0), row(y1), (top_p, (tm, LANES), lambda i: (i, 0)),
           (mod3, (1, N_MOD, d), lambda i: (lay.group(i * tm), 0, 0))]
    return _norm_rows(pre, ops, gain, next_mod3, rows, lay, out_dtype, m=m, d=d, emit_x=emit_x,
                      name="moe_combine")


def _router_ops(w_router, b_router):
    d, n_e = w_router.shape
    wr = jnp.zeros((d, LANES), F32).at[:, :n_e].set(w_router)
    wr_hi = wr.astype(BF16)
    wr_lo = (wr - wr_hi.astype(F32)).astype(BF16)
    rb = jnp.zeros((1, LANES), F32).at[0, :n_e].set(b_router.astype(F32))
    return wr_hi, wr_lo, rb, n_e


def _moe_ffn(x, h, top_e, top_p, mod3, lay, n_e, w_gate, w_up, w_down, nxt):
    n_tok, d = x.shape
    tm = MOE_ROWS
    nk = n_tok * TOP_K
    flat_e = top_e[:, :TOP_K].reshape(nk)
    onehot = (flat_e[:, None] == jnp.arange(n_e, dtype=jnp.int32)[None, :]).astype(jnp.int32)
    rank = jnp.take_along_axis(jnp.cumsum(onehot, axis=0) - onehot, flat_e[:, None], axis=1)[:, 0]
    counts = jnp.sum(onehot, axis=0)
    padded = (counts + tm - 1) // tm * tm
    pad_end = jnp.cumsum(padded)
    dest = (pad_end - padded)[flat_e] + rank
    n_blocks = -(-nk // tm) + n_e
    n_slots = n_blocks * tm
    slot_tok = jnp.zeros((n_slots,), jnp.int32).at[dest].set(jnp.arange(nk, dtype=jnp.int32) // TOP_K)
    n_used = (pad_end[-1] // tm).astype(jnp.int32)
    blk = jnp.arange(n_blocks, dtype=jnp.int32)
    block_src = jnp.minimum(blk, n_used - 1)
    block_e = jnp.minimum(jnp.searchsorted(pad_end, block_src * tm, side="right"), n_e - 1).astype(jnp.int32)
    xs = jnp.take(h, slot_tok, axis=0, mode="clip")
    hid = _moe_up(xs, w_gate, w_up, block_e, block_src, n_used.reshape(1), tm=tm)
    ys = _moe_down(hid, w_down, block_e, block_src, n_used.reshape(1), tm=tm)
    pos = dest.reshape(n_tok, TOP_K)
    return _moe_combine(x, jnp.take(ys, pos[:, 0], axis=0, mode="clip"),
                        jnp.take(ys, pos[:, 1], axis=0, mode="clip"), top_p, mod3, lay, nxt)


def _mla_mid(dq, q_norm, kv_norm, cos64, sin64, q_lora, kv_lora, lay):
    m, n = dq.shape
    tm = lay.tile(512)
    o_kd, o_ko, o_ks = q_lora + kv_lora, q_lora + kv_lora + QK_ROPE, q_lora + kv_lora + 2 * QK_ROPE

    def body(x_ref, qg_ref, kg_ref, cos_ref, sin_ref, cq_ref, ckv_ref, kro_ref, krr_ref):
        cq = x_ref[:, 0:q_lora]
        cq_ref[...] = (cq * lax.rsqrt(jnp.mean(cq * cq, axis=-1, keepdims=True) + NORM_EPS)
                       * qg_ref[...]).astype(cq_ref.dtype)
        ckv = x_ref[:, q_lora:q_lora + kv_lora]
        ckv_ref[...] = ckv * lax.rsqrt(jnp.mean(ckv * ckv, axis=-1, keepdims=True) + NORM_EPS) * kg_ref[...]
        kro_ref[...] = x_ref[:, o_ko:o_ko + QK_ROPE]
        krr_ref[...] = (x_ref[:, o_kd:o_kd + QK_ROPE] * cos_ref[...]
                        + x_ref[:, o_ks:o_ks + QK_ROPE] * sin_ref[...]).astype(krr_ref.dtype)

    rope = pl.BlockSpec((tm, QK_ROPE), lambda i: (i, 0))
    return pl.pallas_call(
        body, grid=(m // tm,),
        in_specs=[pl.BlockSpec((tm, n), lambda i: (i, 0)), pl.BlockSpec((1, q_lora), lambda i: (0, 0)),
                  pl.BlockSpec((1, kv_lora), lambda i: (0, 0)), rope, rope],
        out_specs=[pl.BlockSpec((tm, q_lora), lambda i: (i, 0)), pl.BlockSpec((tm, kv_lora), lambda i: (i, 0)),
                   rope, rope],
        out_shape=[jax.ShapeDtypeStruct((m, q_lora), BF16), jax.ShapeDtypeStruct((m, kv_lora), F32),
                   jax.ShapeDtypeStruct((m, QK_ROPE), F32), jax.ShapeDtypeStruct((m, QK_ROPE), BF16)],
        compiler_params=_cp(1), name="mla_mid",
    )(dq, q_norm.reshape(1, -1), kv_norm.reshape(1, -1), cos64, sin64)


def _mla_q(cq, w_uq_ext, cos64, sin64, n_heads, tm):
    m = cq.shape[0]
    hw = QK_NOPE + 2 * QK_ROPE
    dqk = QK_NOPE + QK_ROPE
    scale = dqk ** -0.5

    def epi(accs, v):
        x = accs[0]
        rot = x[:, QK_NOPE:dqk] * v[0][...] + x[:, dqk:hw] * v[1][...]
        return [jnp.concatenate([x[:, :QK_NOPE], rot], axis=1) * scale]

    rope = lambda arr: (arr, (tm, QK_ROPE), lambda i, j: (i, 0))
    return _mm(cq, [w_uq_ext], tm=tm, tn=hw, gn=n_heads, epi=epi, vecs=[rope(cos64), rope(sin64)], name="mla_q",
               outs=[((n_heads, m, dqk), BF16, (1, tm, dqk), lambda i, j: (j, i, 0))])[0]


def _mla_kv(ckv, w_ukv, krope, n_heads, tm):
    m = ckv.shape[0]
    dqk = QK_NOPE + QK_ROPE

    def epi(accs, v):
        x = accs[0]
        return [jnp.concatenate([x[:, :QK_NOPE], v[0][...].astype(F32)], axis=1), x[:, QK_NOPE:]]

    return _mm(ckv, [w_ukv], tm=tm, tn=QK_NOPE + V_DIM, gn=n_heads, epi=epi, name="mla_kv",
               vecs=[(krope, (tm, QK_ROPE), lambda i, j: (i, 0))],
               outs=[((n_heads, m, dqk), BF16, (1, tm, dqk), lambda i, j: (j, i, 0)),
                     ((n_heads, m, V_DIM), BF16, (1, tm, V_DIM), lambda i, j: (j, i, 0))])


def _attn_context(q, k, v, lay):
    n_heads = q.shape[0]
    t = lay.ctx_len
    dqk = q.shape[2]

    def body(q_ref, k_ref, v_ref, o_ref):
        for h in range(n_heads):
            s = _dot_nt(q_ref[h], k_ref[h])
            p = jnp.exp(s - jnp.max(s, axis=-1, keepdims=True))
            l = jnp.sum(p, axis=-1, keepdims=True)
            o = _dot(p.astype(BF16), v_ref[h]) / l
            o_ref[:, h * V_DIM:(h + 1) * V_DIM] = o.astype(o_ref.dtype)

    return pl.pallas_call(
        body, grid=(lay.n_ctx_seq,),
        in_specs=[pl.BlockSpec((n_heads, t, dqk), lambda b: (0, b, 0)),
                  pl.BlockSpec((n_heads, t, dqk), lambda b: (0, b, 0)),
                  pl.BlockSpec((n_heads, t, V_DIM), lambda b: (0, b, 0))],
        out_specs=pl.BlockSpec((t, n_heads * V_DIM), lambda b: (b, 0)),
        out_shape=jax.ShapeDtypeStruct((lay.n_ctx, n_heads * V_DIM), BF16),
        compiler_params=_cp(1), name="attn_context",
    )(q, k, v)


def _attn_latent(q, k, v, kc, vc, lay, past_len):
    n_heads = q.shape[0]
    t = lay.lat_len
    dqk = q.shape[2]
    hp = 2 if n_heads % 2 == 0 else 1
    tq = _tile(256, t)
    lat0 = lay.n_ctx // t
    q0 = lay.n_ctx // tq
    kch = _tile(1024, t)
    n_ch = t // kch

    def body(q_ref, k_ref, v_ref, kc_ref, vc_ref, o_ref):
        def scores(h):
            qv = q_ref[h]
            return ([_dot_nt(qv, k_ref[h, j * kch:(j + 1) * kch, :]) for j in range(n_ch)]
                    + [_dot_nt(qv, kc_ref[h])])

        def row_max(s):
            return functools.reduce(jnp.maximum, [jnp.max(x, axis=-1, keepdims=True) for x in s])

        def probs(s, mx):
            p = [jnp.exp(x - mx) for x in s]
            l = functools.reduce(jnp.add, [jnp.sum(x, axis=-1, keepdims=True) for x in p])
            return [x.astype(BF16) for x in p], l

        def weighted(h, p):
            vs = [v_ref[h, j * kch:(j + 1) * kch, :] for j in range(n_ch)] + [vc_ref[h]]
            return functools.reduce(jnp.add, [_dot(x, vv) for x, vv in zip(p, vs)])

        s = [scores(h) for h in range(hp)]
        outs = []
        p_prev = None
        for h in range(hp):
            p, l = probs(s[h], row_max(s[h]))
            if p_prev is not None:
                outs.append(weighted(h - 1, p_prev[0]) / p_prev[1])
            p_prev = (p, l)
        outs.append(weighted(hp - 1, p_prev[0]) / p_prev[1])
        o_ref[...] = jnp.concatenate(outs, axis=1).astype(o_ref.dtype)

    return pl.pallas_call(
        body, grid=(lay.n_lat_seq, n_heads // hp, t // tq),
        in_specs=[pl.BlockSpec((hp, tq, dqk), lambda b, h, i: (h, q0 + b * (t // tq) + i, 0)),
                  pl.BlockSpec((hp, t, dqk), lambda b, h, i: (h, lat0 + b, 0)),
                  pl.BlockSpec((hp, t, V_DIM), lambda b, h, i: (h, lat0 + b, 0)),
                  pl.BlockSpec((hp, past_len, dqk), lambda b, h, i: (h, b, 0)),
                  pl.BlockSpec((hp, past_len, V_DIM), lambda b, h, i: (h, b, 0))],
        out_specs=pl.BlockSpec((tq, hp * V_DIM), lambda b, h, i: (b * (t // tq) + i, h)),
        out_shape=jax.ShapeDtypeStruct((lay.n_lat_seq * t, n_heads * V_DIM), BF16),
        compiler_params=_cp(3), name="attn_latent",
    )(q, k, v, kc, vc)


def _rope_tables(lay):
    n = lay.lat_len
    pairs = QK_ROPE // 4
    row_pos = (jnp.arange(n) // GRID_W).astype(F32)
    col_pos = (jnp.arange(n) % GRID_W).astype(F32)
    inv_freq = ROPE_THETA ** (-jnp.arange(pairs, dtype=F32) / pairs)
    ang = jnp.concatenate([row_pos[:, None] * inv_freq, col_pos[:, None] * inv_freq], axis=-1)
    cos, sin = jnp.cos(ang), jnp.sin(ang)
    cos64 = jnp.concatenate([cos, cos], axis=-1)
    sin64 = jnp.concatenate([-sin, sin], axis=-1)
    one = jnp.ones((lay.n_ctx, QK_ROPE), F32)
    cos_all = jnp.concatenate([one] + [cos64] * lay.n_lat_seq, axis=0)
    sin_all = jnp.concatenate([0.0 * one] + [sin64] * lay.n_lat_seq, axis=0)
    return cos_all, sin_all


def _mla_layer(x, h, mod3, lay, norm_ffn, cache_ckv, cache_krope, w_down, q_norm, kv_norm, w_uq, w_ukv, w_o):
    d = x.shape[1]
    q_lora, kv_lora = q_norm.shape[0], kv_norm.shape[0]
    n_heads = w_o.shape[0] // V_DIM
    dqk = QK_NOPE + QK_ROPE
    perm_d = np.concatenate([np.arange(0, QK_ROPE, 2), np.arange(1, QK_ROPE, 2)])
    perm_s = np.concatenate([np.arange(1, QK_ROPE, 2), np.arange(0, QK_ROPE, 2)])
    kr0 = q_lora + kv_lora
    w_down_ext = jnp.concatenate(
        [w_down[:, :kr0], w_down[:, kr0 + perm_d], w_down[:, kr0:], w_down[:, kr0 + perm_s],
         jnp.zeros((d, QK_ROPE), F32)], axis=1)
    w_uq_h = w_uq.reshape(q_lora, n_heads, dqk)
    w_uq_ext = jnp.concatenate([w_uq_h[:, :, :QK_NOPE], w_uq_h[:, :, QK_NOPE + perm_d],
                                w_uq_h[:, :, QK_NOPE + perm_s]], axis=2).reshape(q_lora, -1)
    cos64, sin64 = _rope_tables(lay)
    tm = lay.tile(1024)
    dq = _mm_act(h, w_down_ext, None, None, F32, tm=tm, tn=512, name="mla_down")
    cq, ckv, krope_raw, krope_rot = _mla_mid(dq, q_norm, kv_norm, cos64, sin64, q_lora, kv_lora, lay)
    q = _mla_q(cq, w_uq_ext, cos64, sin64, n_heads, lay.tile(2048))
    k, v = _mla_kv(ckv, w_ukv, krope_rot, n_heads, lay.tile(2048))
    n_past = cache_ckv.shape[0] * cache_ckv.shape[1]
    kc, vc = _mla_kv(cache_ckv.reshape(n_past, kv_lora), w_ukv,
                     cache_krope.reshape(n_past, QK_ROPE)[:, perm_d].astype(BF16), n_heads,
                     _tile(512, cache_ckv.shape[1]))
    o_ctx = _attn_context(q, k, v, lay)
    o_lat = _attn_latent(q, k, v, kc, vc, lay, cache_ckv.shape[1])
    attn = jnp.concatenate([o_ctx, o_lat], axis=0)
    x, h_ffn = _out_norm(attn, w_o, None, x, mod3, lay, norm_ffn, BF16, name="mla_out")
    ckv_ctx = ckv[:lay.n_ctx].reshape(lay.n_ctx_seq, lay.ctx_len, kv_lora)
    krope_ctx = krope_raw[:lay.n_ctx].reshape(lay.n_ctx_seq, lay.ctx_len, QK_ROPE)
    return x, h_ffn, ckv_ctx, krope_ctx


def _conv_dw(u, w_dw, b_dw, ln_g, ln_b, lay):
    m, d = u.shape
    width = w_dw.shape[0]
    pad = width // 2
    t = lay.tile(256)
    halo = 2 * SUBLANES
    assert pad <= halo
    rb = _tile(64, t)

    sh_rows = t + halo + SUBLANES

    def body(prev_ref, cur_ref, nxt_ref, w_ref, bdw_ref, g_ref, b_ref, o_ref, ext_ref, dw_ref, sh_ref):
        pos, ln = lay.seq_pos(pl.program_id(0) * t)
        ext_ref[0:halo, :] = jnp.where(pos > 0, prev_ref[...], 0.0)
        ext_ref[halo:halo + t, :] = cur_ref[...]
        ext_ref[halo + t:halo + t + halo, :] = jnp.where(pos + t < ln, nxt_ref[...], 0.0)

        def chunk(c, carry):
            ls = pl.ds(pl.multiple_of(c * LANES, LANES), LANES)
            for s in range(SUBLANES):
                sh_ref[s] = ext_ref[pl.ds(s, sh_rows), ls]
            for r in range(t // rb):
                acc = jnp.zeros((rb, LANES), F32)
                for j in range(width):
                    off = halo - pad + j
                    acc = acc + (sh_ref[off % SUBLANES, pl.ds(off // SUBLANES * SUBLANES + r * rb, rb), :]
                                 * w_ref[pl.ds(j, 1), ls])
                dw_ref[pl.ds(r * rb, rb), ls] = acc + bdw_ref[:, ls]
            return carry

        lax.fori_loop(0, d // LANES, chunk, 0)
        xv = dw_ref[...]
        mu = jnp.mean(xv, axis=-1, keepdims=True)
        xc = xv - mu
        var = jnp.mean(xc * xc, axis=-1, keepdims=True)
        o_ref[...] = _silu(xc * lax.rsqrt(var + LN_EPS) * g_ref[...] + b_ref[...]).astype(o_ref.dtype)

    hb = t // halo
    vec = pl.BlockSpec((1, d), lambda i: (0, 0))
    return pl.pallas_call(
        body, grid=(m // t,),
        in_specs=[pl.BlockSpec((halo, d), lambda i: (jnp.maximum(i * hb - 1, 0), 0)),
                  pl.BlockSpec((t, d), lambda i: (i, 0)),
                  pl.BlockSpec((halo, d), lambda i: (jnp.minimum((i + 1) * hb, m // halo - 1), 0)),
                  pl.BlockSpec((width, d), lambda i: (0, 0)), vec, vec, vec],
        out_specs=pl.BlockSpec((t, d), lambda i: (i, 0)),
        out_shape=jax.ShapeDtypeStruct((m, d), BF16),
        scratch_shapes=[pltpu.VMEM((t + 2 * halo, d), F32), pltpu.VMEM((t, d), F32),
                        pltpu.VMEM((SUBLANES, sh_rows, LANES), F32)],
        compiler_params=_cp(1), name="conv_dw",
    )(u, u, u, w_dw, b_dw.reshape(1, d), ln_g.reshape(1, d), ln_b.reshape(1, d))


def _conformer_layer(x, h, mod3, lay, norm_ffn, router, w_in, b_in, w_dw, b_dw, ln_g, ln_b, w_out, b_out):
    d = x.shape[1]
    tm = lay.tile(1024)
    u = _mm_dual(h, w_in, w_in, b_in, b_in, lambda a, b: a * jax.nn.sigmoid(b), F32, n=d, col2=d // 512,
                 tm=tm, tn=512, name="conv_in")
    cv = _conv_dw(u, w_dw, b_dw, ln_g, ln_b, lay)
    return _out_norm(cv, w_out, b_out, x, mod3, lay, norm_ffn, BF16, router=router, name="conv_out")


def _sgu_mix(z, ln_g, ln_b, w_s, b_s, lay):
    m, d2 = z.shape
    d = d2 // 2
    groups, chunk, _ = w_s.shape
    gd = d // groups
    tm = lay.tile(2 * chunk)
    bs_t = jnp.transpose(b_s)

    def body(u_ref, v_ref, g_ref, b_ref, ws_ref, bs_ref, o_ref):
        xv = v_ref[...].astype(F32)
        mu = jnp.mean(xv, axis=-1, keepdims=True)
        xc = xv - mu
        var = jnp.mean(xc * xc, axis=-1, keepdims=True)
        vn =(xc * lax.rsqrt(var + LN_EPS) * g_ref[...] + b_ref[...]).astype(BF16)
        for c in range(tm // chunk):
            rs = slice(c * chunk, (c + 1) * chunk)
            for g in range(groups):
                ls = slice(g * gd, (g + 1) * gd)
                mixed = _dot(ws_ref[g].astype(BF16), vn[rs, ls]) + bs_ref[:, g:g + 1]
                o_ref[rs, ls] = (u_ref[rs, ls].astype(F32) * mixed).astype(o_ref.dtype)

    vec = pl.BlockSpec((1, d), lambda i: (0, 0))
    return pl.pallas_call(
        body, grid=(m // tm,),
        in_specs=[pl.BlockSpec((tm, d), lambda i: (i, 0)), pl.BlockSpec((tm, d), lambda i: (i, 1)), vec, vec,
                  pl.BlockSpec((groups, chunk, chunk), lambda i: (0, 0, 0)),
                  pl.BlockSpec((chunk, groups), lambda i: (0, 0))],
        out_specs=pl.BlockSpec((tm, d), lambda i: (i, 0)),
        out_shape=jax.ShapeDtypeStruct((m, d), BF16), compiler_params=_cp(1), name="sgu_mix",
    )(z, z, ln_g.reshape(1, d), ln_b.reshape(1, d), w_s, bs_t)


def _sgu_layer(x, h, mod3, lay, norm_ffn, w_in, b_in, ln_g, ln_b, w_s, b_s, w_out, b_out):
    tm = lay.tile(1024)
    d = x.shape[1]
    z = _mm_act(h, w_in, b_in, _gelu_tanh, BF16, tm=lay.tile(512), tn=_tile(1024, d), name="sgu_in", n_outer=True)
    sm = _sgu_mix(z, ln_g, ln_b, w_s, b_s, lay)
    return _out_norm(sm, w_out, b_out, x, mod3, lay, norm_ffn, BF16, name="sgu_out")


def _shift_mix(h, mu, lay):
    m, d = h.shape
    t = lay.tile(256)
    halo = SUBLANES

    def body(prev_ref, cur_ref, nxt_ref, mu_ref, *rest):
        o_refs, ext_ref = rest[:6], rest[6]
        pos, ln = lay.seq_pos(pl.program_id(0) * t)
        ext_ref[0:halo, :] = jnp.where(pos > 0, prev_ref[...], 0.0)
        ext_ref[halo:halo + t, :] = cur_ref[...]
        ext_ref[halo + t:halo + t + halo, :] = jnp.where(pos + t < ln, nxt_ref[...], 0.0)
        xv = cur_ref[...]
        xx = 0.5 * (ext_ref[pl.ds(halo - 1, t), :] + ext_ref[pl.ds(halo + 1, t), :]) - xv
        for j in range(6):
            o_refs[j][...] = (xv + xx * mu_ref[j:j + 1, :]).astype(BF16)

    hb = t // halo
    row = pl.BlockSpec((t, d), lambda i: (i, 0))
    return pl.pallas_call(
        body, grid=(m // t,),
        in_specs=[pl.BlockSpec((halo, d), lambda i: (jnp.maximum(i * hb - 1, 0), 0)), row,
                  pl.BlockSpec((halo, d), lambda i: (jnp.minimum((i + 1) * hb, m // halo - 1), 0)),
                  pl.BlockSpec((6, d), lambda i: (0, 0))],
        out_specs=[row] * 6, out_shape=[jax.ShapeDtypeStruct((m, d), BF16)] * 6,
        scratch_shapes=[pltpu.VMEM((t + 2 * halo, d), F32)],
        compiler_params=_cp(1), name="rwkv_shift_mix",
    )(h, h, h, mu)


def _wkv_prep(r, k, v, wl, al, k_k, k_a, r_k):
    m, d = r.shape
    tm = _tile(128, m)

    def body(r_ref, k_ref, v_ref, wl0_ref, wl1_ref, al0_ref, al1_ref, kk_ref, ka_ref, rk_ref,
             okk_ref, olw0_ref, olw1_ref, ob0_ref, ob1_ref, okt0_ref, okt1_ref, obon_ref):
        rv, kv, vv = r_ref[...], k_ref[...], v_ref[...]
        kk = kv * kk_ref[...]
        kk = kk / jnp.maximum(jnp.sqrt(_head_sum(kk * kk)), 1e-12)
        okk_ref[...] = kk.astype(okk_ref.dtype)
        bonus = jnp.zeros_like(rv)
        for wl_ref, al_ref, olw_ref, ob_ref, okt_ref in ((wl0_ref, al0_ref, olw0_ref, ob0_ref, okt0_ref),
                                                          (wl1_ref, al1_ref, olw1_ref, ob1_ref, okt1_ref)):
            w_log = -_softplus(-wl_ref[...]) - 0.5
            olw_ref[...] = -jnp.exp(w_log)
            a = jax.nn.sigmoid(al_ref[...])
            kt = kv * (1.0 + (a - 1.0) * ka_ref[...])
            ob_ref[...] = (kk * a).astype(ob_ref.dtype)
            okt_ref[...] = kt.astype(okt_ref.dtype)
            bonus = bonus + _head_sum(rv * kt * rk_ref[...]) * vv
        obon_ref[...] = bonus

    row = pl.BlockSpec((tm, d), lambda i: (i, 0))
    vec = pl.BlockSpec((1, d), lambda i: (0, 0))
    f32 = jax.ShapeDtypeStruct((m, d), F32)
    bf16 = jax.ShapeDtypeStruct((m, d), BF16)
    kk, lw0, lw1, b0, b1, kt0, kt1, bonus = pl.pallas_call(
        body, grid=(m // tm,),
        in_specs=[row] * 7 + [vec] * 3,
        out_specs=[row] * 8,
        out_shape=[bf16, f32, f32, bf16, bf16, bf16, bf16, f32],
        compiler_params=_cp(1), name="wkv_prep",
    )(r, k, v, wl[0], wl[1], al[0], al[1], k_k.reshape(1, d), k_a.reshape(1, d), r_k.reshape(1, d))
    return kk, (lw0, lw1), (b0, b1), (kt0, kt1), bonus


def _wkv_scan_dir(r, kk, v, lw, b, kt, s_init, lay, reverse):
    m, d = r.shape
    c = WKV_CHUNK
    rows = c * WKV_CHUNKS_PER_STEP
    n_pairs = d // LANES
    pps = math.gcd(WKV_PAIRS_PER_STEP, n_pairs)
    pw = pps * LANES
    n_steps = m // rows
    ctx_steps = lay.n_ctx // rows
    sps_ctx, sps_lat = lay.ctx_len // rows, lay.lat_len // rows
    n_seq = lay.n_ctx_seq + lay.n_lat_seq
    c2 = 2 * c
    sgn = -1 if reverse else 1

    def block_of(s):
        return n_steps - 1 - s if reverse else s

    def seq_of(tb):
        is_ctx = tb < ctx_steps
        seq = jnp.where(is_ctx, tb // sps_ctx, lay.n_ctx_seq + (tb - ctx_steps) // sps_lat)
        pos = jnp.where(is_ctx, tb % sps_ctx, (tb - ctx_steps) % sps_lat)
        return is_ctx, seq, pos, jnp.where(is_ctx, sps_ctx, sps_lat)

    def body(r_ref, kk_ref, v_ref, lw_ref, b_ref, kt_ref, si_ref, y_ref, so_ref, s_scr):
        is_ctx, _, pos, sps = seq_of(block_of(pl.program_id(1)))
        first = pos == (sps - 1 if reverse else 0)
        last = pos == (0 if reverse else sps - 1)

        @pl.when(first)
        def _():
            s_scr[...] = jnp.where(is_ctx, 0.0, si_ref[0])

        ri = lax.broadcasted_iota(jnp.int32, (c2, c2), 0)
        ci = lax.broadcasted_iota(jnp.int32, (c2, c2), 1)
        same = (ri // c) == (ci // c)
        strict = same & ((ri - ci) * sgn > 0)
        incl = same & ((ri - ci) * sgn >= 0)
        eye = jnp.where(ri == ci, 1.0, 0.0)
        end_row = 0 if reverse else c - 1
        step_row = lax.broadcasted_iota(jnp.int32, (c, LANES), 0)

        def running_sum(x):
            s = 1
            while s < c:
                if reverse:
                    x = x + jnp.where(step_row < c - s, pltpu.roll(x, c - s, axis=0), 0.0)
                else:
                    x = x + jnp.where(step_row >= s, pltpu.roll(x, s, axis=0), 0.0)
                s *= 2
            return x
        own = (lax.broadcasted_iota(jnp.int32, (c2, LANES), 0) // c
               == lax.broadcasted_iota(jnp.int32, (c2, LANES), 1) // RWKV_N)

        def stack(xv):
            return jnp.where(own, jnp.concatenate([xv, xv], axis=0), 0.0).astype(BF16)

        chunk_order = list(range(WKV_CHUNKS_PER_STEP))[::sgn]
        probs = [(j, p) for j in chunk_order for p in range(pps)]
        at = lambda ref, j, p: ref[j * c:(j + 1) * c, p * LANES:(p + 1) * LANES].astype(F32)

        lwv = [at(lw_ref, j, p) for j, p in probs]
        cs = [running_sum(x) for x in lwv]
        g = [jnp.exp(x) for x in cs]
        ginv = [jnp.exp(-x) for x in cs]
        gprev = [jnp.exp(x - w) for x, w in zip(cs, lwv)]
        gend = [jnp.exp(x[end_row:end_row + 1] - x) for x in cs]
        gall = [jnp.exp(x[end_row:end_row + 1]) for x in cs]
        kkv = [at(kk_ref, j, p) for j, p in probs]
        bv = [at(b_ref, j, p) for j, p in probs]
        ktv = [at(kt_ref, j, p) for j, p in probs]
        n = len(probs)
        xa = [stack(-kkv[i] * gprev[i]) for i in range(n)]
        xr = [stack(at(r_ref, *probs[i]) * g[i]) for i in range(n)]
        xb = [stack(bv[i] * ginv[i]) for i in range(n)]
        xk = [stack(ktv[i] * ginv[i]) for i in range(n)]
        xb_end = [stack(bv[i] * gend[i]) for i in range(n)]
        xk_end = [stack(ktv[i] * gend[i]) for i in range(n)]
        xv = [stack(at(v_ref, *probs[i])) for i in range(n)]
        mm = [_dot_nt(jnp.concatenate([xa[i], xr[i]], axis=0), jnp.concatenate([xb[i], xk[i]], axis=0))
              for i in range(n)]
        m_ab = [jnp.where(strict, x[:c2, :c2], 0.0) for x in mm]
        m_ak = [jnp.where(strict, x[:c2, c2:], 0.0).astype(BF16) for x in mm]
        m_rb = [jnp.where(incl, x[c2:, :c2], 0.0).astype(BF16) for x in mm]
        m_rk = [jnp.where(incl, x[c2:, c2:], 0.0).astype(BF16) for x in mm]
        akv = [_dot(m_ak[i], xv[i]).astype(BF16) for i in range(n)]
        pj = [x.astype(BF16) for x in m_ab]
        tinv = [eye + x for x in m_ab]
        pj = [_dot(x, x).astype(BF16) for x in pj]
        for _ in range(int(math.log2(c)) - 2):
            both = [_dot(jnp.concatenate([pj[i], tinv[i].astype(BF16)], axis=0), pj[i]) for i in range(n)]
            pj = [x[:c2].astype(BF16) for x in both]
            tinv = [tinv[i] + both[i][c2:] for i in range(n)]
        tinv = [(tinv[i] + _dot(tinv[i].astype(BF16), pj[i])).astype(BF16) for i in range(n)]
        pq = [_dot(tinv[i], jnp.concatenate([xa[i], akv[i]], axis=1)).astype(BF16) for i in range(n)]
        ab = [_dot_tn(pq[i], xb_end[i]) for i in range(n)]
        a_c = [(eye * gall[i] + ab[i][:LANES]).astype(BF16) for i in range(n)]
        b_c = [ab[i][LANES:] + _dot_tn(xv[i], xk_end[i]) for i in range(n)]
        cd = [_dot(m_rb[i], pq[i]) for i in range(n)]
        c_c = [(xr[i].astype(F32) + cd[i][:, :LANES]).astype(BF16) for i in range(n)]
        d_c = [cd[i][:, LANES:] + _dot(m_rk[i], xv[i]) for i in range(n)]
        for p in range(pps):
            s = s_scr[p]
            for i, (j, pp) in enumerate(probs):
                if pp != p:
                    continue
                s_hi, s_lo = _split_bf16(s, 2)
                yy = _dot_nt(c_c[i], s_hi) + _dot_nt(c_c[i], s_lo) + d_c[i]
                y_ref[j * c:(j + 1) * c, p * LANES:(p + 1) * LANES] = yy[:c] + yy[c:]
                s = _dot(s_hi, a_c[i]) + _dot(s_lo, a_c[i]) + b_c[i]
            s_scr[p] = s

        @pl.when(last)
        def _():
            so_ref[0] = s_scr[...]

    tok = pl.BlockSpec((rows, pw), lambda lb, s: (block_of(s), lb))
    st = (1, pps, LANES, LANES)
    return pl.pallas_call(
        body, grid=(n_pairs // pps, n_steps),
        in_specs=[tok] * 6 + [pl.BlockSpec(st, lambda lb, s: (
            jnp.clip(seq_of(block_of(s))[1] - lay.n_ctx_seq, 0, lay.n_lat_seq - 1), lb, 0, 0))],
        out_specs=[tok, pl.BlockSpec(st, lambda lb, s: (seq_of(block_of(s))[1], lb, 0, 0))],
        out_shape=[jax.ShapeDtypeStruct((m, d), F32), jax.ShapeDtypeStruct((n_seq, n_pairs, LANES, LANES), F32)],
        scratch_shapes=[pltpu.VMEM((pps, LANES, LANES), F32)],
        compiler_params=_cp(2, arbitrary_last=True), name="wkv_scan_bwd" if reverse else "wkv_scan_fwd",
    )(r, kk, v, lw, b, kt, s_init)


def _wkv_post(y_fwd, y_bwd, bonus, g, ln_g, ln_b, tm):
    d = bonus.shape[1]

    def tile_fn(yf_ref, yb_ref, bon_ref, g_ref, lg_ref, lb_ref):
        yv = yf_ref[...] + yb_ref[...]
        mu = _head_sum(yv) * (1.0 / RWKV_N)
        yc = yv - mu
        var = _head_sum(yc * yc) * (1.0 / RWKV_N)
        o = yc * lax.rsqrt(var + GN_EPS) * lg_ref[...] + lb_ref[...] + bon_ref[...]
        return (o * g_ref[...]).astype(BF16)

    ops = [(y_fwd, tm), (y_bwd, tm), (bonus, tm), (g, tm), (ln_g.reshape(1, d), 1), (ln_b.reshape(1, d), 1)]
    return ops, tile_fn


def _pad_cols(w, n):
    return jnp.pad(w, ((0, 0), (0, n - w.shape[1])))


def _pad_rows(w, n):
    return jnp.pad(w, ((0, n - w.shape[0]), (0, 0)))


def _rwkv_layer(x, hf, mod3, lay, norm_ffn, router, s_fwd, s_bwd, mu, w_r, w_k, w_v, w_o, w0, w1, w2, a0, a1, a2,
                g1, g2, k_k, k_a, r_k, ln_g, ln_b):
    m, d = x.shape
    tm = lay.tile(1024)
    xr, xw, xk, xv, xa, xg = _shift_mix(hf, mu, lay)
    sq = dict(tm=lay.tile(512), tn=_tile(1024, d), n_outer=True)
    r = _mm_act(xr, w_r, None, None, F32, name="rwkv_r", **sq)
    k = _mm_act(xk, w_k, None, None, F32, name="rwkv_k", **sq)
    v = _mm_act(xv, w_v, None, None, F32, name="rwkv_v", **sq)
    gh = _mm_act(xg, g1, None, jax.nn.sigmoid, BF16, tm=tm, tn=g1.shape[1], name="rwkv_g1")
    g = _mm_act(gh, g2, None, None, F32, name="rwkv_g2", **sq)
    both = lambda w: jnp.concatenate([_pad_cols(w[0], LANES), _pad_cols(w[1], LANES)], axis=1)
    th = _mm_act(xw, both(w1), None, jnp.tanh, BF16, tm=tm, tn=2 * LANES, name="rwkv_w1")
    ah = _mm_act(xa, both(a1), None, None, BF16, tm=tm, tn=2 * LANES, name="rwkv_a1")
    wl, al = [], []
    for dr in range(2):
        cols = slice(dr * LANES, (dr + 1) * LANES)
        wl.append(_mm_act(th[:, cols], _pad_rows(w2[dr], LANES), w0[dr], None, F32, name="rwkv_w2", **sq))
        al.append(_mm_act(ah[:, cols], _pad_rows(a2[dr], LANES), a0[dr], None, F32, name="rwkv_a2", **sq))
    kk, lw, b, kt, bonus = _wkv_prep(r, k, v, wl, al, k_k, k_a, r_k.reshape(-1))
    n_pairs = d // LANES
    ys, finals = [], []
    for dr, s0 in enumerate((s_fwd, s_bwd)):
        st = s0.astype(F32).reshape(-1, n_pairs, 2, RWKV_N, RWKV_N)
        s_init = jnp.zeros((st.shape[0], n_pairs, LANES, LANES), F32)
        s_init = s_init.at[..., :RWKV_N, :RWKV_N].set(st[:, :, 0]).at[..., RWKV_N:, RWKV_N:].set(st[:, :, 1])
        y, s_fin = _wkv_scan_dir(r, kk, v, lw[dr], b[dr], kt[dr], s_init, lay, reverse=dr == 1)
        sf = s_fin[:lay.n_ctx_seq]
        s_pair = jnp.stack([sf[..., :RWKV_N, :RWKV_N], sf[..., RWKV_N:, RWKV_N:]], axis=2)
        ys.append(y)
        finals.append(s_pair.reshape(lay.n_ctx_seq, 2 * n_pairs, RWKV_N, RWKV_N))
    post_ops, post_fn = _wkv_post(ys[0], ys[1], bonus, g, ln_g, ln_b, lay.tile(256))
    x, h_ffn, top_e, top_p = _out_norm(post_ops, w_o, None, x, mod3, lay, norm_ffn, BF16, router=router,
                                       name="rwkv_out", a_fn=post_fn)
    return x, h_ffn, top_e, top_p, finals[0], finals[1]


def kernel(x_prompt, x_sample, cache_ckv_l0, cache_krope_l0, state_wkv_fwd_l3, state_wkv_bwd_l3, c, c_ctx, l0_w_mod, l0_b_mod, l0_norm_mix, l0_norm_ffn, l0_mla_w_down, l0_mla_q_norm, l0_mla_kv_norm, l0_mla_w_uq, l0_mla_w_ukv, l0_mla_w_o, l0_ffn_w_gate, l0_ffn_w_up, l0_ffn_w_down, l1_w_mod, l1_b_mod, l1_norm_mix, l1_norm_ffn, l1_conv_w_in, l1_conv_b_in, l1_conv_w_dw, l1_conv_b_dw, l1_conv_ln_g, l1_conv_ln_b, l1_conv_w_out, l1_conv_b_out, l1_moe_w_router, l1_moe_b_router, l1_moe_w_gate, l1_moe_w_up, l1_moe_w_down, l2_w_mod, l2_b_mod, l2_norm_mix, l2_norm_ffn, l2_sgu_w_in, l2_sgu_b_in, l2_sgu_ln_g, l2_sgu_ln_b, l2_sgu_w_s, l2_sgu_b_s, l2_sgu_w_out, l2_sgu_b_out, l2_ffn_w_gate, l2_ffn_w_up, l2_ffn_w_down, l3_w_mod, l3_b_mod, l3_norm_mix, l3_norm_ffn, l3_rwkv_mu, l3_rwkv_w_r, l3_rwkv_w_k, l3_rwkv_w_v, l3_rwkv_w_o, l3_rwkv_w0, l3_rwkv_w1, l3_rwkv_w2, l3_rwkv_a0, l3_rwkv_a1, l3_rwkv_a2, l3_rwkv_g1, l3_rwkv_g2, l3_rwkv_k_k, l3_rwkv_k_a, l3_rwkv_r_k, l3_rwkv_ln_g, l3_rwkv_ln_b, l3_moe_w_router, l3_moe_b_router, l3_moe_w_gate, l3_moe_w_up, l3_moe_w_down, norm_out):
    n_ctx_seq, ctx_len, d = x_prompt.shape
    n_lat_seq, lat_len, _ = x_sample.shape
    lay = _Lay(n_ctx_seq, ctx_len, n_lat_seq, lat_len)
    assert n_lat_seq + 1 <= SUBLANES
    x = jnp.concatenate([x_prompt.reshape(-1, d), x_sample.reshape(-1, d)], axis=0)
    cond8 = jnp.zeros((SUBLANES, d), F32).at[0].set(c_ctx).at[1:1 + n_lat_seq].set(c)

    mods = [_ada_mod(cond8, w, b) for w, b in ((l0_w_mod, l0_b_mod), (l1_w_mod, l1_b_mod), (l2_w_mod, l2_b_mod),
                                               (l3_w_mod, l3_b_mod))]

    h = _norm_mod(x, l0_norm_mix, mods[0], (0, 1), lay, BF16)
    x, h, ckv_l0, krope_l0 = _mla_layer(x, h, mods[0], lay, l0_norm_ffn, cache_ckv_l0, cache_krope_l0, l0_mla_w_down,
                                        l0_mla_q_norm, l0_mla_kv_norm, l0_mla_w_uq, l0_mla_w_ukv, l0_mla_w_o)
    x = _dense_ffn(x, h, l0_ffn_w_gate, l0_ffn_w_up, l0_ffn_w_down, mods[0], lay)

    h = _norm_mod(x, l1_norm_mix, mods[1], (0, 1), lay, BF16)
    router = _router_ops(l1_moe_w_router, l1_moe_b_router)
    x, h, top_e, top_p = _conformer_layer(x, h, mods[1], lay, l1_norm_ffn, router, l1_conv_w_in, l1_conv_b_in,
                                          l1_conv_w_dw, l1_conv_b_dw, l1_conv_ln_g, l1_conv_ln_b, l1_conv_w_out,
                                          l1_conv_b_out)
    x, h = _moe_ffn(x, h, top_e, top_p, mods[1], lay, router[3], l1_moe_w_gate, l1_moe_w_up, l1_moe_w_down,
                    (l2_norm_mix, mods[2], (0, 1), BF16, True))

    x, h = _sgu_layer(x, h, mods[2], lay, l2_norm_ffn, l2_sgu_w_in, l2_sgu_b_in, l2_sgu_ln_g, l2_sgu_ln_b,
                      l2_sgu_w_s, l2_sgu_b_s, l2_sgu_w_out, l2_sgu_b_out)
    x = _dense_ffn(x, h, l2_ffn_w_gate, l2_ffn_w_up, l2_ffn_w_down, mods[2], lay)

    hf = _norm_mod(x, l3_norm_mix, mods[3], (0, 1), lay, F32)
    router = _router_ops(l3_moe_w_router, l3_moe_b_router)
    x, h, top_e, top_p, wkv_fwd, wkv_bwd = _rwkv_layer(
        x, hf, mods[3], lay, l3_norm_ffn, router, state_wkv_fwd_l3, state_wkv_bwd_l3, l3_rwkv_mu, l3_rwkv_w_r,
        l3_rwkv_w_k, l3_rwkv_w_v, l3_rwkv_w_o, l3_rwkv_w0, l3_rwkv_w1, l3_rwkv_w2, l3_rwkv_a0, l3_rwkv_a1,
        l3_rwkv_a2, l3_rwkv_g1, l3_rwkv_g2, l3_rwkv_k_k, l3_rwkv_k_a, l3_rwkv_r_k, l3_rwkv_ln_g, l3_rwkv_ln_b)
    y = _moe_ffn(x, h, top_e, top_p, mods[3], lay, router[3], l3_moe_w_gate, l3_moe_w_up, l3_moe_w_down,
                 (norm_out, None, None, F32, False))
    y_prompt = y[:lay.n_ctx].reshape(n_ctx_seq, ctx_len, d)
    y_sample = y[lay.n_ctx:].reshape(n_lat_seq, lat_len, d)
    return (y_prompt, y_sample, ckv_l0, krope_l0, wkv_fwd, wkv_bwd)
```
